```python
import math
import jax
import jax.numpy as jnp
from jax import lax
import numpy as np


D_MODEL = 1024
BATCH = 16
SEQ = 4096
DEPTH = 2

GRID_W = 64
CTX_LEN = 256
N_MOD = 6
NORM_EPS = 1e-6
NEG_INF = -1e30

FNET_GROUPS = 4
FNET_GROUP_DIM = D_MODEL // 8
FNET_WIDTH = FNET_GROUPS * FNET_GROUP_DIM
HY_WIDTH = D_MODEL // 2
HY_EMB_BANDS = 16
HY_EMB_DIM = 1 + 2 * HY_EMB_BANDS
HY_FILTER_HIDDEN = 64
HY_SHORT_CONV = 3
HY_DECAY_TARGET = 1e-2
HY_FAST_DECAY = 0.3
HY_SLOW_DECAY = 1.5
EVEN_IN_WIDTH = FNET_WIDTH + 3 * HY_WIDTH
EVEN_MIX_WIDTH = FNET_WIDTH + HY_WIDTH

HEAD_DIM = 64
N_HEADS = D_MODEL // HEAD_DIM
N_KV_HEADS = 4
GROUP = N_HEADS // N_KV_HEADS
Q_WIDTH = N_HEADS * HEAD_DIM
KV_WIDTH = N_KV_HEADS * HEAD_DIM
WINDOW = 128
BLOCK_Q = 128
ROPE_THETA = 10000.0
AXIS_ROPE_DIM = HEAD_DIM // 2

D_FF = 7 * D_MODEL // 2
N_EXPERTS = 8
TOP_K = 2
EXPERT_BLOCK = 256

kernel_name = 'hybrid_fnet_hyena_swa_moe_dit'


def _rmsnorm(t, g):
    t32 = t.astype(jnp.float32)
    y = t32 * lax.rsqrt(jnp.mean(t32 * t32, axis=-1, keepdims=True) + NORM_EPS)
    return (y * g.astype(jnp.float32)).astype(t.dtype)


def _modulate(h, shift, scale):
    return h * (1 + scale) + shift


def _swiglu(h, wg, wu, wd):
    return (jax.nn.silu(h @ wg) * (h @ wu)) @ wd


def _short_conv(u, w, b):
    up = jnp.pad(u, ((0, 0), (1, 1), (0, 0)))
    return up[:, :-2] * w[0] + up[:, 1:-1] * w[1] + up[:, 2:] * w[2] + b


def _hyena_filter_spectrum(L, fw0, fb0, fw1, fb1, fw2, fb2, fw3, freq):
    f32 = jnp.float32
    pos = jnp.arange(L, dtype=f32)
    t = pos / max(L - 1, 1)
    w = (2.0 * math.pi / L) * pos
    bands = jnp.linspace(1e-4, HY_EMB_BANDS - 1, HY_EMB_BANDS, dtype=f32)
    ang = w[:, None] * bands[None, :]
    z = jnp.concatenate([t[:, None], jnp.cos(ang), -jnp.sin(ang)], axis=-1)
    fr = freq.astype(f32)
    h = jnp.sin(fr * (z @ fw0.astype(f32) + fb0.astype(f32)))
    h = jnp.sin(fr * (h @ fw1.astype(f32) + fb1.astype(f32)))
    h = jnp.sin(fr * (h @ fw2.astype(f32) + fb2.astype(f32)))
    h = (h @ fw3.astype(f32)).reshape(L, 2, HY_WIDTH)
    deltas = jnp.abs(jnp.linspace(math.log(HY_DECAY_TARGET) / HY_SLOW_DECAY,
                                  math.log(HY_DECAY_TARGET) / HY_FAST_DECAY, HY_WIDTH, dtype=f32))
    h = h * jnp.exp(-t[:, None, None] * deltas)
    h_fwd, h_bwd = h[:, 0], h[:, 1]
    k = jnp.concatenate([h_fwd, jnp.zeros((1, HY_WIDTH), f32), h_bwd[1:][::-1]], axis=0)
    k = k * lax.rsqrt(jnp.sum(k * k, axis=0, keepdims=True) + NORM_EPS)
    return jnp.fft.rfft(k, axis=0)


def _fft_long_conv(u, k_spec, bias):
    L = u.shape[1]
    y = jnp.fft.irfft(jnp.fft.rfft(u, n=2 * L, axis=1) * k_spec[None], n=2 * L, axis=1)[:, :L]
    return y + u * bias


def _fourier_mix(u_a):
    b, L, _ = u_a.shape
    ua = u_a.astype(jnp.float32).reshape(b, L, FNET_GROUPS, FNET_GROUP_DIM)
    return jnp.fft.fft2(ua, axes=(1, 3), norm='ortho').real.reshape(b, L, FNET_WIDTH)


def _hyena_mix(u_b, conv_w, conv_b, bias, fw0, fb0, fw1, fb1, fw2, fb2, fw3, freq):
    f32 = jnp.float32
    L = u_b.shape[1]
    z = _short_conv(u_b.astype(f32), conv_w.astype(f32), conv_b.astype(f32))
    v, x1, x0 = jnp.split(z, 3, axis=-1)
    k_spec = _hyena_filter_spectrum(L, fw0, fb0, fw1, fb1, fw2, fb2, fw3, freq)
    return x0 * _fft_long_conv(v * x1, k_spec, bias.astype(f32))


def _even_mixer(h, w_in, w_out, conv_w, conv_b, hy_bias, fw0, fb0, fw1, fb1, fw2, fb2, fw3, freq):
    u = h @ w_in
    a = _fourier_mix(u[..., :FNET_WIDTH])
    b = _hyena_mix(u[..., FNET_WIDTH:], conv_w, conv_b, hy_bias, fw0, fb0, fw1, fb1, fw2, fb2, fw3, freq)
    return jnp.concatenate([a, b], axis=-1).astype(h.dtype) @ w_out


def _axial_rope(t, row, col):
    inv = ROPE_THETA ** (-jnp.arange(0, AXIS_ROPE_DIM, 2, dtype=jnp.float32) / AXIS_ROPE_DIM)

    def rot(seg, p):
        ang = p[:, None] * inv[None, :]
        cos = jnp.cos(ang)[None, :, None, :]
        sin = jnp.sin(ang)[None, :, None, :]
        s1, s2 = jnp.split(seg, 2, axis=-1)
        return jnp.concatenate([s1 * cos - s2 * sin, s1 * sin + s2 * cos], axis=-1)

    return jnp.concatenate([rot(t[..., :AXIS_ROPE_DIM], row), rot(t[..., AXIS_ROPE_DIM:], col)], axis=-1)


def _sink_attend(q, keys, values, masks, sink):
    scale = HEAD_DIM ** -0.5
    scores = []
    for k, m in zip(keys, masks):
        s = jnp.einsum('bqhgd,bkhd->bhgqk', q, k) * scale
        if m is not None:
            s = jnp.where(m, s, NEG_INF)
        scores.append(s)
    b, nq = q.shape[0], q.shape[1]
    sink_col = jnp.broadcast_to(sink.astype(jnp.float32).reshape(1, N_KV_HEADS, GROUP, 1, 1),
                                (b, N_KV_HEADS, GROUP, nq, 1))
    p = jax.nn.softmax(jnp.concatenate(scores + [sink_col], axis=-1), axis=-1)
    off = 0
    out = None
    for v in values:
        n = v.shape[1]
        o = jnp.einsum('bhgqk,bkhd->bqhgd', p[..., off:off + n], v)
        out = o if out is None else out + o
        off += n
    return out


def _window_attention(q, k, v, k_ctx, v_ctx, sink):
    b, L = q.shape[0], q.shape[1]
    nb = L // BLOCK_Q
    qb = jnp.moveaxis(q.reshape(b, nb, BLOCK_Q, N_KV_HEADS, GROUP, HEAD_DIM), 1, 0)

    def band(t):
        tp = jnp.pad(t, ((0, 0), (BLOCK_Q, BLOCK_Q), (0, 0), (0, 0)))
        tp = tp.reshape(b, nb + 2, BLOCK_Q, N_KV_HEADS, HEAD_DIM)
        t3 = jnp.concatenate([tp[:, :-2], tp[:, 1:-1], tp[:, 2:]], axis=2)
        return jnp.moveaxis(t3, 1, 0)

    kb, vb = band(k), band(v)
    offs_q = jnp.arange(BLOCK_Q)
    offs_k = jnp.arange(3 * BLOCK_Q) - BLOCK_Q

    def body(args):
        blk, qi, ki, vi = args
        qpos = blk * BLOCK_Q + offs_q
        kpos = blk * BLOCK_Q + offs_k
        m = ((kpos[None, :] >= 0) & (kpos[None, :] < L)
             & (jnp.abs(qpos[:, None] - kpos[None, :]) <= WINDOW))
        return _sink_attend(qi, [ki, k_ctx], [vi, v_ctx], [m, None], sink)

    o = lax.map(body, (jnp.arange(nb), qb, kb, vb))
    return jnp.moveaxis(o, 0, 1).reshape(b, L, Q_WIDTH)


def _heads(t, n):
    return t.astype(jnp.float32).reshape(t.shape[0], t.shape[1], n, HEAD_DIM)


def _odd_mixer(hx, hc, w_qkv, w_out, q_g, k_g, sink, need_ctx_out):
    dt = hx.dtype
    b, L, _ = hx.shape
    lc = hc.shape[1]
    qkv = hx @ w_qkv
    q = _rmsnorm(_heads(qkv[..., :Q_WIDTH], N_HEADS), q_g)
    k = _rmsnorm(_heads(qkv[..., Q_WIDTH:Q_WIDTH + KV_WIDTH], N_KV_HEADS), k_g)
    v = _heads(qkv[..., Q_WIDTH + KV_WIDTH:], N_KV_HEADS)
    rows = L // GRID_W
    row = jnp.repeat(jnp.arange(rows, dtype=jnp.float32), GRID_W)
    col = jnp.tile(jnp.arange(GRID_W, dtype=jnp.float32), rows)
    q = _axial_rope(q, row, col).reshape(b, L, N_KV_HEADS, GROUP, HEAD_DIM)
    k = _axial_rope(k, row, col)
    c_off = 0 if need_ctx_out else Q_WIDTH
    cp = hc @ w_qkv[:, c_off:]
    kc = _rmsnorm(_heads(cp[..., Q_WIDTH - c_off:Q_WIDTH - c_off + KV_WIDTH], N_KV_HEADS), k_g)
    vc = _heads(cp[..., Q_WIDTH - c_off + KV_WIDTH:], N_KV_HEADS)
    o_lat = _window_attention(q, k, v, kc, vc, sink).astype(dt) @ w_out
    o_ctx = None
    if need_ctx_out:
        qc = _rmsnorm(_heads(cp[..., :Q_WIDTH], N_HEADS), q_g).reshape(b, lc, N_KV_HEADS, GROUP, HEAD_DIM)
        o_ctx = _sink_attend(qc, [kc], [vc], [None], sink).reshape(b, lc, Q_WIDTH).astype(dt) @ w_out
    return o_lat, o_ctx


def _moe_swiglu(h, w_router, w_gate, w_up, w_down):
    dt = h.dtype
    b, L, d = h.shape
    n = b * L
    xf = h.reshape(n, d)
    logits = (xf @ w_router).astype(jnp.float32)
    top_val, top_idx = lax.top_k(logits, TOP_K)
    gates = jax.nn.softmax(top_val, axis=-1)
    e_flat = top_idx.reshape(-1)
    t_flat = jnp.repeat(jnp.arange(n, dtype=jnp.int32), TOP_K)
    g_flat = gates.reshape(-1)
    order = jnp.argsort(e_flat)
    e_sorted = e_flat[order]
    counts = jnp.bincount(e_flat, length=N_EXPERTS)
    padded = (counts + EXPERT_BLOCK - 1) // EXPERT_BLOCK * EXPERT_BLOCK
    start = jnp.cumsum(counts) - counts
    pend = jnp.cumsum(padded)
    pstart = pend - padded
    dest = pstart[e_sorted] + jnp.arange(n * TOP_K) - start[e_sorted]
    n_slots = (n * TOP_K + EXPERT_BLOCK - 1) // EXPERT_BLOCK * EXPERT_BLOCK + N_EXPERTS * EXPERT_BLOCK
    n_blocks = n_slots // EXPERT_BLOCK
    tok_buf = jnp.full((n_slots,), n, dtype=jnp.int32).at[dest].set(t_flat[order])
    gate_buf = jnp.zeros((n_slots,), jnp.float32).at[dest].set(g_flat[order])
    block_e = jnp.minimum(jnp.searchsorted(pend, jnp.arange(n_blocks) * EXPERT_BLOCK, side='right'),
                          N_EXPERTS - 1)
    xpad = jnp.concatenate([xf, jnp.zeros((1, d), dt)], axis=0)

    def block(args):
        tok, e = args
        xb = xpad[tok]
        return _swiglu(xb, w_gate[e], w_up[e], w_down[e])

    yb = lax.map(block, (tok_buf.reshape(n_blocks, EXPERT_BLOCK), block_e))
    y = jnp.zeros((n + 1, d), dt).at[tok_buf].add(yb.reshape(n_slots, d) * gate_buf[:, None].astype(dt))
    return y[:n].reshape(b, L, d)


def setup_inputs(seed: int = 0) -> dict:
    key = jax.random.key(seed)
    ks = iter(jax.random.split(key, 40))
    n_ev = (DEPTH + 1) // 2
    n_od = DEPTH // 2
    D = D_MODEL

    def nrm(shape, scale=1.0):
        return jax.random.normal(next(ks), shape, jnp.float32) * scale

    return {
        'x': nrm((BATCH, SEQ, D)),
        'c': nrm((BATCH, D)),
        'ctx': nrm((BATCH, CTX_LEN, D)),
        'c_ctx': nrm((D,)),
        'ada_w': nrm((DEPTH, D, N_MOD * D), 0.5 * D ** -0.5),
        'ada_b': nrm((DEPTH, N_MOD * D), 0.01),
        'norm1_g': 1.0 + nrm((DEPTH, D), 0.02),
        'norm2_g': 1.0 + nrm((DEPTH, D), 0.02),
        'ev_w_in': nrm((n_ev, D, EVEN_IN_WIDTH), D ** -0.5),
        'ev_w_out': nrm((n_ev, EVEN_MIX_WIDTH, D), EVEN_MIX_WIDTH ** -0.5),
        'hy_conv_w': nrm((n_ev, HY_SHORT_CONV, 3 * HY_WIDTH), HY_SHORT_CONV ** -0.5),
        'hy_conv_b': nrm((n_ev, 3 * HY_WIDTH), 0.02),
        'hy_bias': nrm((n_ev, HY_WIDTH), 0.5),
        'hf_w0': nrm((n_ev, HY_EMB_DIM, HY_FILTER_HIDDEN), HY_EMB_DIM ** -0.5),
        'hf_b0': nrm((n_ev, HY_FILTER_HIDDEN), 0.1),
        'hf_w1': nrm((n_ev, HY_FILTER_HIDDEN, HY_FILTER_HIDDEN), HY_FILTER_HIDDEN ** -0.5),
        'hf_b1': nrm((n_ev, HY_FILTER_HIDDEN), 0.1),
        'hf_w2': nrm((n_ev, HY_FILTER_HIDDEN, HY_FILTER_HIDDEN), HY_FILTER_HIDDEN ** -0.5),
        'hf_b2': nrm((n_ev, HY_FILTER_HIDDEN), 0.1),
        'hf_w3': nrm((n_ev, HY_FILTER_HIDDEN, 2 * HY_WIDTH), HY_FILTER_HIDDEN ** -0.5),
        'hf_freq': 1.0 + nrm((n_ev, HY_FILTER_HIDDEN), 0.1),
        'ffn_w_gate': nrm((n_ev, D, D_FF), D ** -0.5),
        'ffn_w_up': nrm((n_ev, D, D_FF), D ** -0.5),
        'ffn_w_down': nrm((n_ev, D_FF, D), D_FF ** -0.5),
        'od_w_qkv': nrm((n_od, D, Q_WIDTH + 2 * KV_WIDTH), D ** -0.5),
        'od_w_out': nrm((n_od, Q_WIDTH, D), Q_WIDTH ** -0.5),
        'q_norm_g': 1.0 + nrm((n_od, HEAD_DIM), 0.02),
        'k_norm_g': 1.0 + nrm((n_od, HEAD_DIM), 0.02),
        'attn_sink': nrm((n_od, N_HEADS), 0.5),
        'moe_router': nrm((n_od, D, N_EXPERTS), D ** -0.5),
        'moe_w_gate': nrm((n_od, N_EXPERTS, D, D_FF), D ** -0.5),
        'moe_w_up': nrm((n_od, N_EXPERTS, D, D_FF), D ** -0.5),
        'moe_w_down': nrm((n_od, N_EXPERTS, D_FF, D), D_FF ** -0.5),
    }


def reference(x, c, ctx, c_ctx, ada_w, ada_b, norm1_g, norm2_g, ev_w_in, ev_w_out,
              hy_conv_w, hy_conv_b, hy_bias, hf_w0, hf_b0, hf_w1, hf_b1, hf_w2, hf_b2, hf_w3, hf_freq,
              ffn_w_gate, ffn_w_up, ffn_w_down, od_w_qkv, od_w_out, q_norm_g, k_norm_g, attn_sink,
              moe_router, moe_w_gate, moe_w_up, moe_w_down):
    for i in range(DEPTH):
        j = i // 2
        last = i == DEPTH - 1
        ml = jnp.split((jax.nn.silu(c) @ ada_w[i] + ada_b[i])[:, None, :], N_MOD, axis=-1)
        mc = jnp.split((jax.nn.silu(c_ctx) @ ada_w[i] + ada_b[i])[None, None, :], N_MOD, axis=-1)
        hx = _modulate(_rmsnorm(x, norm1_g[i]), ml[0], ml[1])
        hc = _modulate(_rmsnorm(ctx, norm1_g[i]), mc[0], mc[1])
        if i % 2 == 0:
            ev = (ev_w_in[j], ev_w_out[j], hy_conv_w[j], hy_conv_b[j], hy_bias[j],
                  hf_w0[j], hf_b0[j], hf_w1[j], hf_b1[j], hf_w2[j], hf_b2[j], hf_w3[j], hf_freq[j])
            x = x + ml[2] * _even_mixer(hx, *ev)
            if not last:
                ctx = ctx + mc[2] * _even_mixer(hc, *ev)
            hx2 = _modulate(_rmsnorm(x, norm2_g[i]), ml[3], ml[4])
            x = x + ml[5] * _swiglu(hx2, ffn_w_gate[j], ffn_w_up[j], ffn_w_down[j])
            if not last:
                hc2 = _modulate(_rmsnorm(ctx, norm2_g[i]), mc[3], mc[4])
                ctx = ctx + mc[5] * _swiglu(hc2, ffn_w_gate[j], ffn_w_up[j], ffn_w_down[j])
        else:
            o_lat, o_ctx = _odd_mixer(hx, hc, od_w_qkv[j], od_w_out[j], q_norm_g[j], k_norm_g[j],
                                      attn_sink[j], not last)
            x = x + ml[2] * o_lat
            if not last:
                ctx = ctx + mc[2] * o_ctx
            hx2 = _modulate(_rmsnorm(x, norm2_g[i]), ml[3], ml[4])
            x = x + ml[5] * _moe_swiglu(hx2, moe_router[j], moe_w_gate[j], moe_w_up[j], moe_w_down[j])
            if not last:
                hc2 = _modulate(_rmsnorm(ctx, norm2_g[i]), mc[3], mc[4])
                ctx = ctx + mc[5] * _moe_swiglu(hc2, moe_router[j], moe_w_gate[j], moe_w_up[j], moe_w_down[j])
    return x
```

```python
import functools
import math

import jax
import jax.numpy as jnp
from jax import lax
from jax.experimental import pallas as pl
from jax.experimental.pallas import tpu as pltpu

F32 = jnp.float32
BF16 = jnp.bfloat16
I32 = jnp.int32

NORM_EPS = 1e-6
NEG_INF = -1e30
N_MOD = 6

FNET_GROUP_DIM = 128
HY_EMB_BANDS = 16
HY_FILTER_HIDDEN = 64
HY_DECAY_TARGET = 1e-2
HY_FAST_DECAY = 0.3
HY_SLOW_DECAY = 1.5

HEAD_DIM = 64
N_KV_HEADS = 4
GRID_W = 64
WINDOW = 128
BLOCK_Q = 128
ROPE_THETA = 10000.0
AXIS_ROPE_DIM = HEAD_DIM // 2
ROPE_HALF = AXIS_ROPE_DIM // 2
N_EXPERTS = 8
TOP_K = 2

LANES = 128
VMEM_LIMIT = 56 * 1024 * 1024
MOE_ROWS = 512


def _cparams(*sem):
    return pltpu.CompilerParams(dimension_semantics=sem, vmem_limit_bytes=VMEM_LIMIT)


def _split(a):
    hi = a.astype(BF16)
    lo = (a - hi.astype(F32)).astype(BF16)
    return hi, lo


def _dot(a, b):
    return jnp.dot(a, b, preferred_element_type=F32)


def _dot3(a, b):
    ah, al = _split(a)
    bh, bl = _split(b)
    return _dot(ah, bh) + _dot(ah, bl) + _dot(al, bh)


def _silu(t):
    return t / (1.0 + jnp.exp(-t))


def _norm_mod(x, g, shift, scale):
    ms = jnp.mean(x * x, axis=-1, keepdims=True)
    y = x * lax.rsqrt(ms + NORM_EPS) * g
    return y * (1.0 + scale) + shift


def _tile(n, pref):
    t = min(n, pref)
    assert n % t == 0, (n, pref)
    return t


def _ada_kernel(c_ref, w_ref, b_ref, o_ref):
    o_ref[0] = _dot3(_silu(c_ref[...]), w_ref[0]) + b_ref[0]


def _ada_vectors(cc, ada_w, ada_b):
    depth, d, n = ada_w.shape
    rows = cc.shape[0]
    tn = _tile(n, 1536)
    return pl.pallas_call(
        _ada_kernel,
        grid=(depth, n // tn),
        in_specs=[pl.BlockSpec((rows, d), lambda l, j: (0, 0)),
                  pl.BlockSpec((1, d, tn), lambda l, j: (l, 0, j)),
                  pl.BlockSpec((1, 1, tn), lambda l, j: (l, 0, j))],
        out_specs=pl.BlockSpec((1, rows, tn), lambda l, j: (l, 0, j)),
        out_shape=jax.ShapeDtypeStruct((depth, rows, n), F32),
        compiler_params=_cparams("arbitrary", "arbitrary"),
    )(cc, ada_w, ada_b.reshape(depth, 1, n))


def _mm3_kernel(a_ref, b_ref, o_ref):
    o_ref[...] = _dot3(a_ref[...], b_ref[...])


def _matmul3(a, b):
    m, _ = a.shape
    n = b.shape[1]
    return pl.pallas_call(_mm3_kernel, out_shape=jax.ShapeDtypeStruct((m, n), F32),
                          compiler_params=_cparams())(a, b)


def _nmm_kernel(x_ref, g_ref, sh_ref, sc_ref, w_ref, o_ref):
    h = _norm_mod(x_ref[0], g_ref[...], sh_ref[0], sc_ref[0])
    o_ref[0] = _dot(h.astype(BF16), w_ref[...]).astype(o_ref.dtype)


def _norm_mod_matmul(x, g, shift, scale, w, out_dtype, tm=512):
    b, l, d = x.shape
    n = w.shape[1]
    tm = _tile(l, tm)
    return pl.pallas_call(
        _nmm_kernel,
        grid=(b, l // tm),
        in_specs=[pl.BlockSpec((1, tm, d), lambda bi, i: (bi, i, 0)),
                  pl.BlockSpec((1, d), lambda bi, i: (0, 0)),
                  pl.BlockSpec((1, 1, d), lambda bi, i: (bi, 0, 0)),
                  pl.BlockSpec((1, 1, d), lambda bi, i: (bi, 0, 0)),
                  pl.BlockSpec((d, n), lambda bi, i: (0, 0))],
        out_specs=pl.BlockSpec((1, tm, n), lambda bi, i: (bi, i, 0)),
        out_shape=jax.ShapeDtypeStruct((b, l, n), out_dtype),
        compiler_params=_cparams("parallel", "parallel"),
    )(x, g.reshape(1, d), shift, scale, w)


def _ldft_kernel(w_ref, r_ref, o_ref, acc_ref, *, nk):
    k = pl.program_id(2)

    @pl.when(k == 0)
    def _():
        acc_ref[...] = jnp.zeros_like(acc_ref)

    acc_ref[...] += _dot(w_ref[...], r_ref[0].astype(BF16))

    @pl.when(k == nk - 1)
    def _():
        o_ref[0] = acc_ref[...].astype(o_ref.dtype)


def _ldft_scaled_kernel(w_ref, r_ref, rs_ref, ss_ref, o_ref, acc_ref, *, nk):
    k = pl.program_id(2)

    @pl.when(k == 0)
    def _():
        acc_ref[...] = jnp.zeros_like(acc_ref)

    acc_ref[...] += _dot(w_ref[...], r_ref[0].astype(BF16))

    @pl.when(k == nk - 1)
    def _():
        o_ref[0] = acc_ref[...] * rs_ref[...] * lax.rsqrt(ss_ref[...] + NORM_EPS)


def _left_dft(w, rhs, rhs_map, kdim, c, out_dtype, tm=2048, tk=1024, row_scale=None, col_sumsq=None):
    m = w.shape[0]
    nb = rhs.shape[0]
    tm = _tile(m, tm)
    tk = _tile(kdim, tk)
    nk = kdim // tk
    in_specs = [pl.BlockSpec((tm, tk), lambda b, i, k: (i, k)),
                pl.BlockSpec((1, tk, c), lambda b, i, k: rhs_map(b, k, tk))]
    args = [w, rhs]
    if row_scale is None:
        body = functools.partial(_ldft_kernel, nk=nk)
    else:
        body = functools.partial(_ldft_scaled_kernel, nk=nk)
        in_specs += [pl.BlockSpec((tm, 1), lambda b, i, k: (i, 0)),
                     pl.BlockSpec((1, c), lambda b, i, k: (0, 0))]
        args += [row_scale, col_sumsq]
    return pl.pallas_call(
        body,
        grid=(nb, m // tm, nk),
        in_specs=in_specs,
        out_specs=pl.BlockSpec((1, tm, c), lambda b, i, k: (b, i, 0)),
        out_shape=jax.ShapeDtypeStruct((nb, m, c), out_dtype),
        scratch_shapes=[pltpu.VMEM((tm, c), F32)],
        compiler_params=_cparams("parallel", "parallel", "arbitrary"),
    )(*args)


def _hy_pre_kernel(u_ref, up_ref, un_ref, cw_ref, cb_ref, uo_ref, x0_ref, *, nl, tl, hw):
    i = pl.program_id(1)
    u = u_ref[0].astype(F32)
    halo = up_ref.shape[1]
    prev = up_ref[0].astype(F32)[halo - 1:halo]
    nxt = un_ref[0].astype(F32)[0:1]
    prev = jnp.where(i == 0, 0.0, prev)
    nxt = jnp.where(i == nl - 1, 0.0, nxt)
    rows = lax.broadcasted_iota(I32, u.shape, 0)
    um = jnp.where(rows == 0, prev, pltpu.roll(u, 1, 0))
    up = jnp.where(rows == tl - 1, nxt, pltpu.roll(u, tl - 1, 0))
    cw = cw_ref[...]
    z = um * cw[0:1] + u * cw[1:2] + up * cw[2:3] + cb_ref[...]
    uo_ref[0] = (z[:, :hw] * z[:, hw:2 * hw]).astype(uo_ref.dtype)
    x0_ref[0] = z[:, 2 * hw:].astype(x0_ref.dtype)


def _hyena_pre(u, conv_w, conv_b, hw, tl=512):
    b, l, _ = u.shape
    tl = _tile(l, tl)
    nl = l // tl
    halo = 16
    hb = tl // halo
    nh = l // halo
    w3 = 3 * hw
    return pl.pallas_call(
        functools.partial(_hy_pre_kernel, nl=nl, tl=tl, hw=hw),
        grid=(b, nl),
        in_specs=[pl.BlockSpec((1, tl, w3), lambda bi, i: (bi, i, 0)),
                  pl.BlockSpec((1, halo, w3), lambda bi, i: (bi, jnp.maximum(i * hb - 1, 0), 0)),
                  pl.BlockSpec((1, halo, w3), lambda bi, i: (bi, jnp.minimum((i + 1) * hb, nh - 1), 0)),
                  pl.BlockSpec((3, w3), lambda bi, i: (0, 0)),
                  pl.BlockSpec((1, w3), lambda bi, i: (0, 0))],
        out_specs=[pl.BlockSpec((1, tl, hw), lambda bi, i: (bi, i, 0)),
                   pl.BlockSpec((1, tl, hw), lambda bi, i: (bi, i, 0))],
        out_shape=[jax.ShapeDtypeStruct((b, l, hw), BF16), jax.ShapeDtypeStruct((b, l, hw), BF16)],
        compiler_params=_cparams("parallel", "parallel"),
    )(u, u, u, conv_w, conv_b.reshape(1, w3))


def _hy_filter_kernel(z_ref, aux_ref, w0_ref, b0_ref, w1_ref, b1_ref, w2_ref, b2_ref, w3_ref, fr_ref, dl_ref,
                      k_ref, ss_ref, *, hw):
    i = pl.program_id(0)
    fr = fr_ref[...]
    h = jnp.sin(fr * (_dot3(z_ref[...], w0_ref[...]) + b0_ref[...]))
    h = jnp.sin(fr * (_dot3(h, w1_ref[...]) + b1_ref[...]))
    h = jnp.sin(fr * (_dot3(h, w2_ref[...]) + b2_ref[...]))
    h = _dot3(h, w3_ref[...])
    aux = aux_ref[...]
    t, m_fwd, m_bwd = aux[:, 0:1], aux[:, 1:2], aux[:, 2:3]
    k = (h[:, :hw] * m_fwd + h[:, hw:] * m_bwd) * jnp.exp(-t * dl_ref[...])
    k_ref[...] = k

    @pl.when(i == 0)
    def _():
        ss_ref[...] = jnp.zeros_like(ss_ref)

    ss_ref[...] += jnp.sum(k * k, axis=0, keepdims=True)


def _hyena_filter(l, hw, fw0, fb0, fw1, fb1, fw2, fb2, fw3, freq):
    n = 2 * l
    hid = HY_FILTER_HIDDEN
    r = jnp.arange(n)
    pos = jnp.where(r < l, r, n - r).astype(F32)
    t = pos / max(l - 1, 1)
    wv = (2.0 * math.pi / l) * pos
    bands = jnp.linspace(1e-4, HY_EMB_BANDS - 1, HY_EMB_BANDS, dtype=F32)
    ang = wv[:, None] * bands[None, :]
    ztab = jnp.concatenate([t[:, None], jnp.cos(ang), -jnp.sin(ang),
                            jnp.zeros((n, hid - 1 - 2 * HY_EMB_BANDS), F32)], axis=-1)
    aux = jnp.zeros((n, LANES), F32)
    aux = aux.at[:, 0].set(t).at[:, 1].set((r < l).astype(F32)).at[:, 2].set((r > l).astype(F32))
    w0 = jnp.concatenate([fw0, jnp.zeros((hid - fw0.shape[0], hid), F32)], axis=0)
    deltas = jnp.abs(jnp.linspace(math.log(HY_DECAY_TARGET) / HY_SLOW_DECAY,
                                  math.log(HY_DECAY_TARGET) / HY_FAST_DECAY, hw, dtype=F32)).reshape(1, hw)
    tr = _tile(n, 1024)
    full = lambda shape: pl.BlockSpec(shape, lambda i: (0, 0))
    return pl.pallas_call(
        functools.partial(_hy_filter_kernel, hw=hw),
        grid=(n // tr,),
        in_specs=[pl.BlockSpec((tr, hid), lambda i: (i, 0)),
                  pl.BlockSpec((tr, LANES), lambda i: (i, 0)),
                  full((hid, hid)), full((1, hid)), full((hid, hid)), full((1, hid)),
                  full((hid, hid)), full((1, hid)), full((hid, 2 * hw)), full((1, hid)), full((1, hw))],
        out_specs=[pl.BlockSpec((tr, hw), lambda i: (i, 0)), full((1, hw))],
        out_shape=[jax.ShapeDtypeStruct((n, hw), F32), jax.ShapeDtypeStruct((1, hw), F32)],
        compiler_params=_cparams("arbitrary"),
    )(ztab, aux, w0, fb0.reshape(1, hid), fw1, fb1.reshape(1, hid), fw2, fb2.reshape(1, hid), fw3,
      freq.reshape(1, hid), deltas)


def _hy_inv_kernel(wr_ref, wi_ref, ur_ref, ui_ref, kr_ref, ki_ref, u_ref, x0_ref, bias_ref, o_ref, acc_ref, *, nf):
    f = pl.program_id(2)

    @pl.when(f == 0)
    def _():
        acc_ref[...] = jnp.zeros_like(acc_ref)

    ur, ui, kr, ki = ur_ref[0], ui_ref[0], kr_ref[0], ki_ref[0]
    packed = jnp.logical_and(lax.broadcasted_iota(I32, ur.shape, 0) == 0, f == 0)
    yr = jnp.where(packed, ur * kr, ur * kr - ui * ki)
    yi = jnp.where(packed, ui * ki, ur * ki + ui * kr)
    acc_ref[...] += _dot(wr_ref[...], yr.astype(BF16)) + _dot(wi_ref[...], yi.astype(BF16))

    @pl.when(f == nf - 1)
    def _():
        u = u_ref[0].astype(F32)
        o_ref[0] = (x0_ref[0].astype(F32) * (acc_ref[...] + u * bias_ref[...])).astype(o_ref.dtype)


def _hyena_inverse(winv, uhat, kspec, u, x0, bias, tm=2048, tf=512):
    b, l, c = u.shape
    nfreq = uhat.shape[1] // 2
    tm = _tile(l, tm)
    tf = _tile(nfreq, tf)
    nf = nfreq // tf
    return pl.pallas_call(
        functools.partial(_hy_inv_kernel, nf=nf),
        grid=(b, l // tm, nf),
        in_specs=[pl.BlockSpec((tm, tf), lambda bi, i, f: (i, f)),
                  pl.BlockSpec((tm, tf), lambda bi, i, f: (i, nf + f)),
                  pl.BlockSpec((1, tf, c), lambda bi, i, f: (bi, f, 0)),
                  pl.BlockSpec((1, tf, c), lambda bi, i, f: (bi, nf + f, 0)),
                  pl.BlockSpec((1, tf, c), lambda bi, i, f: (0, f, 0)),
                  pl.BlockSpec((1, tf, c), lambda bi, i, f: (0, nf + f, 0)),
                  pl.BlockSpec((1, tm, c), lambda bi, i, f: (bi, i, 0)),
                  pl.BlockSpec((1, tm, c), lambda bi, i, f: (bi, i, 0)),
                  pl.BlockSpec((1, c), lambda bi, i, f: (0, 0))],
        out_specs=pl.BlockSpec((1, tm, c), lambda bi, i, f: (bi, i, 0)),
        out_shape=jax.ShapeDtypeStruct((b, l, c), BF16),
        scratch_shapes=[pltpu.VMEM((tm, c), F32)],
        compiler_params=_cparams("parallel", "parallel", "arbitrary"),
    )(winv, winv, uhat, uhat, kspec, kspec, u, x0, bias.reshape(1, c))


def _proj_res_kernel(x_ref, a_ref, b_ref, w_ref, gate_ref, o_ref, *, ka):
    y = _dot(a_ref[0], w_ref[:ka]) + _dot(b_ref[0], w_ref[ka:])
    o_ref[0] = x_ref[0] + gate_ref[0] * y


def _proj_residual(x, a, a_blk, b2, b_blk, ka, w, gate, tm=512):
    b, l, d = x.shape
    tm = _tile(l, tm)
    return pl.pallas_call(
        functools.partial(_proj_res_kernel, ka=ka),
        grid=(b, l // tm),
        in_specs=[pl.BlockSpec((1, tm, d), lambda bi, i: (bi, i, 0)),
                  pl.BlockSpec((1, tm, ka), lambda bi, i: (bi, i, a_blk)),
                  pl.BlockSpec((1, tm, ka), lambda bi, i: (bi, i, b_blk)),
                  pl.BlockSpec(w.shape, lambda bi, i: (0, 0)),
                  pl.BlockSpec((1, 1, d), lambda bi, i: (bi, 0, 0))],
        out_specs=pl.BlockSpec((1, tm, d), lambda bi, i: (bi, i, 0)),
        out_shape=jax.ShapeDtypeStruct((b, l, d), F32),
        compiler_params=_cparams("parallel", "parallel"),
    )(x, a, b2, w, gate)


def _ffn_kernel(x_ref, g_ref, sh_ref, sc_ref, gate_ref, wg_ref, wu_ref, wd_ref, o_ref, h_ref, acc_ref, *, nf):
    f = pl.program_id(2)

    @pl.when(f == 0)
    def _():
        h_ref[...] = _norm_mod(x_ref[0], g_ref[...], sh_ref[0], sc_ref[0]).astype(BF16)
        acc_ref[...] = jnp.zeros_like(acc_ref)

    h = h_ref[...]
    mid = _silu(_dot(h, wg_ref[...])) * _dot(h, wu_ref[...])
    acc_ref[...] += _dot(mid.astype(BF16), wd_ref[...])

    @pl.when(f == nf - 1)
    def _():
        o_ref[0] = x_ref[0] + gate_ref[0] * acc_ref[...]


def _dense_ffn(x, g, shift, scale, gate, wg, wu, wd, tm=1024, tf=512):
    b, l, d = x.shape
    dff = wg.shape[1]
    tm = _tile(l, tm)
    tf = _tile(dff, tf)
    nf = dff // tf
    vec = pl.BlockSpec((1, 1, d), lambda bi, i, f: (bi, 0, 0))
    return pl.pallas_call(
        functools.partial(_ffn_kernel, nf=nf),
        grid=(b, l // tm, nf),
        in_specs=[pl.BlockSpec((1, tm, d), lambda bi, i, f: (bi, i, 0)),
                  pl.BlockSpec((1, d), lambda bi, i, f: (0, 0)),
                  vec, vec, vec,
                  pl.BlockSpec((d, tf), lambda bi, i, f: (0, f)),
                  pl.BlockSpec((d, tf), lambda bi, i, f: (0, f)),
                  pl.BlockSpec((tf, d), lambda bi, i, f: (f, 0))],
        out_specs=pl.BlockSpec((1, tm, d), lambda bi, i, f: (bi, i, 0)),
        out_shape=jax.ShapeDtypeStruct((b, l, d), F32),
        scratch_shapes=[pltpu.VMEM((tm, d), BF16), pltpu.VMEM((tm, d), F32)],
        compiler_params=_cparams("parallel", "parallel", "arbitrary"),
    )(x, g.reshape(1, d), shift, scale, gate, wg, wu, wd)


def _head_norm(t, e, et, g_full):
    ss = _dot((t * t).astype(BF16), e)
    rinv = lax.rsqrt(ss * (1.0 / HEAD_DIM) + NORM_EPS)
    hi, lo = _split(rinv)
    return t * (_dot(hi, et) + _dot(lo, et)) * g_full


def _rope(t, cos, sin_lo, sin_hi):
    w = t.shape[1]
    rep = w // LANES
    tile = lambda a: jnp.concatenate([a] * rep, axis=1)
    return (t * tile(cos) + pltpu.roll(t, w - ROPE_HALF, 1) * tile(sin_lo)
            + pltpu.roll(t, ROPE_HALF, 1) * tile(sin_hi))


def _qkv_kernel(x_ref, g_ref, sh_ref, sc_ref, w_ref, e_ref, et_ref, qg_ref, kg_ref, cos_ref, sl_ref, sh2_ref,
                *out_refs, qw, kw, rope):
    k_ref, v_ref = out_refs[-2:]
    h = _norm_mod(x_ref[0], g_ref[...], sh_ref[0], sc_ref[0])
    t = _dot(h.astype(BF16), w_ref[...])
    e, et = e_ref[...], et_ref[...]
    k = _head_norm(t[:, qw:qw + kw], e[:kw], et[:, :kw], kg_ref[...])
    if rope:
        k = _rope(k, cos_ref[...], sl_ref[...], sh2_ref[...])
    k_ref[0] = k.astype(BF16)
    v_ref[0] = t[:, qw + kw:].astype(BF16)
    if qw:
        q = _head_norm(t[:, :qw], e, et, qg_ref[...])
        q = _rope(q, cos_ref[...], sl_ref[...], sh2_ref[...]) * (HEAD_DIM ** -0.5)
        out_refs[0][0] = q.astype(BF16)


def _head_tables(qw):
    lane = jnp.arange(qw)
    e = (lane[:, None] // HEAD_DIM == jnp.arange(LANES)[None, :]).astype(BF16)
    return e, e.T


def _rope_tables(l):
    rows = l // GRID_W
    row = jnp.repeat(jnp.arange(rows, dtype=F32), GRID_W)
    col = jnp.tile(jnp.arange(GRID_W, dtype=F32), rows)
    inv = ROPE_THETA ** (-jnp.arange(0, AXIS_ROPE_DIM, 2, dtype=F32) / AXIS_ROPE_DIM)
    lane = jnp.arange(LANES)
    in_head = lane % HEAD_DIM
    use_col = (in_head // AXIS_ROPE_DIM) == 1
    hi_half = ((in_head % AXIS_ROPE_DIM) // ROPE_HALF) == 1
    freq = inv[in_head % ROPE_HALF]
    ang = jnp.where(use_col[None, :], col[:, None], row[:, None]) * freq[None, :]
    cos, sin = jnp.cos(ang), jnp.sin(ang)
    return cos, jnp.where(hi_half[None, :], 0.0, -sin), jnp.where(hi_half[None, :], sin, 0.0)


def _qkv_project(x, g, shift, scale, w, q_g, k_g, qw, kw, rope, tm=512):
    b, l, d = x.shape
    tm = _tile(l, tm)
    e, et = _head_tables(max(qw, kw))
    n_q = max(qw, kw) // HEAD_DIM
    qg = jnp.tile(q_g, n_q).reshape(1, -1)
    kg = jnp.tile(k_g, kw // HEAD_DIM).reshape(1, kw)
    if rope:
        cos, s_lo, s_hi = _rope_tables(l)
    else:
        cos = s_lo = s_hi = jnp.zeros((l, LANES), F32)
    vec = pl.BlockSpec((1, 1, d), lambda bi, i: (bi, 0, 0))
    full = lambda a: pl.BlockSpec(a.shape, lambda bi, i: (0,) * a.ndim)
    tab = pl.BlockSpec((tm, LANES), lambda bi, i: (i, 0))
    widths = ([qw] if qw else []) + [kw, kw]
    return pl.pallas_call(
        functools.partial(_qkv_kernel, qw=qw, kw=kw, rope=rope),
        grid=(b, l // tm),
        in_specs=[pl.BlockSpec((1, tm, d), lambda bi, i: (bi, i, 0)),
                  pl.BlockSpec((1, d), lambda bi, i: (0, 0)), vec, vec,
                  full(w), full(e), full(et), full(qg), full(kg), tab, tab, tab],
        out_specs=[pl.BlockSpec((1, tm, n), lambda bi, i: (bi, i, 0)) for n in widths],
        out_shape=[jax.ShapeDtypeStruct((b, l, n), BF16) for n in widths],
        compiler_params=_cparams("parallel", "parallel"),
    )(x, g.reshape(1, d), shift, scale, w, e, et, qg, kg, cos, s_lo, s_hi)


def _attn_kernel(q_ref, kp_ref, kc_ref, kn_ref, vp_ref, vc_ref, vn_ref, kx_ref, vx_ref, sink_ref, o_ref, *,
                 seq, group):
    qb = pl.program_id(1)
    bq = q_ref.shape[1]
    lc = kx_ref.shape[1]
    nkeys = 3 * bq + lc
    rows = lax.broadcasted_iota(I32, (group * bq, nkeys), 0)
    cols = lax.broadcasted_iota(I32, (group * bq, nkeys), 1)
    qpos = qb * bq + rows % bq
    kpos = (qb - 1) * bq + cols
    valid = jnp.logical_or(
        cols >= 3 * bq,
        jnp.logical_and(jnp.logical_and(kpos >= 0, kpos < seq), jnp.abs(qpos - kpos) <= WINDOW))
    for h in range(N_KV_HEADS):
        ks = slice(h * HEAD_DIM, (h + 1) * HEAD_DIM)
        kh = jnp.concatenate([kp_ref[0, :, ks], kc_ref[0, :, ks], kn_ref[0, :, ks], kx_ref[0, :, ks]], axis=0)
        vh = jnp.concatenate([vp_ref[0, :, ks], vc_ref[0, :, ks], vn_ref[0, :, ks], vx_ref[0, :, ks]], axis=0)
        qh = jnp.concatenate(
            [q_ref[0, :, (h * group + g) * HEAD_DIM:(h * group + g + 1) * HEAD_DIM] for g in range(group)], axis=0)
        s = lax.dot_general(qh, kh, (((1,), (1,)), ((), ())), preferred_element_type=F32)
        s = jnp.where(valid, s, NEG_INF)
        sk = sink_ref[h]
        m = jnp.maximum(jnp.max(s, axis=1, keepdims=True), sk)
        p = jnp.exp(s - m)
        den = jnp.sum(p, axis=1, keepdims=True) + jnp.exp(sk - m)
        o = _dot(p.astype(BF16), vh) / den
        for g in range(group):
            hq = h * group + g
            o_ref[0, :, hq * HEAD_DIM:(hq + 1) * HEAD_DIM] = o[g * bq:(g + 1) * bq].astype(o_ref.dtype)


def _window_attention(q, k, v, kx, vx, sink):
    b, l, qw = q.shape
    kw = k.shape[2]
    lc = kx.shape[1]
    bq = BLOCK_Q
    nb = l // bq
    group = qw // kw
    sink_tab = jnp.repeat(sink.astype(F32).reshape(N_KV_HEADS, group), bq, axis=1)[..., None]
    kv_prev = pl.BlockSpec((1, bq, kw), lambda bi, i: (bi, jnp.maximum(i - 1, 0), 0))
    kv_cur = pl.BlockSpec((1, bq, kw), lambda bi, i: (bi, i, 0))
    kv_next = pl.BlockSpec((1, bq, kw), lambda bi, i: (bi, jnp.minimum(i + 1, nb - 1), 0))
    kv_ctx = pl.BlockSpec((1, lc, kw), lambda bi, i: (bi, 0, 0))
    return pl.pallas_call(
        functools.partial(_attn_kernel, seq=l, group=group),
        grid=(b, nb),
        in_specs=[pl.BlockSpec((1, bq, qw), lambda bi, i: (bi, i, 0)),
                  kv_prev, kv_cur, kv_next, kv_prev, kv_cur, kv_next, kv_ctx, kv_ctx,
                  pl.BlockSpec(sink_tab.shape, lambda bi, i: (0, 0, 0))],
        out_specs=pl.BlockSpec((1, bq, qw), lambda bi, i: (bi, i, 0)),
        out_shape=jax.ShapeDtypeStruct((b, l, qw), BF16),
        compiler_params=_cparams("parallel", "parallel"),
    )(q, k, k, k, v, v, v, kx, vx, sink_tab)


def _router_kernel(x_ref, g_ref, sh_ref, sc_ref, wh_ref, wl_ref, h_ref, idx_ref, gate_ref):
    h = _norm_mod(x_ref[0], g_ref[...], sh_ref[0], sc_ref[0])
    h_ref[0] = h
    hi, lo = _split(h)
    logits = _dot(hi, wh_ref[...]) + _dot(hi, wl_ref[...]) + _dot(lo, wh_ref[...])
    lane = lax.broadcasted_iota(I32, logits.shape, 1)
    logits = jnp.where(lane < N_EXPERTS, logits, -jnp.inf)
    m1 = jnp.max(logits, axis=1, keepdims=True)
    i1 = jnp.min(jnp.where(logits == m1, lane, LANES), axis=1, keepdims=True)
    rest = jnp.where(lane == i1, -jnp.inf, logits)
    m2 = jnp.max(rest, axis=1, keepdims=True)
    i2 = jnp.min(jnp.where(rest == m2, lane, LANES), axis=1, keepdims=True)
    e = jnp.exp(m2 - m1)
    g1 = 1.0 / (1.0 + e)
    g2 = e / (1.0 + e)
    idx_ref[0] = jnp.where(lane == 0, i1, jnp.where(lane == 1, i2, 0))
    gate_ref[0] = jnp.where(lane == 0, g1, jnp.where(lane == 1, g2, 0.0))


def _route(x, g, shift, scale, w_router, tm=512):
    b, l, d = x.shape
    tm = _tile(l, tm)
    wr = jnp.concatenate([w_router, jnp.zeros((d, LANES - w_router.shape[1]), F32)], axis=1)
    wh, wl = _split(wr)
    vec = pl.BlockSpec((1, 1, d), lambda bi, i: (bi, 0, 0))
    return pl.pallas_call(
        _router_kernel,
        grid=(b, l // tm),
        in_specs=[pl.BlockSpec((1, tm, d), lambda bi, i: (bi, i, 0)),
                  pl.BlockSpec((1, d), lambda bi, i: (0, 0)), vec, vec,
                  pl.BlockSpec((d, LANES), lambda bi, i: (0, 0)),
                  pl.BlockSpec((d, LANES), lambda bi, i: (0, 0))],
        out_specs=[pl.BlockSpec((1, tm, d), lambda bi, i: (bi, i, 0)),
                   pl.BlockSpec((1, tm, LANES), lambda bi, i: (bi, i, 0)),
                   pl.BlockSpec((1, tm, LANES), lambda bi, i: (bi, i, 0))],
        out_shape=[jax.ShapeDtypeStruct((b, l, d), F32),
                   jax.ShapeDtypeStruct((b, l, LANES), I32),
                   jax.ShapeDtypeStruct((b, l, LANES), F32)],
        compiler_params=_cparams("parallel", "parallel"),
    )(x, g.reshape(1, d), shift, scale, wh, wl)


def _row_copy(src_ref, dst_ref, src_row, dst_row, sem):
    return pltpu.make_async_copy(src_ref.at[pl.ds(src_row, 1)], dst_ref.at[pl.ds(dst_row, 1)], sem)


def _gather_kernel(ids_ref, src_ref, o_ref, sem, *, rows):
    def start(r, c):
        _row_copy(src_ref, o_ref, ids_ref[0, 0, r], r, sem).start()
        return c

    def wait(r, c):
        _row_copy(src_ref, o_ref, ids_ref[0, 0, r], r, sem).wait()
        return c

    lax.fori_loop(0, rows, start, 0)
    lax.fori_loop(0, rows, wait, 0)


def _gather_rows(src, ids, rows):
    s = ids.shape[0]
    d = src.shape[1]
    nblk = s // rows
    return pl.pallas_call(
        functools.partial(_gather_kernel, rows=rows),
        grid=(nblk,),
        in_specs=[pl.BlockSpec((1, 1, rows), lambda i: (i, 0, 0), memory_space=pltpu.SMEM),
                  pl.BlockSpec(memory_space=pl.ANY)],
        out_specs=pl.BlockSpec((rows, d), lambda i: (i, 0)),
        out_shape=jax.ShapeDtypeStruct((s, d), src.dtype),
        scratch_shapes=[pltpu.SemaphoreType.DMA(())],
        compiler_params=_cparams("arbitrary"),
    )(ids.reshape(nblk, 1, rows), src)


def _moe_kernel(be_ref, nu_ref, x_ref, wg_ref, wu_ref, wd_ref, o_ref, acc_ref, *, nf):
    i = pl.program_id(0)
    f = pl.program_id(1)
    used = i < nu_ref[0]

    @pl.when(jnp.logical_and(used, f == 0))
    def _():
        acc_ref[...] = jnp.zeros_like(acc_ref)

    @pl.when(used)
    def _():
        x = x_ref[...].astype(BF16)
        mid = _silu(_dot(x, wg_ref[0])) * _dot(x, wu_ref[0])
        acc_ref[...] += _dot(mid.astype(BF16), wd_ref[0])

    @pl.when(jnp.logical_and(used, f == nf - 1))
    def _():
        o_ref[...] = acc_ref[...]

    @pl.when(jnp.logical_and(jnp.logical_not(used), f == nf - 1))
    def _():
        o_ref[...] = jnp.zeros_like(o_ref)


def _expert_ffn(xs, block_e, n_used, wg, wu, wd, rows, tf=512):
    s, d = xs.shape
    dff = wg.shape[2]
    tf = _tile(dff, tf)
    nf = dff // tf
    nblk = s // rows
    grid_spec = pltpu.PrefetchScalarGridSpec(
        num_scalar_prefetch=2,
        grid=(nblk, nf),
        in_specs=[pl.BlockSpec((rows, d), lambda i, f, be, nu: (i, 0)),
                  pl.BlockSpec((1, d, tf), lambda i, f, be, nu: (be[i], 0, f)),
                  pl.BlockSpec((1, d, tf), lambda i, f, be, nu: (be[i], 0, f)),
                  pl.BlockSpec((1, tf, d), lambda i, f, be, nu: (be[i], f, 0))],
        out_specs=pl.BlockSpec((rows, d), lambda i, f, be, nu: (i, 0)),
        scratch_shapes=[pltpu.VMEM((rows, d), F32)],
    )
    return pl.pallas_call(
        functools.partial(_moe_kernel, nf=nf),
        grid_spec=grid_spec,
        out_shape=jax.ShapeDtypeStruct((s, d), F32),
        compiler_params=_cparams("arbitrary", "arbitrary"),
    )(block_e, n_used, xs, wg, wu, wd)


def _combine_kernel(p0_ref, p1_ref, x_ref, gate_ref, rg_ref, ys_ref, o_ref, r0_ref, r1_ref, sem, *, rows):
    def start(r, c):
        _row_copy(ys_ref, r0_ref, p0_ref[0, 0, r], r, sem.at[0]).start()
        _row_copy(ys_ref, r1_ref, p1_ref[0, 0, r], r, sem.at[1]).start()
        return c

    def wait(r, c):
        _row_copy(ys_ref, r0_ref, p0_ref[0, 0, r], r, sem.at[0]).wait()
        _row_copy(ys_ref, r1_ref, p1_ref[0, 0, r], r, sem.at[1]).wait()
        return c

    lax.fori_loop(0, rows, start, 0)
    lax.fori_loop(0, rows, wait, 0)
    rg = rg_ref[0]
    y = rg[:, 0:1] * r0_ref[...] + rg[:, 1:2] * r1_ref[...]
    o_ref[0] = x_ref[0] + gate_ref[0] * y


def _moe_combine(x, gate, route_gates, ys, p0, p1, tm=512):
    b, l, d = x.shape
    tm = _tile(l, tm)
    nt = l // tm
    ids = lambda: pl.BlockSpec((1, 1, tm), lambda bi, i: (bi * nt + i, 0, 0), memory_space=pltpu.SMEM)
    return pl.pallas_call(
        functools.partial(_combine_kernel, rows=tm),
        grid=(b, nt),
        in_specs=[ids(), ids(),
                  pl.BlockSpec((1, tm, d), lambda bi, i: (bi, i, 0)),
                  pl.BlockSpec((1, 1, d), lambda bi, i: (bi, 0, 0)),
                  pl.BlockSpec((1, tm, LANES), lambda bi, i: (bi, i, 0)),
                  pl.BlockSpec(memory_space=pl.ANY)],
        out_specs=pl.BlockSpec((1, tm, d), lambda bi, i: (bi, i, 0)),
        out_shape=jax.ShapeDtypeStruct((b, l, d), F32),
        scratch_shapes=[pltpu.VMEM((tm, d), F32), pltpu.VMEM((tm, d), F32), pltpu.SemaphoreType.DMA((2,))],
        compiler_params=_cparams("arbitrary", "arbitrary"),
    )(p0.reshape(b * nt, 1, tm), p1.reshape(b * nt, 1, tm), x, gate, route_gates, ys)


def _dispatch_tables(expert_ids, rows):
    n = expert_ids.shape[0]
    e_flat = expert_ids.reshape(-1)
    onehot = (e_flat[:, None] == jnp.arange(N_EXPERTS, dtype=I32)[None, :]).astype(I32)
    csum = jnp.cumsum(onehot, axis=0)
    rank = jnp.sum(csum * onehot, axis=1) - 1
    counts = csum[-1]
    padded = (counts + rows - 1) // rows * rows
    pend = jnp.cumsum(padded)
    pstart = pend - padded
    dest = (jnp.sum(pstart[None, :] * onehot, axis=1) + rank).astype(I32)
    n_slots = (n * TOP_K + rows - 1) // rows * rows + N_EXPERTS * rows
    nblk = n_slots // rows
    tok = jnp.repeat(jnp.arange(n, dtype=I32), TOP_K)
    tok_buf = jnp.zeros((n_slots,), I32).at[dest].set(tok)
    block_e = jnp.minimum(jnp.searchsorted(pend, jnp.arange(nblk, dtype=I32) * rows, side='right'),
                          N_EXPERTS - 1).astype(I32)
    n_used = (pend[-1:] // rows).astype(I32)
    dest2 = dest.reshape(n, TOP_K)
    return tok_buf, block_e, n_used, dest2[:, 0], dest2[:, 1]


def _angles(rows, cols, n):
    m = (rows[:, None] * cols[None, :]) % n
    return (2.0 * math.pi / n) * m.astype(F32)


def _fnet_table(l):
    idx = jnp.arange(l, dtype=I32)
    ang = _angles(idx, idx, l)
    s = 1.0 / math.sqrt(l)
    return jnp.concatenate([jnp.cos(ang) * s, jnp.sin(ang) * (-s)], axis=1).astype(BF16)


def _rfft_table(n):
    half = n // 2
    f = jnp.arange(half, dtype=I32)
    t = jnp.arange(n, dtype=I32)
    ang = _angles(f, t, n)
    top = jnp.cos(ang)
    bot = -jnp.sin(ang)
    nyq = jnp.cos(_angles(jnp.full((1,), half, I32), t, n))
    bot = jnp.concatenate([nyq, bot[1:]], axis=0)
    return jnp.concatenate([top, bot], axis=0).astype(BF16)


def _group_dft_table(width):
    gd = FNET_GROUP_DIM
    idx = jnp.arange(gd, dtype=I32)
    ang = _angles(idx, idx, gd)
    s = 1.0 / math.sqrt(gd)
    eye = jnp.eye(width // gd, dtype=F32)
    return jnp.concatenate([jnp.kron(eye, jnp.cos(ang) * s), jnp.kron(eye, jnp.sin(ang) * s)], axis=1)


def _even_mixer(x, g1, shift, scale, gate, w_in_f, w_out, conv_w, conv_b, hy_bias, filt, fw, hw):
    b, l, d = x.shape
    n = 2 * l
    u = _norm_mod_matmul(x, g1, shift, scale, w_in_f, BF16)
    cb = 3 * hw // fw
    a = _left_dft(_fnet_table(l), u, lambda bi, k, tk: (bi, k % (l // tk), cb + k // (l // tk)), n, fw, BF16,
                  tk=min(l, 1024))
    uu, x0 = _hyena_pre(u, conv_w, conv_b, hw)
    wf = _rfft_table(n)
    k_raw, k_ss = filt(l)
    wts = jnp.concatenate([jnp.ones((1,), F32), jnp.full((l - 1,), 2.0, F32)]) / n
    row_scale = jnp.concatenate([wts, wts]).reshape(n, 1)
    kspec = _left_dft(wf, k_raw[None], lambda bi, k, tk: (0, k, 0), n, hw, F32,
                      row_scale=row_scale, col_sumsq=k_ss)
    uhat = _left_dft(wf, uu, lambda bi, k, tk: (bi, k, 0), l, hw, F32)
    winv = wf[:, :l].T
    hy = _hyena_inverse(winv, uhat, kspec, uu, x0, hy_bias)
    return _proj_residual(x, a, 0, hy, 0, fw, w_out, gate)


def kernel(x, c, ctx, c_ctx, ada_w, ada_b, norm1_g, norm2_g, ev_w_in, ev_w_out, hy_conv_w, hy_conv_b, hy_bias,
           hf_w0, hf_b0, hf_w1, hf_b1, hf_w2, hf_b2, hf_w3, hf_freq, ffn_w_gate, ffn_w_up, ffn_w_down, od_w_qkv,
           od_w_out, q_norm_g, k_norm_g, attn_sink, moe_router, moe_w_gate, moe_w_up, moe_w_down):
    b, l, d = x.shape
    lc = ctx.shape[1]
    assert ada_w.shape[0] == 2, "this implementation covers the two-layer (even, odd) stack"
    hw = hy_bias.shape[1]
    fw = ev_w_in.shape[2] - 3 * hw
    qw = od_w_out.shape[1]
    kw = (od_w_qkv.shape[2] - qw) // 2

    rows = (b + 1 + 7) // 8 * 8
    cc = jnp.concatenate([c, c_ctx[None, :], jnp.zeros((rows - b - 1, d), F32)], axis=0)
    mod = _ada_vectors(cc, ada_w, ada_b)
    ml = [[mod[i, :b, None, m * d:(m + 1) * d] for m in range(N_MOD)] for i in range(2)]
    mc = [[mod[i, b:b + 1, None, m * d:(m + 1) * d] for m in range(N_MOD)] for i in range(2)]

    w_in = ev_w_in[0]
    w_fnet = _matmul3(w_in[:, :fw], _group_dft_table(fw))
    w_in_f = jnp.concatenate([w_in[:, fw:], w_fnet], axis=1).astype(BF16)
    w_out0 = ev_w_out[0].astype(BF16)
    filt = lambda seq: _hyena_filter(seq, hw, hf_w0[0], hf_b0[0], hf_w1[0], hf_b1[0], hf_w2[0], hf_b2[0],
                                     hf_w3[0], hf_freq[0])
    wg, wu, wd = ffn_w_gate[0].astype(BF16), ffn_w_up[0].astype(BF16), ffn_w_down[0].astype(BF16)

    x = _even_mixer(x, norm1_g[0], ml[0][0], ml[0][1], ml[0][2], w_in_f, w_out0, hy_conv_w[0], hy_conv_b[0],
                    hy_bias[0], filt, fw, hw)
    x = _dense_ffn(x, norm2_g[0], ml[0][3], ml[0][4], ml[0][5], wg, wu, wd)

    bc = lambda v: jnp.broadcast_to(v, (b, 1, d))
    ctx = _even_mixer(ctx, norm1_g[0], bc(mc[0][0]), bc(mc[0][1]), bc(mc[0][2]), w_in_f, w_out0, hy_conv_w[0],
                      hy_conv_b[0], hy_bias[0], filt, fw, hw)
    ctx = _dense_ffn(ctx.reshape(1, b * lc, d), norm2_g[0], mc[0][3], mc[0][4], mc[0][5], wg, wu, wd)

    w_qkv = od_w_qkv[0].astype(BF16)
    q, k, v = _qkv_project(x, norm1_g[1], ml[1][0], ml[1][1], w_qkv, q_norm_g[0], k_norm_g[0], qw, kw, True)
    kx, vx = _qkv_project(ctx, norm1_g[1], mc[1][0], mc[1][1], w_qkv[:, qw:], q_norm_g[0], k_norm_g[0], 0, kw,
                             False)
    o = _window_attention(q, k, v, kx.reshape(b, lc, kw), vx.reshape(b, lc, kw), attn_sink[0])
    half = qw // 2
    x = _proj_residual(x, o, 0, o, 1, half, od_w_out[0].astype(BF16), ml[1][2])

    h2, idx, gates = _route(x, norm2_g[1], ml[1][3], ml[1][4], moe_router[0])
    tok_buf, block_e, n_used, p0, p1 = _dispatch_tables(idx.reshape(b * l, LANES)[:, :TOP_K], MOE_ROWS)
    xs = _gather_rows(h2.reshape(b * l, d), tok_buf, MOE_ROWS)
    ys = _expert_ffn(xs, block_e, n_used, moe_w_gate[0].astype(BF16), moe_w_up[0].astype(BF16),
                     moe_w_down[0].astype(BF16), MOE_ROWS)
    return _moe_combine(x, ml[1][5], gates, ys, p0, p1)
```

```python
import functools
import math

import jax
import jax.numpy as jnp
from jax import lax
from jax.experimental import pallas as pl
from jax.experimental.pallas import tpu as pltpu

F32 = jnp.float32
BF16 = jnp.bfloat16
I32 = jnp.int32

NORM_EPS = 1e-6
NEG_INF = -1e30
N_MOD = 6

FNET_GROUP_DIM = 128
HY_EMB_BANDS = 16
HY_FILTER_HIDDEN = 64
HY_DECAY_TARGET = 1e-2
HY_FAST_DECAY = 0.3
HY_SLOW_DECAY = 1.5

HEAD_DIM = 64
N_KV_HEADS = 4
GRID_W = 64
WINDOW = 128
BLOCK_Q = 128
ROPE_THETA = 10000.0
AXIS_ROPE_DIM = HEAD_DIM // 2
ROPE_HALF = AXIS_ROPE_DIM // 2
N_EXPERTS = 8
TOP_K = 2

LANES = 128
SUBLANES = 8
VMEM_LIMIT = 56 * 1024 * 1024
MOE_TILE = 512
MOE_ROWS = 512
MOE_CAP = 256


def _cparams(*sem):
    return pltpu.CompilerParams(dimension_semantics=sem, vmem_limit_bytes=VMEM_LIMIT)


def _split(a):
    hi = a.astype(BF16)
    lo = (a - hi.astype(F32)).astype(BF16)
    return hi, lo


def _dot(a, b):
    return jnp.dot(a, b, preferred_element_type=F32)


def _dot3(a, b):
    ah, al = _split(a)
    bh, bl = _split(b)
    return _dot(ah, bh) + _dot(ah, bl) + _dot(al, bh)


def _silu(t):
    return t / (1.0 + jnp.exp(-t))


def _norm_mod(x, g, shift, scale):
    ms = jnp.mean(x * x, axis=-1, keepdims=True)
    y = x * lax.rsqrt(ms + NORM_EPS) * g
    return y * (1.0 + scale) + shift


def _tile(n, pref):
    t = min(n, pref)
    assert n % t == 0, (n, pref)
    return t


def _ada_kernel(c_ref, w_ref, b_ref, o_ref):
    o_ref[0] = _dot3(_silu(c_ref[...]), w_ref[0]) + b_ref[0]


def _ada_vectors(cc, ada_w, ada_b):
    depth, d, n = ada_w.shape
    rows = cc.shape[0]
    tn = _tile(n, 1536)
    return pl.pallas_call(
        _ada_kernel,
        grid=(depth, n // tn),
        in_specs=[pl.BlockSpec((rows, d), lambda l, j: (0, 0)),
                  pl.BlockSpec((1, d, tn), lambda l, j: (l, 0, j)),
                  pl.BlockSpec((1, 1, tn), lambda l, j: (l, 0, j))],
        out_specs=pl.BlockSpec((1, rows, tn), lambda l, j: (l, 0, j)),
        out_shape=jax.ShapeDtypeStruct((depth, rows, n), F32),
        compiler_params=_cparams("arbitrary", "arbitrary"),
    )(cc, ada_w, ada_b.reshape(depth, 1, n))


def _mm3_kernel(a_ref, b_ref, o_ref):
    o_ref[...] = _dot3(a_ref[...], b_ref[...])


def _matmul3(a, b):
    m, _ = a.shape
    n = b.shape[1]
    return pl.pallas_call(_mm3_kernel, out_shape=jax.ShapeDtypeStruct((m, n), F32),
                          compiler_params=_cparams())(a, b)


def _nmm_kernel(x_ref, g_ref, sh_ref, sc_ref, w_ref, o_ref):
    h = _norm_mod(x_ref[0], g_ref[...], sh_ref[0], sc_ref[0])
    o_ref[0] = _dot(h.astype(BF16), w_ref[...]).astype(o_ref.dtype)


def _norm_mod_matmul(x, g, shift, scale, w, out_dtype, tm=512):
    b, l, d = x.shape
    n = w.shape[1]
    tm = _tile(l, tm)
    return pl.pallas_call(
        _nmm_kernel,
        grid=(b, l // tm),
        in_specs=[pl.BlockSpec((1, tm, d), lambda bi, i: (bi, i, 0)),
                  pl.BlockSpec((1, d), lambda bi, i: (0, 0)),
                  pl.BlockSpec((1, 1, d), lambda bi, i: (bi, 0, 0)),
                  pl.BlockSpec((1, 1, d), lambda bi, i: (bi, 0, 0)),
                  pl.BlockSpec((d, n), lambda bi, i: (0, 0))],
        out_specs=pl.BlockSpec((1, tm, n), lambda bi, i: (bi, i, 0)),
        out_shape=jax.ShapeDtypeStruct((b, l, n), out_dtype),
        compiler_params=_cparams("parallel", "parallel"),
    )(x, g.reshape(1, d), shift, scale, w)


def _ldft_kernel(w_ref, r_ref, o_ref, acc_ref, *, nk):
    k = pl.program_id(2)

    @pl.when(k == 0)
    def _():
        acc_ref[...] = jnp.zeros_like(acc_ref)

    acc_ref[...] += _dot(w_ref[...], r_ref[0].astype(BF16))

    @pl.when(k == nk - 1)
    def _():
        o_ref[0] = acc_ref[...].astype(o_ref.dtype)


def _ldft_scaled_kernel(w_ref, r_ref, rs_ref, ss_ref, o_ref, acc_ref, *, nk):
    k = pl.program_id(2)

    @pl.when(k == 0)
    def _():
        acc_ref[...] = jnp.zeros_like(acc_ref)

    acc_ref[...] += _dot(w_ref[...], r_ref[0].astype(BF16))

    @pl.when(k == nk - 1)
    def _():
        o_ref[0] = acc_ref[...] * rs_ref[...] * lax.rsqrt(ss_ref[...] + NORM_EPS)


def _left_dft(w, rhs, rhs_map, kdim, c, out_dtype, tm=2048, tk=1024, row_scale=None, col_sumsq=None):
    m = w.shape[0]
    nb = rhs.shape[0]
    tm = _tile(m, tm)
    tk = _tile(kdim, tk)
    nk = kdim // tk
    in_specs = [pl.BlockSpec((tm, tk), lambda b, i, k: (i, k)),
                pl.BlockSpec((1, tk, c), lambda b, i, k: rhs_map(b, k, tk))]
    args = [w, rhs]
    if row_scale is None:
        body = functools.partial(_ldft_kernel, nk=nk)
    else:
        body = functools.partial(_ldft_scaled_kernel, nk=nk)
        in_specs += [pl.BlockSpec((tm, 1), lambda b, i, k: (i, 0)),
                     pl.BlockSpec((1, c), lambda b, i, k: (0, 0))]
        args += [row_scale, col_sumsq]
    return pl.pallas_call(
        body,
        grid=(nb, m // tm, nk),
        in_specs=in_specs,
        out_specs=pl.BlockSpec((1, tm, c), lambda b, i, k: (b, i, 0)),
        out_shape=jax.ShapeDtypeStruct((nb, m, c), out_dtype),
        scratch_shapes=[pltpu.VMEM((tm, c), F32)],
        compiler_params=_cparams("parallel", "parallel", "arbitrary"),
    )(*args)


def _hy_pre_kernel(u_ref, up_ref, un_ref, cw_ref, cb_ref, uo_ref, x0_ref, *, nl, tl, hw):
    i = pl.program_id(1)
    u = u_ref[0].astype(F32)
    halo = up_ref.shape[1]
    prev = up_ref[0].astype(F32)[halo - 1:halo]
    nxt = un_ref[0].astype(F32)[0:1]
    prev = jnp.where(i == 0, 0.0, prev)
    nxt = jnp.where(i == nl - 1, 0.0, nxt)
    rows = lax.broadcasted_iota(I32, u.shape, 0)
    um = jnp.where(rows == 0, prev, pltpu.roll(u, 1, 0))
    up = jnp.where(rows == tl - 1, nxt, pltpu.roll(u, tl - 1, 0))
    cw = cw_ref[...]
    z = um * cw[0:1] + u * cw[1:2] + up * cw[2:3] + cb_ref[...]
    uo_ref[0] = (z[:, :hw] * z[:, hw:2 * hw]).astype(uo_ref.dtype)
    x0_ref[0] = z[:, 2 * hw:].astype(x0_ref.dtype)


def _hyena_pre(u, conv_w, conv_b, hw, tl=512):
    b, l, _ = u.shape
    tl = _tile(l, tl)
    nl = l // tl
    halo = 16
    hb = tl // halo
    nh = l // halo
    w3 = 3 * hw
    return pl.pallas_call(
        functools.partial(_hy_pre_kernel, nl=nl, tl=tl, hw=hw),
        grid=(b, nl),
        in_specs=[pl.BlockSpec((1, tl, w3), lambda bi, i: (bi, i, 0)),
                  pl.BlockSpec((1, halo, w3), lambda bi, i: (bi, jnp.maximum(i * hb - 1, 0), 0)),
                  pl.BlockSpec((1, halo, w3), lambda bi, i: (bi, jnp.minimum((i + 1) * hb, nh - 1), 0)),
                  pl.BlockSpec((3, w3), lambda bi, i: (0, 0)),
                  pl.BlockSpec((1, w3), lambda bi, i: (0, 0))],
        out_specs=[pl.BlockSpec((1, tl, hw), lambda bi, i: (bi, i, 0)),
                   pl.BlockSpec((1, tl, hw), lambda bi, i: (bi, i, 0))],
        out_shape=[jax.ShapeDtypeStruct((b, l, hw), BF16), jax.ShapeDtypeStruct((b, l, hw), BF16)],
        compiler_params=_cparams("parallel", "parallel"),
    )(u, u, u, conv_w, conv_b.reshape(1, w3))


def _hy_filter_kernel(z_ref, aux_ref, w0_ref, b0_ref, w1_ref, b1_ref, w2_ref, b2_ref, w3_ref, fr_ref, dl_ref,
                      k_ref, ss_ref, *, hw):
    i = pl.program_id(0)
    fr = fr_ref[...]
    h = jnp.sin(fr * (_dot3(z_ref[...], w0_ref[...]) + b0_ref[...]))
    h = jnp.sin(fr * (_dot3(h, w1_ref[...]) + b1_ref[...]))
    h = jnp.sin(fr * (_dot3(h, w2_ref[...]) + b2_ref[...]))
    h = _dot3(h, w3_ref[...])
    aux = aux_ref[...]
    t, m_fwd, m_bwd = aux[:, 0:1], aux[:, 1:2], aux[:, 2:3]
    k = (h[:, :hw] * m_fwd + h[:, hw:] * m_bwd) * jnp.exp(-t * dl_ref[...])
    k_ref[...] = k

    @pl.when(i == 0)
    def _():
        ss_ref[...] = jnp.zeros_like(ss_ref)

    ss_ref[...] += jnp.sum(k * k, axis=0, keepdims=True)


def _hyena_filter(l, hw, fw0, fb0, fw1, fb1, fw2, fb2, fw3, freq):
    n = 2 * l
    hid = HY_FILTER_HIDDEN
    r = jnp.arange(n)
    pos = jnp.where(r < l, r, n - r).astype(F32)
    t = pos / max(l - 1, 1)
    wv = (2.0 * math.pi / l) * pos
    bands = jnp.linspace(1e-4, HY_EMB_BANDS - 1, HY_EMB_BANDS, dtype=F32)
    ang = wv[:, None] * bands[None, :]
    ztab = jnp.concatenate([t[:, None], jnp.cos(ang), -jnp.sin(ang),
                            jnp.zeros((n, hid - 1 - 2 * HY_EMB_BANDS), F32)], axis=-1)
    aux = jnp.zeros((n, LANES), F32)
    aux = aux.at[:, 0].set(t).at[:, 1].set((r < l).astype(F32)).at[:, 2].set((r > l).astype(F32))
    w0 = jnp.concatenate([fw0, jnp.zeros((hid - fw0.shape[0], hid), F32)], axis=0)
    deltas = jnp.abs(jnp.linspace(math.log(HY_DECAY_TARGET) / HY_SLOW_DECAY,
                                  math.log(HY_DECAY_TARGET) / HY_FAST_DECAY, hw, dtype=F32)).reshape(1, hw)
    tr = _tile(n, 1024)
    full = lambda shape: pl.BlockSpec(shape, lambda i: (0, 0))
    return pl.pallas_call(
        functools.partial(_hy_filter_kernel, hw=hw),
        grid=(n // tr,),
        in_specs=[pl.BlockSpec((tr, hid), lambda i: (i, 0)),
                  pl.BlockSpec((tr, LANES), lambda i: (i, 0)),
                  full((hid, hid)), full((1, hid)), full((hid, hid)), full((1, hid)),
                  full((hid, hid)), full((1, hid)), full((hid, 2 * hw)), full((1, hid)), full((1, hw))],
        out_specs=[pl.BlockSpec((tr, hw), lambda i: (i, 0)), full((1, hw))],
        out_shape=[jax.ShapeDtypeStruct((n, hw), F32), jax.ShapeDtypeStruct((1, hw), F32)],
        compiler_params=_cparams("arbitrary"),
    )(ztab, aux, w0, fb0.reshape(1, hid), fw1, fb1.reshape(1, hid), fw2, fb2.reshape(1, hid), fw3,
      freq.reshape(1, hid), deltas)


def _hy_inv_kernel(wr_ref, wi_ref, ur_ref, ui_ref, kr_ref, ki_ref, u_ref, x0_ref, bias_ref, o_ref, acc_ref, *, nf):
    f = pl.program_id(2)

    @pl.when(f == 0)
    def _():
        acc_ref[...] = jnp.zeros_like(acc_ref)

    ur, ui, kr, ki = ur_ref[0], ui_ref[0], kr_ref[0], ki_ref[0]
    packed = jnp.logical_and(lax.broadcasted_iota(I32, ur.shape, 0) == 0, f == 0)
    yr = jnp.where(packed, ur * kr, ur * kr - ui * ki)
    yi = jnp.where(packed, ui * ki, ur * ki + ui * kr)
    acc_ref[...] += _dot(wr_ref[...], yr.astype(BF16)) + _dot(wi_ref[...], yi.astype(BF16))

    @pl.when(f == nf - 1)
    def _():
        u = u_ref[0].astype(F32)
        o_ref[0] = (x0_ref[0].astype(F32) * (acc_ref[...] + u * bias_ref[...])).astype(o_ref.dtype)


def _hyena_inverse(winv, uhat, kspec, u, x0, bias, tm=2048, tf=512):
    b, l, c = u.shape
    nfreq = uhat.shape[1] // 2
    tm = _tile(l, tm)
    tf = _tile(nfreq, tf)
    nf = nfreq // tf
    return pl.pallas_call(
        functools.partial(_hy_inv_kernel, nf=nf),
        grid=(b, l // tm, nf),
        in_specs=[pl.BlockSpec((tm, tf), lambda bi, i, f: (i, f)),
                  pl.BlockSpec((tm, tf), lambda bi, i, f: (i, nf + f)),
                  pl.BlockSpec((1, tf, c), lambda bi, i, f: (bi, f, 0)),
                  pl.BlockSpec((1, tf, c), lambda bi, i, f: (bi, nf + f, 0)),
                  pl.BlockSpec((1, tf, c), lambda bi, i, f: (0, f, 0)),
                  pl.BlockSpec((1, tf, c), lambda bi, i, f: (0, nf + f, 0)),
                  pl.BlockSpec((1, tm, c), lambda bi, i, f: (bi, i, 0)),
                  pl.BlockSpec((1, tm, c), lambda bi, i, f: (bi, i, 0)),
                  pl.BlockSpec((1, c), lambda bi, i, f: (0, 0))],
        out_specs=pl.BlockSpec((1, tm, c), lambda bi, i, f: (bi, i, 0)),
        out_shape=jax.ShapeDtypeStruct((b, l, c), BF16),
        scratch_shapes=[pltpu.VMEM((tm, c), F32)],
        compiler_params=_cparams("parallel", "parallel", "arbitrary"),
    )(winv, winv, uhat, uhat, kspec, kspec, u, x0, bias.reshape(1, c))


def _proj_res_kernel(x_ref, a_ref, b_ref, w_ref, gate_ref, o_ref, *, ka):
    y = _dot(a_ref[0], w_ref[:ka]) + _dot(b_ref[0], w_ref[ka:])
    o_ref[0] = x_ref[0] + gate_ref[0] * y


def _proj_residual(x, a, a_blk, b2, b_blk, ka, w, gate, tm=512):
    b, l, d = x.shape
    tm = _tile(l, tm)
    return pl.pallas_call(
        functools.partial(_proj_res_kernel, ka=ka),
        grid=(b, l // tm),
        in_specs=[pl.BlockSpec((1, tm, d), lambda bi, i: (bi, i, 0)),
                  pl.BlockSpec((1, tm, ka), lambda bi, i: (bi, i, a_blk)),
                  pl.BlockSpec((1, tm, ka), lambda bi, i: (bi, i, b_blk)),
                  pl.BlockSpec(w.shape, lambda bi, i: (0, 0)),
                  pl.BlockSpec((1, 1, d), lambda bi, i: (bi, 0, 0))],
        out_specs=pl.BlockSpec((1, tm, d), lambda bi, i: (bi, i, 0)),
        out_shape=jax.ShapeDtypeStruct((b, l, d), F32),
        compiler_params=_cparams("parallel", "parallel"),
    )(x, a, b2, w, gate)


def _ffn_kernel(x_ref, g_ref, sh_ref, sc_ref, gate_ref, wg_ref, wu_ref, wd_ref, o_ref, h_ref, acc_ref, *, nf):
    f = pl.program_id(2)

    @pl.when(f == 0)
    def _():
        h_ref[...] = _norm_mod(x_ref[0], g_ref[...], sh_ref[0], sc_ref[0]).astype(BF16)
        acc_ref[...] = jnp.zeros_like(acc_ref)

    h = h_ref[...]
    mid = _silu(_dot(h, wg_ref[...])) * _dot(h, wu_ref[...])
    acc_ref[...] += _dot(mid.astype(BF16), wd_ref[...])

    @pl.when(f == nf - 1)
    def _():
        o_ref[0] = x_ref[0] + gate_ref[0] * acc_ref[...]


def _dense_ffn(x, g, shift, scale, gate, wg, wu, wd, tm=512, tf=1792):
    b, l, d = x.shape
    dff = wg.shape[1]
    tm = _tile(l, tm)
    tf = _tile(dff, tf)
    nf = dff // tf
    vec = pl.BlockSpec((1, 1, d), lambda bi, i, f: (bi, 0, 0))
    return pl.pallas_call(
        functools.partial(_ffn_kernel, nf=nf),
        grid=(b, l // tm, nf),
        in_specs=[pl.BlockSpec((1, tm, d), lambda bi, i, f: (bi, i, 0)),
                  pl.BlockSpec((1, d), lambda bi, i, f: (0, 0)),
                  vec, vec, vec,
                  pl.BlockSpec((d, tf), lambda bi, i, f: (0, f)),
                  pl.BlockSpec((d, tf), lambda bi, i, f: (0, f)),
                  pl.BlockSpec((tf, d), lambda bi, i, f: (f, 0))],
        out_specs=pl.BlockSpec((1, tm, d), lambda bi, i, f: (bi, i, 0)),
        out_shape=jax.ShapeDtypeStruct((b, l, d), F32),
        scratch_shapes=[pltpu.VMEM((tm, d), BF16), pltpu.VMEM((tm, d), F32)],
        compiler_params=_cparams("parallel", "parallel", "arbitrary"),
    )(x, g.reshape(1, d), shift, scale, gate, wg, wu, wd)


def _head_norm(t, e, et, g_full):
    ss = _dot((t * t).astype(BF16), e)
    rinv = lax.rsqrt(ss * (1.0 / HEAD_DIM) + NORM_EPS)
    hi, lo = _split(rinv)
    return t * (_dot(hi, et) + _dot(lo, et)) * g_full


def _rope(t, cos, sin_lo, sin_hi):
    w = t.shape[1]
    rep = w // LANES
    tile = lambda a: jnp.concatenate([a] * rep, axis=1)
    return (t * tile(cos) + pltpu.roll(t, w - ROPE_HALF, 1) * tile(sin_lo)
            + pltpu.roll(t, ROPE_HALF, 1) * tile(sin_hi))


def _qkv_kernel(x_ref, g_ref, sh_ref, sc_ref, w_ref, e_ref, et_ref, qg_ref, kg_ref, cos_ref, sl_ref, sh2_ref,
                *out_refs, qw, kw, rope):
    k_ref, v_ref = out_refs[-2:]
    h = _norm_mod(x_ref[0], g_ref[...], sh_ref[0], sc_ref[0])
    t = _dot(h.astype(BF16), w_ref[...])
    e, et = e_ref[...], et_ref[...]
    k = _head_norm(t[:, qw:qw + kw], e[:kw], et[:, :kw], kg_ref[...])
    if rope:
        k = _rope(k, cos_ref[...], sl_ref[...], sh2_ref[...])
    k_ref[0] = k.astype(BF16)
    v_ref[0] = t[:, qw + kw:].astype(BF16)
    if qw:
        q = _head_norm(t[:, :qw], e, et, qg_ref[...])
        q = _rope(q, cos_ref[...], sl_ref[...], sh2_ref[...]) * (HEAD_DIM ** -0.5)
        out_refs[0][0] = q.astype(BF16)


def _head_tables(qw):
    lane = jnp.arange(qw)
    e = (lane[:, None] // HEAD_DIM == jnp.arange(LANES)[None, :]).astype(BF16)
    return e, e.T


def _rope_tables(l):
    rows = l // GRID_W
    row = jnp.repeat(jnp.arange(rows, dtype=F32), GRID_W)
    col = jnp.tile(jnp.arange(GRID_W, dtype=F32), rows)
    inv = ROPE_THETA ** (-jnp.arange(0, AXIS_ROPE_DIM, 2, dtype=F32) / AXIS_ROPE_DIM)
    lane = jnp.arange(LANES)
    in_head = lane % HEAD_DIM
    use_col = (in_head // AXIS_ROPE_DIM) == 1
    hi_half = ((in_head % AXIS_ROPE_DIM) // ROPE_HALF) == 1
    freq = inv[in_head % ROPE_HALF]
    ang = jnp.where(use_col[None, :], col[:, None], row[:, None]) * freq[None, :]
    cos, sin = jnp.cos(ang), jnp.sin(ang)
    return cos, jnp.where(hi_half[None, :], 0.0, -sin), jnp.where(hi_half[None, :], sin, 0.0)


def _qkv_project(x, g, shift, scale, w, q_g, k_g, qw, kw, rope, tm=512):
    b, l, d = x.shape
    tm = _tile(l, tm)
    e, et = _head_tables(max(qw, kw))
    n_q = max(qw, kw) // HEAD_DIM
    qg = jnp.tile(q_g, n_q).reshape(1, -1)
    kg = jnp.tile(k_g, kw // HEAD_DIM).reshape(1, kw)
    if rope:
        cos, s_lo, s_hi = _rope_tables(l)
    else:
        cos = s_lo = s_hi = jnp.zeros((l, LANES), F32)
    vec = pl.BlockSpec((1, 1, d), lambda bi, i: (bi, 0, 0))
    full = lambda a: pl.BlockSpec(a.shape, lambda bi, i: (0,) * a.ndim)
    tab = pl.BlockSpec((tm, LANES), lambda bi, i: (i, 0))
    widths = ([qw] if qw else []) + [kw, kw]
    return pl.pallas_call(
        functools.partial(_qkv_kernel, qw=qw, kw=kw, rope=rope),
        grid=(b, l // tm),
        in_specs=[pl.BlockSpec((1, tm, d), lambda bi, i: (bi, i, 0)),
                  pl.BlockSpec((1, d), lambda bi, i: (0, 0)), vec, vec,
                  full(w), full(e), full(et), full(qg), full(kg), tab, tab, tab],
        out_specs=[pl.BlockSpec((1, tm, n), lambda bi, i: (bi, i, 0)) for n in widths],
        out_shape=[jax.ShapeDtypeStruct((b, l, n), BF16) for n in widths],
        compiler_params=_cparams("parallel", "parallel"),
    )(x, g.reshape(1, d), shift, scale, w, e, et, qg, kg, cos, s_lo, s_hi)


def _attn_kernel(q_ref, kp_ref, kc_ref, kn_ref, vp_ref, vc_ref, vn_ref, kx_ref, vx_ref, sink_ref, o_ref, *,
                 seq, group):
    qb = pl.program_id(1)
    bq = q_ref.shape[1]
    lc = kx_ref.shape[1]
    nkeys = 3 * bq + lc
    rows = lax.broadcasted_iota(I32, (group * bq, nkeys), 0)
    cols = lax.broadcasted_iota(I32, (group * bq, nkeys), 1)
    qpos = qb * bq + rows % bq
    kpos = (qb - 1) * bq + cols
    valid = jnp.logical_or(
        cols >= 3 * bq,
        jnp.logical_and(jnp.logical_and(kpos >= 0, kpos < seq), jnp.abs(qpos - kpos) <= WINDOW))
    for h in range(N_KV_HEADS):
        ks = slice(h * HEAD_DIM, (h + 1) * HEAD_DIM)
        kh = jnp.concatenate([kp_ref[0, :, ks], kc_ref[0, :, ks], kn_ref[0, :, ks], kx_ref[0, :, ks]], axis=0)
        vh = jnp.concatenate([vp_ref[0, :, ks], vc_ref[0, :, ks], vn_ref[0, :, ks], vx_ref[0, :, ks]], axis=0)
        qh = jnp.concatenate(
            [q_ref[0, :, (h * group + g) * HEAD_DIM:(h * group + g + 1) * HEAD_DIM] for g in range(group)], axis=0)
        s = lax.dot_general(qh, kh, (((1,), (1,)), ((), ())), preferred_element_type=F32)
        s = jnp.where(valid, s, NEG_INF)
        sk = sink_ref[h]
        m = jnp.maximum(jnp.max(s, axis=1, keepdims=True), sk)
        p = jnp.exp(s - m)
        den = jnp.sum(p, axis=1, keepdims=True) + jnp.exp(sk - m)
        o = _dot(p.astype(BF16), vh) / den
        for g in range(group):
            hq = h * group + g
            o_ref[0, :, hq * HEAD_DIM:(hq + 1) * HEAD_DIM] = o[g * bq:(g + 1) * bq].astype(o_ref.dtype)


def _window_attention(q, k, v, kx, vx, sink):
    b, l, qw = q.shape
    kw = k.shape[2]
    lc = kx.shape[1]
    bq = BLOCK_Q
    nb = l // bq
    group = qw // kw
    sink_tab = jnp.repeat(sink.astype(F32).reshape(N_KV_HEADS, group), bq, axis=1)[..., None]
    kv_prev = pl.BlockSpec((1, bq, kw), lambda bi, i: (bi, jnp.maximum(i - 1, 0), 0))
    kv_cur = pl.BlockSpec((1, bq, kw), lambda bi, i: (bi, i, 0))
    kv_next = pl.BlockSpec((1, bq, kw), lambda bi, i: (bi, jnp.minimum(i + 1, nb - 1), 0))
    kv_ctx = pl.BlockSpec((1, lc, kw), lambda bi, i: (bi, 0, 0))
    return pl.pallas_call(
        functools.partial(_attn_kernel, seq=l, group=group),
        grid=(b, nb),
        in_specs=[pl.BlockSpec((1, bq, qw), lambda bi, i: (bi, i, 0)),
                  kv_prev, kv_cur, kv_next, kv_prev, kv_cur, kv_next, kv_ctx, kv_ctx,
                  pl.BlockSpec(sink_tab.shape, lambda bi, i: (0, 0, 0))],
        out_specs=pl.BlockSpec((1, bq, qw), lambda bi, i: (bi, i, 0)),
        out_shape=jax.ShapeDtypeStruct((b, l, qw), BF16),
        compiler_params=_cparams("parallel", "parallel"),
    )(q, k, k, k, v, v, v, kx, vx, sink_tab)


def _router_kernel(x_ref, g_ref, sh_ref, sc_ref, wh_ref, wl_ref, tri_ref, h_ref, gate_ref, rank_ref, rankt_ref,
                   cnt_ref):
    h = _norm_mod(x_ref[0], g_ref[...], sh_ref[0], sc_ref[0])
    h_ref[0] = h.astype(BF16)
    hi, lo = _split(h)
    logits = _dot(hi, wh_ref[...]) + _dot(hi, wl_ref[...]) + _dot(lo, wh_ref[...])
    lane = lax.broadcasted_iota(I32, logits.shape, 1)
    logits = jnp.where(lane < N_EXPERTS, logits, -jnp.inf)
    m1 = jnp.max(logits, axis=1, keepdims=True)
    i1 = jnp.min(jnp.where(logits == m1, lane, LANES), axis=1, keepdims=True)
    rest = jnp.where(lane == i1, -jnp.inf, logits)
    m2 = jnp.max(rest, axis=1, keepdims=True)
    i2 = jnp.min(jnp.where(rest == m2, lane, LANES), axis=1, keepdims=True)
    e = jnp.exp(m2 - m1)
    g1 = 1.0 / (1.0 + e)
    g2 = e / (1.0 + e)
    pick1, pick2 = lane == i1, lane == i2
    member = jnp.logical_or(pick1, pick2)
    gate_ref[0] = jnp.where(pick1, g1, jnp.where(pick2, g2, 0.0))
    m = jnp.where(member, 1.0, 0.0)
    rank = jnp.where(member, _dot(tri_ref[...], m.astype(BF16)), -1.0)
    rank_ref[0] = rank.astype(I32)
    rankt_ref[0] = jnp.transpose(rank)[:SUBLANES].astype(I32)
    cnt_ref[0] = jnp.sum(m, axis=0, keepdims=True).astype(I32)


def _route(x, g, shift, scale, w_router):
    b, l, d = x.shape
    tm = _tile(l, MOE_TILE)
    nt = l // tm
    wr = jnp.concatenate([w_router, jnp.zeros((d, LANES - w_router.shape[1]), F32)], axis=1)
    wh, wl = _split(wr)
    tri = jnp.tril(jnp.ones((tm, tm), BF16), -1)
    vec = pl.BlockSpec((1, 1, d), lambda bi, i: (bi, 0, 0))
    tok = lambda n: pl.BlockSpec((1, tm, n), lambda bi, i: (bi, i, 0))
    return pl.pallas_call(
        _router_kernel,
        grid=(b, nt),
        in_specs=[tok(d), pl.BlockSpec((1, d), lambda bi, i: (0, 0)), vec, vec,
                  pl.BlockSpec((d, LANES), lambda bi, i: (0, 0)),
                  pl.BlockSpec((d, LANES), lambda bi, i: (0, 0)),
                  pl.BlockSpec((tm, tm), lambda bi, i: (0, 0))],
        out_specs=[tok(d), tok(LANES), tok(LANES),
                   pl.BlockSpec((1, SUBLANES, tm), lambda bi, i: (bi * nt + i, 0, 0)),
                   pl.BlockSpec((1, 1, LANES), lambda bi, i: (bi * nt + i, 0, 0))],
        out_shape=[jax.ShapeDtypeStruct((b, l, d), BF16),
                   jax.ShapeDtypeStruct((b, l, LANES), F32),
                   jax.ShapeDtypeStruct((b, l, LANES), I32),
                   jax.ShapeDtypeStruct((b * nt, SUBLANES, tm), I32),
                   jax.ShapeDtypeStruct((b * nt, 1, LANES), I32)],
        compiler_params=_cparams("parallel", "parallel"),
    )(x, g.reshape(1, d), shift, scale, wh, wl, tri)


def _slot_layout(cnt, rows):
    nt = cnt.shape[0]
    seg = (cnt + SUBLANES - 1) // SUBLANES * SUBLANES
    padded = (jnp.sum(seg, axis=0) + rows - 1) // rows * rows
    stride = padded + rows
    pstart = jnp.cumsum(stride) - stride
    off = pstart[None, :] + jnp.cumsum(seg, axis=0) - seg
    n_slots = (nt * N_EXPERTS * (SUBLANES - 1) + nt * MOE_TILE * TOP_K + rows - 1) // rows * rows \
        + 2 * N_EXPERTS * rows
    nblk = n_slots // rows
    bstart = jnp.arange(nblk, dtype=I32) * rows
    be = jnp.minimum(jnp.sum((bstart[:, None] >= (pstart + stride)[None, :]).astype(I32), axis=1), N_EXPERTS - 1)
    onehot = (be[:, None] == jnp.arange(N_EXPERTS, dtype=I32)[None, :]).astype(I32)
    lo = jnp.sum(onehot * pstart[None, :], axis=1)
    hi = jnp.sum(onehot * (pstart + padded)[None, :], axis=1)
    used = jnp.logical_and(bstart >= lo, bstart < hi)
    zero_blk = jnp.logical_or(jnp.logical_not(used), bstart == hi - rows)
    as_i32 = lambda a: a.reshape(-1).astype(I32)
    return n_slots, as_i32(off), as_i32(cnt), as_i32(zero_blk), as_i32(be), as_i32(used)


def _seg_copy(src_ref, dst_ref, row, sem):
    n = src_ref.shape[0]
    return pltpu.make_async_copy(src_ref, dst_ref.at[pl.ds(pl.multiple_of(row, SUBLANES), n)], sem)


def _chunk_copy(src_ref, dst_ref, row, sem):
    n = dst_ref.shape[0]
    return pltpu.make_async_copy(src_ref.at[pl.ds(pl.multiple_of(row, SUBLANES), n)], dst_ref, sem)


def _dispatch_kernel(off_ref, cnt_ref, zero_ref, h_ref, rt_ref, xs_ref, xbuf, xbuf2, sem, sem2, *, rows):
    i = pl.program_id(0)
    cap = xbuf.shape[1]
    tm = h_ref.shape[1]

    @pl.when(i == 0)
    def _():
        xbuf[0] = jnp.zeros(xbuf.shape[1:], F32)

        def zero_block(j, carry):
            @pl.when(zero_ref[j] > 0)
            def _():
                for part in range(rows // cap):
                    c = _seg_copy(xbuf.at[0], xs_ref, j * rows + part * cap, sem2)
                    c.start()
                    c.wait()
            return carry

        lax.fori_loop(0, zero_ref.shape[0], zero_block, 0)

    h = h_ref[0]
    rt = rt_ref[0]
    riota = lax.broadcasted_iota(I32, (cap, tm), 0)

    def copy(e):
        return _seg_copy(xbuf.at[e], xs_ref, off_ref[i * N_EXPERTS + e], sem.at[e])

    for e in range(N_EXPERTS):
        sel = rt[e:e + 1, :]

        def segment(base):
            return _dot(jnp.where(sel == riota + base, 1.0, 0.0).astype(BF16), h)

        xbuf[e] = segment(0)
        copy(e).start()

        @pl.when(cnt_ref[i * N_EXPERTS + e] > cap)
        def _():
            xbuf2[...] = segment(cap)
            c = _seg_copy(xbuf2, xs_ref, off_ref[i * N_EXPERTS + e] + cap, sem2)
            c.start()
            c.wait()

    for e in range(N_EXPERTS):
        copy(e).wait()


def _dispatch(h, rankt, n_slots, off, cnt, zero_blk, rows):
    b, l, d = h.shape
    tm = _tile(l, MOE_TILE)
    nt = l // tm
    cap = MOE_CAP
    assert tm <= 2 * cap and rows % cap == 0
    grid_spec = pltpu.PrefetchScalarGridSpec(
        num_scalar_prefetch=3,
        grid=(b * nt,),
        in_specs=[pl.BlockSpec((1, tm, d), lambda i, *_: (i // nt, i % nt, 0)),
                  pl.BlockSpec((1, SUBLANES, tm), lambda i, *_: (i, 0, 0))],
        out_specs=pl.BlockSpec(memory_space=pl.ANY),
        scratch_shapes=[pltpu.VMEM((N_EXPERTS, cap, d), F32), pltpu.VMEM((cap, d), F32),
                        pltpu.SemaphoreType.DMA((N_EXPERTS,)), pltpu.SemaphoreType.DMA(())],
    )
    return pl.pallas_call(
        functools.partial(_dispatch_kernel, rows=rows),
        grid_spec=grid_spec,
        out_shape=jax.ShapeDtypeStruct((n_slots, d), F32),
        compiler_params=_cparams("arbitrary"),
    )(off, cnt, zero_blk, h, rankt)


def _moe_kernel(be_ref, used_ref, x_ref, wg_ref, wu_ref, wd_ref, o_ref, acc_ref, *, nf):
    i = pl.program_id(0)
    f = pl.program_id(1)
    used = used_ref[i] > 0

    @pl.when(jnp.logical_and(used, f == 0))
    def _():
        acc_ref[...] = jnp.zeros_like(acc_ref)

    @pl.when(used)
    def _():
        x = x_ref[...].astype(BF16)
        mid = _silu(_dot(x, wg_ref[0])) * _dot(x, wu_ref[0])
        acc_ref[...] += _dot(mid.astype(BF16), wd_ref[0])

    @pl.when(jnp.logical_and(used, f == nf - 1))
    def _():
        o_ref[...] = acc_ref[...]

    @pl.when(jnp.logical_and(jnp.logical_not(used), f == nf - 1))
    def _():
        o_ref[...] = jnp.zeros_like(o_ref)


def _expert_ffn(xs, block_e, used, wg, wu, wd, rows, tf=1792):
    s, d = xs.shape
    dff = wg.shape[2]
    tf = _tile(dff, tf)
    nf = dff // tf
    nblk = s // rows
    chunk = lambda i, f, us: f * us[i] + (nf - 1) * (1 - us[i])
    grid_spec = pltpu.PrefetchScalarGridSpec(
        num_scalar_prefetch=2,
        grid=(nblk, nf),
        in_specs=[pl.BlockSpec((rows, d), lambda i, f, be, us: (i, 0)),
                  pl.BlockSpec((1, d, tf), lambda i, f, be, us: (be[i], 0, chunk(i, f, us))),
                  pl.BlockSpec((1, d, tf), lambda i, f, be, us: (be[i], 0, chunk(i, f, us))),
                  pl.BlockSpec((1, tf, d), lambda i, f, be, us: (be[i], chunk(i, f, us), 0))],
        out_specs=pl.BlockSpec((rows, d), lambda i, f, be, us: (i, 0)),
        scratch_shapes=[pltpu.VMEM((rows, d), F32)],
    )
    return pl.pallas_call(
        functools.partial(_moe_kernel, nf=nf),
        grid_spec=grid_spec,
        out_shape=jax.ShapeDtypeStruct((s, d), F32),
        compiler_params=_cparams("arbitrary", "arbitrary"),
    )(block_e, used, xs, wg, wu, wd)


def _combine_kernel(off_ref, cnt_ref, x_ref, gate_ref, rg_ref, rank_ref, ys_ref, o_ref, buf, buf2, acc_ref, sem,
                    sem2, *, n_tiles):
    i = pl.program_id(0)
    slot = i % 2
    cap = buf.shape[2]
    tm = x_ref.shape[1]

    def chunk(t, s, e):
        return _chunk_copy(ys_ref, buf.at[s, e], off_ref[t * N_EXPERTS + e], sem.at[s, e])

    @pl.when(i == 0)
    def _():
        for e in range(N_EXPERTS):
            chunk(0, 0, e).start()

    @pl.when(i + 1 < n_tiles)
    def _():
        for e in range(N_EXPERTS):
            chunk(i + 1, 1 - slot, e).start()

    rank = rank_ref[0]
    rg = rg_ref[0]
    liota = lax.broadcasted_iota(I32, (tm, cap), 1)

    def picked(e, base, rows_ref):
        q = jnp.where(rank[:, e:e + 1] == liota + base, 1.0, 0.0).astype(BF16)
        return rg[:, e:e + 1] * _dot(q, rows_ref[...].astype(BF16))

    y = jnp.zeros(acc_ref.shape, F32)
    for e in range(N_EXPERTS):
        chunk(i, slot, e).wait()
        y = y + picked(e, 0, buf.at[slot, e])
    acc_ref[...] = y

    for e in range(N_EXPERTS):
        @pl.when(cnt_ref[i * N_EXPERTS + e] > cap)
        def _():
            c = _chunk_copy(ys_ref, buf2, off_ref[i * N_EXPERTS + e] + cap, sem2)
            c.start()
            c.wait()
            acc_ref[...] += picked(e, cap, buf2)

    o_ref[0] = x_ref[0] + gate_ref[0] * acc_ref[...]


def _moe_combine(x, gate, route_gates, rank, ys, off, cnt):
    b, l, d = x.shape
    tm = _tile(l, MOE_TILE)
    nt = l // tm
    cap = MOE_CAP
    tok = lambda n: pl.BlockSpec((1, tm, n), lambda i, *_: (i // nt, i % nt, 0))
    grid_spec = pltpu.PrefetchScalarGridSpec(
        num_scalar_prefetch=2,
        grid=(b * nt,),
        in_specs=[tok(d), pl.BlockSpec((1, 1, d), lambda i, *_: (i // nt, 0, 0)), tok(LANES), tok(LANES),
                  pl.BlockSpec(memory_space=pl.ANY)],
        out_specs=tok(d),
        scratch_shapes=[pltpu.VMEM((2, N_EXPERTS, cap, d), F32), pltpu.VMEM((cap, d), F32),
                        pltpu.VMEM((tm, d), F32),
                        pltpu.SemaphoreType.DMA((2, N_EXPERTS)), pltpu.SemaphoreType.DMA(())],
    )
    return pl.pallas_call(
        functools.partial(_combine_kernel, n_tiles=b * nt),
        grid_spec=grid_spec,
        out_shape=jax.ShapeDtypeStruct((b, l, d), F32),
        compiler_params=_cparams("arbitrary"),
    )(off, cnt, x, gate, route_gates, rank, ys)


def _angles(rows, cols, n):
    m = (rows[:, None] * cols[None, :]) % n
    return (2.0 * math.pi / n) * m.astype(F32)


def _fnet_table(l):
    idx = jnp.arange(l, dtype=I32)
    ang = _angles(idx, idx, l)
    s = 1.0 / math.sqrt(l)
    return jnp.concatenate([jnp.cos(ang) * s, jnp.sin(ang) * (-s)], axis=1).astype(BF16)


def _rfft_table(n):
    half = n // 2
    f = jnp.arange(half, dtype=I32)
    t = jnp.arange(n, dtype=I32)
    ang = _angles(f, t, n)
    top = jnp.cos(ang)
    bot = -jnp.sin(ang)
    nyq = jnp.cos(_angles(jnp.full((1,), half, I32), t, n))
    bot = jnp.concatenate([nyq, bot[1:]], axis=0)
    return jnp.concatenate([top, bot], axis=0).astype(BF16)


def _group_dft_table(width):
    gd = FNET_GROUP_DIM
    idx = jnp.arange(gd, dtype=I32)
    ang = _angles(idx, idx, gd)
    s = 1.0 / math.sqrt(gd)
    eye = jnp.eye(width // gd, dtype=F32)
    return jnp.concatenate([jnp.kron(eye, jnp.cos(ang) * s), jnp.kron(eye, jnp.sin(ang) * s)], axis=1)


def _even_mixer(x, g1, shift, scale, gate, w_in_f, w_out, conv_w, conv_b, hy_bias, filt, fw, hw):
    b, l, d = x.shape
    n = 2 * l
    u = _norm_mod_matmul(x, g1, shift, scale, w_in_f, BF16)
    cb = 3 * hw // fw
    a = _left_dft(_fnet_table(l), u, lambda bi, k, tk: (bi, k % (l // tk), cb + k // (l // tk)), n, fw, BF16,
                  tk=min(l, 1024))
    uu, x0 = _hyena_pre(u, conv_w, conv_b, hw)
    wf = _rfft_table(n)
    k_raw, k_ss = filt(l)
    wts = jnp.concatenate([jnp.ones((1,), F32), jnp.full((l - 1,), 2.0, F32)]) / n
    row_scale = jnp.concatenate([wts, wts]).reshape(n, 1)
    kspec = _left_dft(wf, k_raw[None], lambda bi, k, tk: (0, k, 0), n, hw, F32,
                      row_scale=row_scale, col_sumsq=k_ss)
    uhat = _left_dft(wf, uu, lambda bi, k, tk: (bi, k, 0), l, hw, F32)
    winv = wf[:, :l].T
    hy = _hyena_inverse(winv, uhat, kspec, uu, x0, hy_bias)
    return _proj_residual(x, a, 0, hy, 0, fw, w_out, gate)


def kernel(x, c, ctx, c_ctx, ada_w, ada_b, norm1_g, norm2_g, ev_w_in, ev_w_out, hy_conv_w, hy_conv_b, hy_bias,
           hf_w0, hf_b0, hf_w1, hf_b1, hf_w2, hf_b2, hf_w3, hf_freq, ffn_w_gate, ffn_w_up, ffn_w_down, od_w_qkv,
           od_w_out, q_norm_g, k_norm_g, attn_sink, moe_router, moe_w_gate, moe_w_up, moe_w_down):
    b, l, d = x.shape
    lc = ctx.shape[1]
    assert ada_w.shape[0] == 2, "this implementation covers the two-layer (even, odd) stack"
    hw = hy_bias.shape[1]
    fw = ev_w_in.shape[2] - 3 * hw
    qw = od_w_out.shape[1]
    kw = (od_w_qkv.shape[2] - qw) // 2

    rows = (b + 1 + 7) // 8 * 8
    cc = jnp.concatenate([c, c_ctx[None, :], jnp.zeros((rows - b - 1, d), F32)], axis=0)
    mod = _ada_vectors(cc, ada_w, ada_b)
    ml = [[mod[i, :b, None, m * d:(m + 1) * d] for m in range(N_MOD)] for i in range(2)]
    mc = [[mod[i, b:b + 1, None, m * d:(m + 1) * d] for m in range(N_MOD)] for i in range(2)]

    w_in = ev_w_in[0]
    w_fnet = _matmul3(w_in[:, :fw], _group_dft_table(fw))
    w_in_f = jnp.concatenate([w_in[:, fw:], w_fnet], axis=1).astype(BF16)
    w_out0 = ev_w_out[0].astype(BF16)
    filt = lambda seq: _hyena_filter(seq, hw, hf_w0[0], hf_b0[0], hf_w1[0], hf_b1[0], hf_w2[0], hf_b2[0],
                                     hf_w3[0], hf_freq[0])
    wg, wu, wd = ffn_w_gate[0].astype(BF16), ffn_w_up[0].astype(BF16), ffn_w_down[0].astype(BF16)

    x = _even_mixer(x, norm1_g[0], ml[0][0], ml[0][1], ml[0][2], w_in_f, w_out0, hy_conv_w[0], hy_conv_b[0],
                    hy_bias[0], filt, fw, hw)
    x = _dense_ffn(x, norm2_g[0], ml[0][3], ml[0][4], ml[0][5], wg, wu, wd)

    bc = lambda v: jnp.broadcast_to(v, (b, 1, d))
    ctx = _even_mixer(ctx, norm1_g[0], bc(mc[0][0]), bc(mc[0][1]), bc(mc[0][2]), w_in_f, w_out0, hy_conv_w[0],
                      hy_conv_b[0], hy_bias[0], filt, fw, hw)
    ctx = _dense_ffn(ctx.reshape(1, b * lc, d), norm2_g[0], mc[0][3], mc[0][4], mc[0][5], wg, wu, wd)

    w_qkv = od_w_qkv[0].astype(BF16)
    q, k, v = _qkv_project(x, norm1_g[1], ml[1][0], ml[1][1], w_qkv, q_norm_g[0], k_norm_g[0], qw, kw, True)
    kx, vx = _qkv_project(ctx, norm1_g[1], mc[1][0], mc[1][1], w_qkv[:, qw:], q_norm_g[0], k_norm_g[0], 0, kw,
                             False)
    o = _window_attention(q, k, v, kx.reshape(b, lc, kw), vx.reshape(b, lc, kw), attn_sink[0])
    half = qw // 2
    x = _proj_residual(x, o, 0, o, 1, half, od_w_out[0].astype(BF16), ml[1][2])

    h2, gates, rank, rankt, cnt = _route(x, norm2_g[1], ml[1][3], ml[1][4], moe_router[0])
    n_slots, off, cnt, zero_blk, block_e, used = _slot_layout(cnt[:, 0, :N_EXPERTS], MOE_ROWS)
    xs = _dispatch(h2, rankt, n_slots, off, cnt, zero_blk, MOE_ROWS)
    ys = _expert_ffn(xs, block_e, used, moe_w_gate[0].astype(BF16), moe_w_up[0].astype(BF16),
                     moe_w_down[0].astype(BF16), MOE_ROWS)
    return _moe_combine(x, ml[1][5], gates, rank, ys, off, cnt)
```

```python
import functools
import math

import jax
import jax.numpy as jnp
from jax import lax
from jax.experimental import pallas as pl
from jax.experimental.pallas import tpu as pltpu

F32 = jnp.float32
BF16 = jnp.bfloat16
I32 = jnp.int32

NORM_EPS = 1e-6
NEG_INF = -1e30
N_MOD = 6

FNET_GROUP_DIM = 128
HY_EMB_BANDS = 16
HY_FILTER_HIDDEN = 64
HY_DECAY_TARGET = 1e-2
HY_FAST_DECAY = 0.3
HY_SLOW_DECAY = 1.5

HEAD_DIM = 64
N_KV_HEADS = 4
GRID_W = 64
WINDOW = 128
BLOCK_Q = 128
ROPE_THETA = 10000.0
AXIS_ROPE_DIM = HEAD_DIM // 2
ROPE_HALF = AXIS_ROPE_DIM // 2
N_EXPERTS = 8
TOP_K = 2

LANES = 128
SUBLANES = 8
VMEM_LIMIT = 56 * 1024 * 1024
MOE_TILE = 512
MOE_ROWS = 512
MOE_CAP = 256


def _cparams(*sem):
    return pltpu.CompilerParams(dimension_semantics=sem, vmem_limit_bytes=VMEM_LIMIT)


def _split(a):
    hi = a.astype(BF16)
    lo = (a - hi.astype(F32)).astype(BF16)
    return hi, lo


def _dot(a, b):
    return jnp.dot(a, b, preferred_element_type=F32)


def _dot3(a, b):
    ah, al = _split(a)
    bh, bl = _split(b)
    return _dot(ah, bh) + _dot(ah, bl) + _dot(al, bh)


def _silu(t):
    return t / (1.0 + jnp.exp(-t))


def _norm_mod(x, g, shift, scale):
    ms = jnp.mean(x * x, axis=-1, keepdims=True)
    y = x * lax.rsqrt(ms + NORM_EPS) * g
    return y * (1.0 + scale) + shift


def _tile(n, pref):
    t = min(n, pref)
    assert n % t == 0, (n, pref)
    return t


def _ada_kernel(c_ref, w_ref, b_ref, o_ref):
    o_ref[0] = _dot3(_silu(c_ref[...]), w_ref[0]) + b_ref[0]


def _ada_vectors(cc, ada_w, ada_b):
    depth, d, n = ada_w.shape
    rows = cc.shape[0]
    tn = _tile(n, 1536)
    return pl.pallas_call(
        _ada_kernel,
        grid=(depth, n // tn),
        in_specs=[pl.BlockSpec((rows, d), lambda l, j: (0, 0)),
                  pl.BlockSpec((1, d, tn), lambda l, j: (l, 0, j)),
                  pl.BlockSpec((1, 1, tn), lambda l, j: (l, 0, j))],
        out_specs=pl.BlockSpec((1, rows, tn), lambda l, j: (l, 0, j)),
        out_shape=jax.ShapeDtypeStruct((depth, rows, n), F32),
        compiler_params=_cparams("arbitrary", "arbitrary"),
    )(cc, ada_w, ada_b.reshape(depth, 1, n))


def _mm3_kernel(a_ref, b_ref, o_ref):
    o_ref[...] = _dot3(a_ref[...], b_ref[...])


def _matmul3(a, b):
    m, _ = a.shape
    n = b.shape[1]
    return pl.pallas_call(_mm3_kernel, out_shape=jax.ShapeDtypeStruct((m, n), F32),
                          compiler_params=_cparams())(a, b)


def _nmm_kernel(x_ref, g_ref, sh_ref, sc_ref, w_ref, *o_refs):
    h = _norm_mod(x_ref[0], g_ref[...], sh_ref[0], sc_ref[0])
    y = _dot(h.astype(BF16), w_ref[...])
    col = 0
    for o_ref in o_refs:
        n = o_ref.shape[2]
        o_ref[0] = y[:, col:col + n].astype(o_ref.dtype)
        col += n


def _norm_mod_matmul(x, g, shift, scale, w, widths, out_dtypes, tm=512):
    b, l, d = x.shape
    n = w.shape[1]
    assert sum(widths) == n
    tm = _tile(l, tm)
    return pl.pallas_call(
        _nmm_kernel,
        grid=(b, l // tm),
        in_specs=[pl.BlockSpec((1, tm, d), lambda bi, i: (bi, i, 0)),
                  pl.BlockSpec((1, d), lambda bi, i: (0, 0)),
                  pl.BlockSpec((1, 1, d), lambda bi, i: (bi, 0, 0)),
                  pl.BlockSpec((1, 1, d), lambda bi, i: (bi, 0, 0)),
                  pl.BlockSpec((d, n), lambda bi, i: (0, 0))],
        out_specs=[pl.BlockSpec((1, tm, wd), lambda bi, i: (bi, i, 0)) for wd in widths],
        out_shape=[jax.ShapeDtypeStruct((b, l, wd), dt) for wd, dt in zip(widths, out_dtypes)],
        compiler_params=_cparams("parallel", "parallel"),
    )(x, g.reshape(1, d), shift, scale, w)


def _ldft_kernel(w_ref, r_ref, o_ref, acc_ref, *, nk):
    k = pl.program_id(2)

    @pl.when(k == 0)
    def _():
        acc_ref[...] = jnp.zeros_like(acc_ref)

    acc_ref[...] += _dot(w_ref[...], r_ref[0].astype(BF16))

    @pl.when(k == nk - 1)
    def _():
        o_ref[0] = acc_ref[...].astype(o_ref.dtype)


def _ldft_scaled_kernel(w_ref, r_ref, rs_ref, ss_ref, o_ref, acc_ref, *, nk):
    k = pl.program_id(2)

    @pl.when(k == 0)
    def _():
        acc_ref[...] = jnp.zeros_like(acc_ref)

    acc_ref[...] += _dot(w_ref[...], r_ref[0].astype(BF16))

    @pl.when(k == nk - 1)
    def _():
        o_ref[0] = acc_ref[...] * rs_ref[...] * lax.rsqrt(ss_ref[...] + NORM_EPS)


def _left_dft(w, rhs, rhs_map, kdim, c, out_dtype, tm=2048, tk=1024, row_scale=None, col_sumsq=None):
    m = w.shape[0]
    nb = rhs.shape[0]
    tm = _tile(m, tm)
    tk = _tile(kdim, tk)
    nk = kdim // tk
    in_specs = [pl.BlockSpec((tm, tk), lambda b, i, k: (i, k)),
                pl.BlockSpec((1, tk, c), lambda b, i, k: rhs_map(b, k, tk))]
    args = [w, rhs]
    if row_scale is None:
        body = functools.partial(_ldft_kernel, nk=nk)
    else:
        body = functools.partial(_ldft_scaled_kernel, nk=nk)
        in_specs += [pl.BlockSpec((tm, 1), lambda b, i, k: (i, 0)),
                     pl.BlockSpec((1, c), lambda b, i, k: (0, 0))]
        args += [row_scale, col_sumsq]
    return pl.pallas_call(
        body,
        grid=(nb, m // tm, nk),
        in_specs=in_specs,
        out_specs=pl.BlockSpec((1, tm, c), lambda b, i, k: (b, i, 0)),
        out_shape=jax.ShapeDtypeStruct((nb, m, c), out_dtype),
        scratch_shapes=[pltpu.VMEM((tm, c), F32)],
        compiler_params=_cparams("parallel", "parallel", "arbitrary"),
    )(*args)


def _hy_pre_kernel(u_ref, up_ref, un_ref, cw_ref, cb_ref, uo_ref, x0_ref, *, nl, tl, hw):
    i = pl.program_id(1)
    u = u_ref[0].astype(F32)
    halo = up_ref.shape[1]
    prev = up_ref[0].astype(F32)[halo - 1:halo]
    nxt = un_ref[0].astype(F32)[0:1]
    prev = jnp.where(i == 0, 0.0, prev)
    nxt = jnp.where(i == nl - 1, 0.0, nxt)
    rows = lax.broadcasted_iota(I32, u.shape, 0)
    um = jnp.where(rows == 0, prev, pltpu.roll(u, 1, 0))
    up = jnp.where(rows == tl - 1, nxt, pltpu.roll(u, tl - 1, 0))
    cw = cw_ref[...]
    z = um * cw[0:1] + u * cw[1:2] + up * cw[2:3] + cb_ref[...]
    uo_ref[0] = (z[:, :hw] * z[:, hw:2 * hw]).astype(uo_ref.dtype)
    x0_ref[0] = z[:, 2 * hw:].astype(x0_ref.dtype)


def _hyena_pre(u, conv_w, conv_b, hw, out_dtype, tl=512):
    b, l, _ = u.shape
    tl = _tile(l, tl)
    nl = l // tl
    halo = 16
    hb = tl // halo
    nh = l // halo
    w3 = 3 * hw
    return pl.pallas_call(
        functools.partial(_hy_pre_kernel, nl=nl, tl=tl, hw=hw),
        grid=(b, nl),
        in_specs=[pl.BlockSpec((1, tl, w3), lambda bi, i: (bi, i, 0)),
                  pl.BlockSpec((1, halo, w3), lambda bi, i: (bi, jnp.maximum(i * hb - 1, 0), 0)),
                  pl.BlockSpec((1, halo, w3), lambda bi, i: (bi, jnp.minimum((i + 1) * hb, nh - 1), 0)),
                  pl.BlockSpec((3, w3), lambda bi, i: (0, 0)),
                  pl.BlockSpec((1, w3), lambda bi, i: (0, 0))],
        out_specs=[pl.BlockSpec((1, tl, hw), lambda bi, i: (bi, i, 0)),
                   pl.BlockSpec((1, tl, hw), lambda bi, i: (bi, i, 0))],
        out_shape=[jax.ShapeDtypeStruct((b, l, hw), out_dtype), jax.ShapeDtypeStruct((b, l, hw), out_dtype)],
        compiler_params=_cparams("parallel", "parallel"),
    )(u, u, u, conv_w, conv_b.reshape(1, w3))


def _hy_filter_kernel(z_ref, aux_ref, w0_ref, b0_ref, w1_ref, b1_ref, w2_ref, b2_ref, w3_ref, fr_ref, dl_ref,
                      k_ref, ss_ref, *, hw):
    i = pl.program_id(0)
    fr = fr_ref[...]
    h = jnp.sin(fr * (_dot3(z_ref[...], w0_ref[...]) + b0_ref[...]))
    h = jnp.sin(fr * (_dot3(h, w1_ref[...]) + b1_ref[...]))
    h = jnp.sin(fr * (_dot3(h, w2_ref[...]) + b2_ref[...]))
    h = _dot3(h, w3_ref[...])
    aux = aux_ref[...]
    t, m_fwd, m_bwd = aux[:, 0:1], aux[:, 1:2], aux[:, 2:3]
    k = (h[:, :hw] * m_fwd + h[:, hw:] * m_bwd) * jnp.exp(-t * dl_ref[...])
    k_ref[...] = k

    @pl.when(i == 0)
    def _():
        ss_ref[...] = jnp.zeros_like(ss_ref)

    ss_ref[...] += jnp.sum(k * k, axis=0, keepdims=True)


def _hyena_filter(l, hw, fw0, fb0, fw1, fb1, fw2, fb2, fw3, freq):
    n = 2 * l
    hid = HY_FILTER_HIDDEN
    r = jnp.arange(n)
    pos = jnp.where(r < l, r, n - r).astype(F32)
    t = pos / max(l - 1, 1)
    wv = (2.0 * math.pi / l) * pos
    bands = jnp.linspace(1e-4, HY_EMB_BANDS - 1, HY_EMB_BANDS, dtype=F32)
    ang = wv[:, None] * bands[None, :]
    ztab = jnp.concatenate([t[:, None], jnp.cos(ang), -jnp.sin(ang),
                            jnp.zeros((n, hid - 1 - 2 * HY_EMB_BANDS), F32)], axis=-1)
    aux = jnp.zeros((n, LANES), F32)
    aux = aux.at[:, 0].set(t).at[:, 1].set((r < l).astype(F32)).at[:, 2].set((r > l).astype(F32))
    w0 = jnp.concatenate([fw0, jnp.zeros((hid - fw0.shape[0], hid), F32)], axis=0)
    deltas = jnp.abs(jnp.linspace(math.log(HY_DECAY_TARGET) / HY_SLOW_DECAY,
                                  math.log(HY_DECAY_TARGET) / HY_FAST_DECAY, hw, dtype=F32)).reshape(1, hw)
    tr = _tile(n, 1024)
    full = lambda shape: pl.BlockSpec(shape, lambda i: (0, 0))
    return pl.pallas_call(
        functools.partial(_hy_filter_kernel, hw=hw),
        grid=(n // tr,),
        in_specs=[pl.BlockSpec((tr, hid), lambda i: (i, 0)),
                  pl.BlockSpec((tr, LANES), lambda i: (i, 0)),
                  full((hid, hid)), full((1, hid)), full((hid, hid)), full((1, hid)),
                  full((hid, hid)), full((1, hid)), full((hid, 2 * hw)), full((1, hid)), full((1, hw))],
        out_specs=[pl.BlockSpec((tr, hw), lambda i: (i, 0)), full((1, hw))],
        out_shape=[jax.ShapeDtypeStruct((n, hw), F32), jax.ShapeDtypeStruct((1, hw), F32)],
        compiler_params=_cparams("arbitrary"),
    )(ztab, aux, w0, fb0.reshape(1, hid), fw1, fb1.reshape(1, hid), fw2, fb2.reshape(1, hid), fw3,
      freq.reshape(1, hid), deltas)


def _hy_inv_kernel(wr_ref, wi_ref, ur_ref, ui_ref, kr_ref, ki_ref, u_ref, x0_ref, bias_ref, o_ref, acc_ref, *, nf):
    f = pl.program_id(2)

    @pl.when(f == 0)
    def _():
        acc_ref[...] = jnp.zeros_like(acc_ref)

    ur, ui, kr, ki = ur_ref[0], ui_ref[0], kr_ref[0], ki_ref[0]
    packed = jnp.logical_and(lax.broadcasted_iota(I32, ur.shape, 0) == 0, f == 0)
    yr = jnp.where(packed, ur * kr, ur * kr - ui * ki)
    yi = jnp.where(packed, ui * ki, ur * ki + ui * kr)
    acc_ref[...] += _dot(wr_ref[...], yr.astype(BF16)) + _dot(wi_ref[...], yi.astype(BF16))

    @pl.when(f == nf - 1)
    def _():
        u = u_ref[0].astype(F32)
        o_ref[0] = (x0_ref[0].astype(F32) * (acc_ref[...] + u * bias_ref[...])).astype(o_ref.dtype)


def _hyena_inverse(winv, uhat, kspec, u, x0, bias, tm=2048, tf=512):
    b, l, c = u.shape
    nfreq = uhat.shape[1] // 2
    tm = _tile(l, tm)
    tf = _tile(nfreq, tf)
    nf = nfreq // tf
    return pl.pallas_call(
        functools.partial(_hy_inv_kernel, nf=nf),
        grid=(b, l // tm, nf),
        in_specs=[pl.BlockSpec((tm, tf), lambda bi, i, f: (i, f)),
                  pl.BlockSpec((tm, tf), lambda bi, i, f: (i, nf + f)),
                  pl.BlockSpec((1, tf, c), lambda bi, i, f: (bi, f, 0)),
                  pl.BlockSpec((1, tf, c), lambda bi, i, f: (bi, nf + f, 0)),
                  pl.BlockSpec((1, tf, c), lambda bi, i, f: (0, f, 0)),
                  pl.BlockSpec((1, tf, c), lambda bi, i, f: (0, nf + f, 0)),
                  pl.BlockSpec((1, tm, c), lambda bi, i, f: (bi, i, 0)),
                  pl.BlockSpec((1, tm, c), lambda bi, i, f: (bi, i, 0)),
                  pl.BlockSpec((1, c), lambda bi, i, f: (0, 0))],
        out_specs=pl.BlockSpec((1, tm, c), lambda bi, i, f: (bi, i, 0)),
        out_shape=jax.ShapeDtypeStruct((b, l, c), BF16),
        scratch_shapes=[pltpu.VMEM((tm, c), F32)],
        compiler_params=_cparams("parallel", "parallel", "arbitrary"),
    )(winv, winv, uhat, uhat, kspec, kspec, u, x0, bias.reshape(1, c))


LONG_SEQ = 2048
FFT_N2 = 128
FNET_N2 = 64
K1_GROUP = 8
HY_K1_STEP = 4


def _cos_sin(m, n):
    ang = (2.0 * math.pi / n) * (m % n).astype(F32)
    return jnp.cos(ang), jnp.sin(ang)


def _stage1_kernel(*refs, n_in):
    tabs, ins, o_ref = refs[:n_in], refs[n_in:2 * n_in], refs[2 * n_in]
    k1 = o_ref.shape[2]
    for j in range(o_ref.shape[3]):
        acc = _dot(tabs[0][...], ins[0][0, :, j, :].astype(BF16))
        for t_ref, x_ref in zip(tabs[1:], ins[1:]):
            acc = acc + _dot(t_ref[...], x_ref[0, :, j, :].astype(BF16))
        o_ref[0, 0, :, j, :] = acc[:k1]
        o_ref[0, 1, :, j, :] = acc[k1:]


def _dft_stage1(tables, inputs, batch_maps, nb):
    k1 = tables[0].shape[0] // 2
    n2w, c = inputs[0].shape[2:]
    in_specs = [pl.BlockSpec(t.shape, lambda p, j: (0, 0)) for t in tables]
    in_specs += [pl.BlockSpec((1, a.shape[1], SUBLANES, c), functools.partial(lambda p, j, m: (m(p), 0, j, 0), m=m))
                 for a, m in zip(inputs, batch_maps)]
    return pl.pallas_call(
        functools.partial(_stage1_kernel, n_in=len(inputs)),
        grid=(nb, n2w // SUBLANES),
        in_specs=in_specs,
        out_specs=pl.BlockSpec((1, 2, k1, SUBLANES, c), lambda p, j: (p, 0, 0, j, 0)),
        out_shape=jax.ShapeDtypeStruct((nb, 2, k1, n2w, c), F32),
        compiler_params=_cparams("parallel", "parallel"),
    )(*tables, *inputs)


def _fnet2_kernel(a_ref, g_ref, o_ref):
    a = a_ref[0]
    o = _dot(g_ref[0], a.reshape(a.shape[0] * a.shape[1] * a.shape[2], a.shape[3]).astype(BF16))
    o_ref[0] = o.reshape(o_ref.shape[1:])


def _fnet_long(y_cos, y_sin):
    b, l, c = y_cos.shape
    n2w, n1w, kg = FNET_N2, l // FNET_N2, K1_GROUP
    idx = jnp.arange(n1w, dtype=I32)
    c1, s1 = _cos_sin(idx[:, None] * idx[None, :], n1w)
    t_cos = jnp.concatenate([c1, -s1], axis=0).astype(BF16)
    t_sin = jnp.concatenate([-s1, -c1], axis=0).astype(BF16)
    view = lambda a: a.reshape(b, n1w, n2w, c)
    a = _dft_stage1([t_cos, t_sin], [view(y_cos), view(y_sin)], [lambda p: p, lambda p: p], b)
    grp = jnp.arange(n1w // kg, dtype=I32)[:, None, None, None]
    k2 = jnp.arange(n2w, dtype=I32)[None, :, None, None]
    j = jnp.arange(kg, dtype=I32)[None, None, :, None]
    n2 = jnp.arange(n2w, dtype=I32)[None, None, None, :]
    c2, s2 = _cos_sin((n1w * k2 + kg * grp + j) * n2, l)
    eye = jnp.eye(kg, dtype=F32)
    expand = lambda t: jnp.einsum('gkjn,ji->gkjin', t, eye).reshape(n1w // kg, n2w * kg, kg * n2w)
    gbig = (jnp.concatenate([expand(c2), expand(s2)], axis=2) * (1.0 / math.sqrt(l))).astype(BF16)
    out = pl.pallas_call(
        _fnet2_kernel,
        grid=(b, n1w // kg),
        in_specs=[pl.BlockSpec((1, 2, kg, n2w, c), lambda bi, i: (bi, 0, i, 0, 0)),
                  pl.BlockSpec((1,) + gbig.shape[1:], lambda bi, i: (i, 0, 0))],
        out_specs=pl.BlockSpec((1, n2w, kg, c), lambda bi, i: (bi, 0, i, 0)),
        out_shape=jax.ShapeDtypeStruct((b, n2w, n1w, c), F32),
        compiler_params=_cparams("parallel", "parallel"),
    )(a, gbig)
    return out.reshape(b, l, c)


def _stage2(g, a):
    return _dot(g, a.reshape(a.shape[0] * a.shape[1], a.shape[2]).astype(BF16))


def _spec_kernel(a_ref, g_ref, ss_ref, o_ref, *, scale):
    n2w = a_ref.shape[3]
    col = lax.rsqrt(ss_ref[...] + NORM_EPS) * scale
    for j in range(a_ref.shape[2]):
        o = _stage2(g_ref[j], a_ref[0, :, j]) * col
        o_ref[0, j] = o[:n2w]
        o_ref[1, j] = o[n2w:]


def _hy_mid_kernel(a_ref, g_ref, gh_ref, k_ref, o_ref):
    n2w = a_ref.shape[3]
    for j in range(a_ref.shape[2]):
        o = _stage2(g_ref[j], a_ref[0, :, j])
        o_r, o_i = o[:n2w], o[n2w:]
        k_r, k_i = k_ref[0, j], k_ref[1, j]
        y = jnp.concatenate([o_r * k_r - o_i * k_i, o_r * k_i + o_i * k_r], axis=0).astype(BF16)
        z = _dot(gh_ref[j], y)
        o_ref[0, 0, j] = z[:n2w]
        o_ref[0, 1, j] = z[n2w:]


def _hy_out_kernel(c_ref, t_ref, u_ref, x0_ref, bias_ref, o_ref):
    for j in range(c_ref.shape[3]):
        c = c_ref[0, :, :, j, :]
        y = _dot(t_ref[...], c.reshape(c.shape[0] * c.shape[1], c.shape[2]).astype(BF16))
        half = y.shape[0] // 2
        for s in range(2):
            conv = y[s * half:(s + 1) * half] + u_ref[s, :, j, :] * bias_ref[...]
            o_ref[s, :, j, :] = x0_ref[s, :, j, :] * conv


def _hyena_long(uu, x0, bias, k_raw, k_ss):
    b, l, c = uu.shape
    assert b % 2 == 0
    n = 2 * l
    n2w, n1w = FFT_N2, n // FFT_N2
    half, kb = n1w // 2, HY_K1_STEP
    k1 = jnp.arange(n1w, dtype=I32)
    c1, s1 = _cos_sin(k1[:, None] * k1[None, :], n1w)
    bf = lambda t: t.astype(BF16)
    k2 = jnp.arange(n2w, dtype=I32)
    c2, s2 = _cos_sin((n1w * k2[None, :, None] + k1[:, None, None]) * k2[None, None, :], n)
    g = bf(jnp.concatenate([jnp.concatenate([c2, s2], axis=2), jnp.concatenate([-s2, c2], axis=2)], axis=1))
    c2t, s2t = jnp.swapaxes(c2, 1, 2), jnp.swapaxes(s2, 1, 2)
    gh = bf(jnp.concatenate([jnp.concatenate([c2t, -s2t], axis=2), jnp.concatenate([s2t, c2t], axis=2)], axis=1))

    ak = _dft_stage1([bf(jnp.concatenate([c1, -s1], axis=0))], [k_raw.reshape(1, n1w, n2w, c)], [lambda p: 0], 1)
    kspec = pl.pallas_call(
        functools.partial(_spec_kernel, scale=1.0 / n),
        grid=(n1w // kb,),
        in_specs=[pl.BlockSpec((1, 2, kb, n2w, c), lambda i: (0, 0, i, 0, 0)),
                  pl.BlockSpec((kb, 2 * n2w, 2 * n2w), lambda i: (i, 0, 0)),
                  pl.BlockSpec((1, c), lambda i: (0, 0))],
        out_specs=pl.BlockSpec((2, kb, n2w, c), lambda i: (0, i, 0, 0)),
        out_shape=jax.ShapeDtypeStruct((2, n1w, n2w, c), F32),
        compiler_params=_cparams("parallel"),
    )(ak, g, k_ss)

    ch, sh = c1[:, :half], s1[:, :half]
    view = lambda a: a.reshape(b, half, n2w, c)
    a = _dft_stage1([bf(jnp.concatenate([ch, -sh], axis=0)), bf(jnp.concatenate([sh, ch], axis=0))],
                    [view(uu), view(uu)], [lambda p: 2 * p, lambda p: 2 * p + 1], b // 2)
    z = pl.pallas_call(
        _hy_mid_kernel,
        grid=(b // 2, n1w // kb),
        in_specs=[pl.BlockSpec((1, 2, kb, n2w, c), lambda p, i: (p, 0, i, 0, 0)),
                  pl.BlockSpec((kb, 2 * n2w, 2 * n2w), lambda p, i: (i, 0, 0)),
                  pl.BlockSpec((kb, 2 * n2w, 2 * n2w), lambda p, i: (i, 0, 0)),
                  pl.BlockSpec((2, kb, n2w, c), lambda p, i: (0, i, 0, 0))],
        out_specs=pl.BlockSpec((1, 2, kb, n2w, c), lambda p, i: (p, 0, i, 0, 0)),
        out_shape=jax.ShapeDtypeStruct((b // 2, 2, n1w, n2w, c), F32),
        compiler_params=_cparams("parallel", "parallel"),
    )(a, g, gh, kspec)
    cht, sht = ch.T, sh.T
    t_inv = bf(jnp.concatenate([jnp.concatenate([cht, -sht], axis=1), jnp.concatenate([sht, cht], axis=1)], axis=0))
    pair = pl.BlockSpec((2, half, SUBLANES, c), lambda p, j: (p, 0, j, 0))
    out = pl.pallas_call(
        _hy_out_kernel,
        grid=(b // 2, n2w // SUBLANES),
        in_specs=[pl.BlockSpec((1, 2, n1w, SUBLANES, c), lambda p, j: (p, 0, 0, j, 0)),
                  pl.BlockSpec(t_inv.shape, lambda p, j: (0, 0)),
                  pair, pair, pl.BlockSpec((1, c), lambda p, j: (0, 0))],
        out_specs=pair,
        out_shape=jax.ShapeDtypeStruct((b, half, n2w, c), F32),
        compiler_params=_cparams("parallel", "parallel"),
    )(z, t_inv, view(uu), view(x0), bias.reshape(1, c))
    return out.reshape(b, l, c)


def _proj_res_kernel(x_ref, a_ref, b_ref, w_ref, gate_ref, o_ref, *, ka):
    y = _dot(a_ref[0].astype(BF16), w_ref[:ka]) + _dot(b_ref[0].astype(BF16), w_ref[ka:])
    o_ref[0] = x_ref[0] + gate_ref[0] * y


def _proj_residual(x, a, a_blk, b2, b_blk, ka, w, gate, tm=512):
    b, l, d = x.shape
    tm = _tile(l, tm)
    return pl.pallas_call(
        functools.partial(_proj_res_kernel, ka=ka),
        grid=(b, l // tm),
        in_specs=[pl.BlockSpec((1, tm, d), lambda bi, i: (bi, i, 0)),
                  pl.BlockSpec((1, tm, ka), lambda bi, i: (bi, i, a_blk)),
                  pl.BlockSpec((1, tm, ka), lambda bi, i: (bi, i, b_blk)),
                  pl.BlockSpec(w.shape, lambda bi, i: (0, 0)),
                  pl.BlockSpec((1, 1, d), lambda bi, i: (bi, 0, 0))],
        out_specs=pl.BlockSpec((1, tm, d), lambda bi, i: (bi, i, 0)),
        out_shape=jax.ShapeDtypeStruct((b, l, d), F32),
        compiler_params=_cparams("parallel", "parallel"),
    )(x, a, b2, w, gate)


def _ffn_kernel(x_ref, g_ref, sh_ref, sc_ref, gate_ref, wg_ref, wu_ref, wd_ref, o_ref, h_ref, acc_ref, *, nf):
    f = pl.program_id(2)

    @pl.when(f == 0)
    def _():
        h_ref[...] = _norm_mod(x_ref[0], g_ref[...], sh_ref[0], sc_ref[0]).astype(BF16)
        acc_ref[...] = jnp.zeros_like(acc_ref)

    h = h_ref[...]
    mid = _silu(_dot(h, wg_ref[...])) * _dot(h, wu_ref[...])
    acc_ref[...] += _dot(mid.astype(BF16), wd_ref[...])

    @pl.when(f == nf - 1)
    def _():
        o_ref[0] = x_ref[0] + gate_ref[0] * acc_ref[...]


def _dense_ffn(x, g, shift, scale, gate, wg, wu, wd, tm=512, tf=1792):
    b, l, d = x.shape
    dff = wg.shape[1]
    tm = _tile(l, tm)
    tf = _tile(dff, tf)
    nf = dff // tf
    vec = pl.BlockSpec((1, 1, d), lambda bi, i, f: (bi, 0, 0))
    return pl.pallas_call(
        functools.partial(_ffn_kernel, nf=nf),
        grid=(b, l // tm, nf),
        in_specs=[pl.BlockSpec((1, tm, d), lambda bi, i, f: (bi, i, 0)),
                  pl.BlockSpec((1, d), lambda bi, i, f: (0, 0)),
                  vec, vec, vec,
                  pl.BlockSpec((d, tf), lambda bi, i, f: (0, f)),
                  pl.BlockSpec((d, tf), lambda bi, i, f: (0, f)),
                  pl.BlockSpec((tf, d), lambda bi, i, f: (f, 0))],
        out_specs=pl.BlockSpec((1, tm, d), lambda bi, i, f: (bi, i, 0)),
        out_shape=jax.ShapeDtypeStruct((b, l, d), F32),
        scratch_shapes=[pltpu.VMEM((tm, d), BF16), pltpu.VMEM((tm, d), F32)],
        compiler_params=_cparams("parallel", "parallel", "arbitrary"),
    )(x, g.reshape(1, d), shift, scale, gate, wg, wu, wd)


def _head_norm(t, e, et, g_full):
    ss = _dot((t * t).astype(BF16), e)
    rinv = lax.rsqrt(ss * (1.0 / HEAD_DIM) + NORM_EPS)
    hi, lo = _split(rinv)
    return t * (_dot(hi, et) + _dot(lo, et)) * g_full


def _rope(t, cos, sin_lo, sin_hi):
    w = t.shape[1]
    rep = w // LANES
    tile = lambda a: jnp.concatenate([a] * rep, axis=1)
    return (t * tile(cos) + pltpu.roll(t, w - ROPE_HALF, 1) * tile(sin_lo)
            + pltpu.roll(t, ROPE_HALF, 1) * tile(sin_hi))


def _qkv_kernel(x_ref, g_ref, sh_ref, sc_ref, w_ref, e_ref, et_ref, qg_ref, kg_ref, cos_ref, sl_ref, sh2_ref,
                *out_refs, qw, kw, rope):
    k_ref, v_ref = out_refs[-2:]
    h = _norm_mod(x_ref[0], g_ref[...], sh_ref[0], sc_ref[0])
    t = _dot(h.astype(BF16), w_ref[...])
    e, et = e_ref[...], et_ref[...]
    k = _head_norm(t[:, qw:qw + kw], e[:kw], et[:, :kw], kg_ref[...])
    if rope:
        k = _rope(k, cos_ref[...], sl_ref[...], sh2_ref[...])
    k_ref[0] = k.astype(BF16)
    v_ref[0] = t[:, qw + kw:].astype(BF16)
    if qw:
        q = _head_norm(t[:, :qw], e, et, qg_ref[...])
        q = _rope(q, cos_ref[...], sl_ref[...], sh2_ref[...]) * (HEAD_DIM ** -0.5)
        out_refs[0][0] = q.astype(BF16)


def _head_tables(qw):
    lane = jnp.arange(qw)
    e = (lane[:, None] // HEAD_DIM == jnp.arange(LANES)[None, :]).astype(BF16)
    return e, e.T


def _rope_tables(l):
    rows = l // GRID_W
    row = jnp.repeat(jnp.arange(rows, dtype=F32), GRID_W)
    col = jnp.tile(jnp.arange(GRID_W, dtype=F32), rows)
    inv = ROPE_THETA ** (-jnp.arange(0, AXIS_ROPE_DIM, 2, dtype=F32) / AXIS_ROPE_DIM)
    lane = jnp.arange(LANES)
    in_head = lane % HEAD_DIM
    use_col = (in_head // AXIS_ROPE_DIM) == 1
    hi_half = ((in_head % AXIS_ROPE_DIM) // ROPE_HALF) == 1
    freq = inv[in_head % ROPE_HALF]
    ang = jnp.where(use_col[None, :], col[:, None], row[:, None]) * freq[None, :]
    cos, sin = jnp.cos(ang), jnp.sin(ang)
    return cos, jnp.where(hi_half[None, :], 0.0, -sin), jnp.where(hi_half[None, :], sin, 0.0)


def _qkv_project(x, g, shift, scale, w, q_g, k_g, qw, kw, rope, tm=512):
    b, l, d = x.shape
    tm = _tile(l, tm)
    e, et = _head_tables(max(qw, kw))
    n_q = max(qw, kw) // HEAD_DIM
    qg = jnp.tile(q_g, n_q).reshape(1, -1)
    kg = jnp.tile(k_g, kw // HEAD_DIM).reshape(1, kw)
    if rope:
        cos, s_lo, s_hi = _rope_tables(l)
    else:
        cos = s_lo = s_hi = jnp.zeros((l, LANES), F32)
    vec = pl.BlockSpec((1, 1, d), lambda bi, i: (bi, 0, 0))
    full = lambda a: pl.BlockSpec(a.shape, lambda bi, i: (0,) * a.ndim)
    tab = pl.BlockSpec((tm, LANES), lambda bi, i: (i, 0))
    widths = ([qw] if qw else []) + [kw, kw]
    return pl.pallas_call(
        functools.partial(_qkv_kernel, qw=qw, kw=kw, rope=rope),
        grid=(b, l // tm),
        in_specs=[pl.BlockSpec((1, tm, d), lambda bi, i: (bi, i, 0)),
                  pl.BlockSpec((1, d), lambda bi, i: (0, 0)), vec, vec,
                  full(w), full(e), full(et), full(qg), full(kg), tab, tab, tab],
        out_specs=[pl.BlockSpec((1, tm, n), lambda bi, i: (bi, i, 0)) for n in widths],
        out_shape=[jax.ShapeDtypeStruct((b, l, n), BF16) for n in widths],
        compiler_params=_cparams("parallel", "parallel"),
    )(x, g.reshape(1, d), shift, scale, w, e, et, qg, kg, cos, s_lo, s_hi)


def _attn_kernel(q_ref, kp_ref, kc_ref, kn_ref, vp_ref, vc_ref, vn_ref, kx_ref, vx_ref, sink_ref, o_ref, *,
                 seq, group):
    qb = pl.program_id(1)
    bq = q_ref.shape[1]
    lc = kx_ref.shape[1]
    nkeys = 3 * bq + lc
    rows = lax.broadcasted_iota(I32, (group * bq, nkeys), 0)
    cols = lax.broadcasted_iota(I32, (group * bq, nkeys), 1)
    qpos = qb * bq + rows % bq
    kpos = (qb - 1) * bq + cols
    valid = jnp.logical_or(
        cols >= 3 * bq,
        jnp.logical_and(jnp.logical_and(kpos >= 0, kpos < seq), jnp.abs(qpos - kpos) <= WINDOW))
    for h in range(N_KV_HEADS):
        ks = slice(h * HEAD_DIM, (h + 1) * HEAD_DIM)
        kh = jnp.concatenate([kp_ref[0, :, ks], kc_ref[0, :, ks], kn_ref[0, :, ks], kx_ref[0, :, ks]], axis=0)
        vh = jnp.concatenate([vp_ref[0, :, ks], vc_ref[0, :, ks], vn_ref[0, :, ks], vx_ref[0, :, ks]], axis=0)
        qh = jnp.concatenate(
            [q_ref[0, :, (h * group + g) * HEAD_DIM:(h * group + g + 1) * HEAD_DIM] for g in range(group)], axis=0)
        s = lax.dot_general(qh, kh, (((1,), (1,)), ((), ())), preferred_element_type=F32)
        s = jnp.where(valid, s, NEG_INF)
        sk = sink_ref[h]
        m = jnp.maximum(jnp.max(s, axis=1, keepdims=True), sk)
        p = jnp.exp(s - m)
        den = jnp.sum(p, axis=1, keepdims=True) + jnp.exp(sk - m)
        o = _dot(p.astype(BF16), vh) / den
        for g in range(group):
            hq = h * group + g
            o_ref[0, :, hq * HEAD_DIM:(hq + 1) * HEAD_DIM] = o[g * bq:(g + 1) * bq].astype(o_ref.dtype)


def _window_attention(q, k, v, kx, vx, sink):
    b, l, qw = q.shape
    kw = k.shape[2]
    lc = kx.shape[1]
    bq = BLOCK_Q
    nb = l // bq
    group = qw // kw
    sink_tab = jnp.repeat(sink.astype(F32).reshape(N_KV_HEADS, group), bq, axis=1)[..., None]
    kv_prev = pl.BlockSpec((1, bq, kw), lambda bi, i: (bi, jnp.maximum(i - 1, 0), 0))
    kv_cur = pl.BlockSpec((1, bq, kw), lambda bi, i: (bi, i, 0))
    kv_next = pl.BlockSpec((1, bq, kw), lambda bi, i: (bi, jnp.minimum(i + 1, nb - 1), 0))
    kv_ctx = pl.BlockSpec((1, lc, kw), lambda bi, i: (bi, 0, 0))
    return pl.pallas_call(
        functools.partial(_attn_kernel, seq=l, group=group),
        grid=(b, nb),
        in_specs=[pl.BlockSpec((1, bq, qw), lambda bi, i: (bi, i, 0)),
                  kv_prev, kv_cur, kv_next, kv_prev, kv_cur, kv_next, kv_ctx, kv_ctx,
                  pl.BlockSpec(sink_tab.shape, lambda bi, i: (0, 0, 0))],
        out_specs=pl.BlockSpec((1, bq, qw), lambda bi, i: (bi, i, 0)),
        out_shape=jax.ShapeDtypeStruct((b, l, qw), BF16),
        compiler_params=_cparams("parallel", "parallel"),
    )(q, k, k, k, v, v, v, kx, vx, sink_tab)


def _router_kernel(x_ref, g_ref, sh_ref, sc_ref, wh_ref, wl_ref, tri_ref, h_ref, gate_ref, rank_ref, rankt_ref,
                   cnt_ref):
    h = _norm_mod(x_ref[0], g_ref[...], sh_ref[0], sc_ref[0])
    h_ref[0] = h.astype(BF16)
    hi, lo = _split(h)
    logits = _dot(hi, wh_ref[...]) + _dot(hi, wl_ref[...]) + _dot(lo, wh_ref[...])
    lane = lax.broadcasted_iota(I32, logits.shape, 1)
    logits = jnp.where(lane < N_EXPERTS, logits, -jnp.inf)
    m1 = jnp.max(logits, axis=1, keepdims=True)
    i1 = jnp.min(jnp.where(logits == m1, lane, LANES), axis=1, keepdims=True)
    rest = jnp.where(lane == i1, -jnp.inf, logits)
    m2 = jnp.max(rest, axis=1, keepdims=True)
    i2 = jnp.min(jnp.where(rest == m2, lane, LANES), axis=1, keepdims=True)
    e = jnp.exp(m2 - m1)
    g1 = 1.0 / (1.0 + e)
    g2 = e / (1.0 + e)
    pick1, pick2 = lane == i1, lane == i2
    member = jnp.logical_or(pick1, pick2)
    gate_ref[0] = jnp.where(pick1, g1, jnp.where(pick2, g2, 0.0))
    m = jnp.where(member, 1.0, 0.0)
    rank = jnp.where(member, _dot(tri_ref[...], m.astype(BF16)), -1.0)
    rank_ref[0] = rank.astype(I32)
    rankt_ref[0] = jnp.transpose(rank)[:SUBLANES].astype(I32)
    cnt_ref[0] = jnp.sum(m, axis=0, keepdims=True).astype(I32)


def _route(x, g, shift, scale, w_router):
    b, l, d = x.shape
    tm = _tile(l, MOE_TILE)
    nt = l // tm
    wr = jnp.concatenate([w_router, jnp.zeros((d, LANES - w_router.shape[1]), F32)], axis=1)
    wh, wl = _split(wr)
    tri = jnp.tril(jnp.ones((tm, tm), BF16), -1)
    vec = pl.BlockSpec((1, 1, d), lambda bi, i: (bi, 0, 0))
    tok = lambda n: pl.BlockSpec((1, tm, n), lambda bi, i: (bi, i, 0))
    return pl.pallas_call(
        _router_kernel,
        grid=(b, nt),
        in_specs=[tok(d), pl.BlockSpec((1, d), lambda bi, i: (0, 0)), vec, vec,
                  pl.BlockSpec((d, LANES), lambda bi, i: (0, 0)),
                  pl.BlockSpec((d, LANES), lambda bi, i: (0, 0)),
                  pl.BlockSpec((tm, tm), lambda bi, i: (0, 0))],
        out_specs=[tok(d), tok(LANES), tok(LANES),
                   pl.BlockSpec((1, SUBLANES, tm), lambda bi, i: (bi * nt + i, 0, 0)),
                   pl.BlockSpec((1, 1, LANES), lambda bi, i: (bi * nt + i, 0, 0))],
        out_shape=[jax.ShapeDtypeStruct((b, l, d), BF16),
                   jax.ShapeDtypeStruct((b, l, LANES), F32),
                   jax.ShapeDtypeStruct((b, l, LANES), I32),
                   jax.ShapeDtypeStruct((b * nt, SUBLANES, tm), I32),
                   jax.ShapeDtypeStruct((b * nt, 1, LANES), I32)],
        compiler_params=_cparams("parallel", "parallel"),
    )(x, g.reshape(1, d), shift, scale, wh, wl, tri)


def _slot_layout(cnt, rows):
    nt = cnt.shape[0]
    seg = (cnt + SUBLANES - 1) // SUBLANES * SUBLANES
    padded = (jnp.sum(seg, axis=0) + rows - 1) // rows * rows
    stride = padded + rows
    pstart = jnp.cumsum(stride) - stride
    off = pstart[None, :] + jnp.cumsum(seg, axis=0) - seg
    n_slots = (nt * N_EXPERTS * (SUBLANES - 1) + nt * MOE_TILE * TOP_K + rows - 1) // rows * rows \
        + 2 * N_EXPERTS * rows
    nblk = n_slots // rows
    bstart = jnp.arange(nblk, dtype=I32) * rows
    be = jnp.minimum(jnp.sum((bstart[:, None] >= (pstart + stride)[None, :]).astype(I32), axis=1), N_EXPERTS - 1)
    onehot = (be[:, None] == jnp.arange(N_EXPERTS, dtype=I32)[None, :]).astype(I32)
    lo = jnp.sum(onehot * pstart[None, :], axis=1)
    hi = jnp.sum(onehot * (pstart + padded)[None, :], axis=1)
    used = jnp.logical_and(bstart >= lo, bstart < hi)
    zero_blk = jnp.logical_or(jnp.logical_not(used), bstart == hi - rows)
    as_i32 = lambda a: a.reshape(-1).astype(I32)
    return n_slots, as_i32(off), as_i32(cnt), as_i32(zero_blk), as_i32(be), as_i32(used)


def _seg_copy(src_ref, dst_ref, row, sem):
    n = src_ref.shape[0]
    return pltpu.make_async_copy(src_ref, dst_ref.at[pl.ds(pl.multiple_of(row, SUBLANES), n)], sem)


def _chunk_copy(src_ref, dst_ref, row, sem):
    n = dst_ref.shape[0]
    return pltpu.make_async_copy(src_ref.at[pl.ds(pl.multiple_of(row, SUBLANES), n)], dst_ref, sem)


def _dispatch_kernel(off_ref, cnt_ref, zero_ref, h_ref, rt_ref, xs_ref, xbuf, xbuf2, sem, sem2, *, rows):
    i = pl.program_id(0)
    cap = xbuf.shape[1]
    tm = h_ref.shape[1]

    @pl.when(i == 0)
    def _():
        xbuf[0] = jnp.zeros(xbuf.shape[1:], F32)

        def zero_block(j, carry):
            @pl.when(zero_ref[j] > 0)
            def _():
                for part in range(rows // cap):
                    c = _seg_copy(xbuf.at[0], xs_ref, j * rows + part * cap, sem2)
                    c.start()
                    c.wait()
            return carry

        lax.fori_loop(0, zero_ref.shape[0], zero_block, 0)

    h = h_ref[0]
    rt = rt_ref[0]
    riota = lax.broadcasted_iota(I32, (cap, tm), 0)

    def copy(e):
        return _seg_copy(xbuf.at[e], xs_ref, off_ref[i * N_EXPERTS + e], sem.at[e])

    for e in range(N_EXPERTS):
        sel = rt[e:e + 1, :]

        def segment(base):
            return _dot(jnp.where(sel == riota + base, 1.0, 0.0).astype(BF16), h)

        xbuf[e] = segment(0)
        copy(e).start()

        @pl.when(cnt_ref[i * N_EXPERTS + e] > cap)
        def _():
            xbuf2[...] = segment(cap)
            c = _seg_copy(xbuf2, xs_ref, off_ref[i * N_EXPERTS + e] + cap, sem2)
            c.start()
            c.wait()

    for e in range(N_EXPERTS):
        copy(e).wait()


def _dispatch(h, rankt, n_slots, off, cnt, zero_blk, rows):
    b, l, d = h.shape
    tm = _tile(l, MOE_TILE)
    nt = l // tm
    cap = MOE_CAP
    assert tm <= 2 * cap and rows % cap == 0
    grid_spec = pltpu.PrefetchScalarGridSpec(
        num_scalar_prefetch=3,
        grid=(b * nt,),
        in_specs=[pl.BlockSpec((1, tm, d), lambda i, *_: (i // nt, i % nt, 0)),
                  pl.BlockSpec((1, SUBLANES, tm), lambda i, *_: (i, 0, 0))],
        out_specs=pl.BlockSpec(memory_space=pl.ANY),
        scratch_shapes=[pltpu.VMEM((N_EXPERTS, cap, d), F32), pltpu.VMEM((cap, d), F32),
                        pltpu.SemaphoreType.DMA((N_EXPERTS,)), pltpu.SemaphoreType.DMA(())],
    )
    return pl.pallas_call(
        functools.partial(_dispatch_kernel, rows=rows),
        grid_spec=grid_spec,
        out_shape=jax.ShapeDtypeStruct((n_slots, d), F32),
        compiler_params=_cparams("arbitrary"),
    )(off, cnt, zero_blk, h, rankt)


def _moe_kernel(be_ref, used_ref, x_ref, wg_ref, wu_ref, wd_ref, o_ref, acc_ref, *, nf):
    i = pl.program_id(0)
    f = pl.program_id(1)
    used = used_ref[i] > 0

    @pl.when(jnp.logical_and(used, f == 0))
    def _():
        acc_ref[...] = jnp.zeros_like(acc_ref)

    @pl.when(used)
    def _():
        x = x_ref[...].astype(BF16)
        mid = _silu(_dot(x, wg_ref[0])) * _dot(x, wu_ref[0])
        acc_ref[...] += _dot(mid.astype(BF16), wd_ref[0])

    @pl.when(jnp.logical_and(used, f == nf - 1))
    def _():
        o_ref[...] = acc_ref[...]

    @pl.when(jnp.logical_and(jnp.logical_not(used), f == nf - 1))
    def _():
        o_ref[...] = jnp.zeros_like(o_ref)


def _expert_ffn(xs, block_e, used, wg, wu, wd, rows, tf=1792):
    s, d = xs.shape
    dff = wg.shape[2]
    tf = _tile(dff, tf)
    nf = dff // tf
    nblk = s // rows
    chunk = lambda i, f, us: f * us[i] + (nf - 1) * (1 - us[i])
    grid_spec = pltpu.PrefetchScalarGridSpec(
        num_scalar_prefetch=2,
        grid=(nblk, nf),
        in_specs=[pl.BlockSpec((rows, d), lambda i, f, be, us: (i, 0)),
                  pl.BlockSpec((1, d, tf), lambda i, f, be, us: (be[i], 0, chunk(i, f, us))),
                  pl.BlockSpec((1, d, tf), lambda i, f, be, us: (be[i], 0, chunk(i, f, us))),
                  pl.BlockSpec((1, tf, d), lambda i, f, be, us: (be[i], chunk(i, f, us), 0))],
        out_specs=pl.BlockSpec((rows, d), lambda i, f, be, us: (i, 0)),
        scratch_shapes=[pltpu.VMEM((rows, d), F32)],
    )
    return pl.pallas_call(
        functools.partial(_moe_kernel, nf=nf),
        grid_spec=grid_spec,
        out_shape=jax.ShapeDtypeStruct((s, d), F32),
        compiler_params=_cparams("arbitrary", "arbitrary"),
    )(block_e, used, xs, wg, wu, wd)


def _combine_kernel(off_ref, cnt_ref, x_ref, gate_ref, rg_ref, rank_ref, ys_ref, o_ref, buf, buf2, acc_ref, sem,
                    sem2, *, n_tiles):
    i = pl.program_id(0)
    slot = i % 2
    cap = buf.shape[2]
    tm = x_ref.shape[1]

    def chunk(t, s, e):
        return _chunk_copy(ys_ref, buf.at[s, e], off_ref[t * N_EXPERTS + e], sem.at[s, e])

    @pl.when(i == 0)
    def _():
        for e in range(N_EXPERTS):
            chunk(0, 0, e).start()

    @pl.when(i + 1 < n_tiles)
    def _():
        for e in range(N_EXPERTS):
            chunk(i + 1, 1 - slot, e).start()

    rank = rank_ref[0]
    rg = rg_ref[0]
    liota = lax.broadcasted_iota(I32, (tm, cap), 1)

    def picked(e, base, rows_ref):
        q = jnp.where(rank[:, e:e + 1] == liota + base, 1.0, 0.0).astype(BF16)
        return rg[:, e:e + 1] * _dot(q, rows_ref[...].astype(BF16))

    y = jnp.zeros(acc_ref.shape, F32)
    for e in range(N_EXPERTS):
        chunk(i, slot, e).wait()
        y = y + picked(e, 0, buf.at[slot, e])
    acc_ref[...] = y

    for e in range(N_EXPERTS):
        @pl.when(cnt_ref[i * N_EXPERTS + e] > cap)
        def _():
            c = _chunk_copy(ys_ref, buf2, off_ref[i * N_EXPERTS + e] + cap, sem2)
            c.start()
            c.wait()
            acc_ref[...] += picked(e, cap, buf2)

    o_ref[0] = x_ref[0] + gate_ref[0] * acc_ref[...]


def _moe_combine(x, gate, route_gates, rank, ys, off, cnt):
    b, l, d = x.shape
    tm = _tile(l, MOE_TILE)
    nt = l // tm
    cap = MOE_CAP
    tok = lambda n: pl.BlockSpec((1, tm, n), lambda i, *_: (i // nt, i % nt, 0))
    grid_spec = pltpu.PrefetchScalarGridSpec(
        num_scalar_prefetch=2,
        grid=(b * nt,),
        in_specs=[tok(d), pl.BlockSpec((1, 1, d), lambda i, *_: (i // nt, 0, 0)), tok(LANES), tok(LANES),
                  pl.BlockSpec(memory_space=pl.ANY)],
        out_specs=tok(d),
        scratch_shapes=[pltpu.VMEM((2, N_EXPERTS, cap, d), F32), pltpu.VMEM((cap, d), F32),
                        pltpu.VMEM((tm, d), F32),
                        pltpu.SemaphoreType.DMA((2, N_EXPERTS)), pltpu.SemaphoreType.DMA(())],
    )
    return pl.pallas_call(
        functools.partial(_combine_kernel, n_tiles=b * nt),
        grid_spec=grid_spec,
        out_shape=jax.ShapeDtypeStruct((b, l, d), F32),
        compiler_params=_cparams("arbitrary"),
    )(off, cnt, x, gate, route_gates, rank, ys)


def _angles(rows, cols, n):
    m = (rows[:, None] * cols[None, :]) % n
    return (2.0 * math.pi / n) * m.astype(F32)


def _fnet_table(l):
    idx = jnp.arange(l, dtype=I32)
    ang = _angles(idx, idx, l)
    s = 1.0 / math.sqrt(l)
    return jnp.concatenate([jnp.cos(ang) * s, jnp.sin(ang) * (-s)], axis=1).astype(BF16)


def _rfft_table(n):
    half = n // 2
    f = jnp.arange(half, dtype=I32)
    t = jnp.arange(n, dtype=I32)
    ang = _angles(f, t, n)
    top = jnp.cos(ang)
    bot = -jnp.sin(ang)
    nyq = jnp.cos(_angles(jnp.full((1,), half, I32), t, n))
    bot = jnp.concatenate([nyq, bot[1:]], axis=0)
    return jnp.concatenate([top, bot], axis=0).astype(BF16)


def _group_dft_table(width):
    gd = FNET_GROUP_DIM
    idx = jnp.arange(gd, dtype=I32)
    ang = _angles(idx, idx, gd)
    s = 1.0 / math.sqrt(gd)
    eye = jnp.eye(width // gd, dtype=F32)
    return jnp.concatenate([jnp.kron(eye, jnp.cos(ang) * s), jnp.kron(eye, jnp.sin(ang) * s)], axis=1)


def _even_mixer(x, g1, shift, scale, gate, w_in_f, w_out, conv_w, conv_b, hy_bias, filt, fw, hw):
    b, l, d = x.shape
    n = 2 * l
    long_seq = l >= LONG_SEQ
    seq_dtype = F32 if long_seq else BF16
    u, y_cos, y_sin = _norm_mod_matmul(x, g1, shift, scale, w_in_f, [3 * hw, fw, fw], [BF16, seq_dtype, seq_dtype])
    uu, x0 = _hyena_pre(u, conv_w, conv_b, hw, seq_dtype)
    k_raw, k_ss = filt(l)
    if long_seq:
        a = _fnet_long(y_cos, y_sin)
        hy = _hyena_long(uu, x0, hy_bias, k_raw, k_ss)
    else:
        y_both = jnp.concatenate([y_cos, y_sin], axis=2)
        a = _left_dft(_fnet_table(l), y_both, lambda bi, k, tk: (bi, k % (l // tk), k // (l // tk)), n, fw, BF16,
                      tk=min(l, 1024))
        wf = _rfft_table(n)
        wts = jnp.concatenate([jnp.ones((1,), F32), jnp.full((l - 1,), 2.0, F32)]) / n
        row_scale = jnp.concatenate([wts, wts]).reshape(n, 1)
        kspec = _left_dft(wf, k_raw[None], lambda bi, k, tk: (0, k, 0), n, hw, F32,
                          row_scale=row_scale, col_sumsq=k_ss)
        uhat = _left_dft(wf, uu, lambda bi, k, tk: (bi, k, 0), l, hw, F32)
        hy = _hyena_inverse(wf[:, :l].T, uhat, kspec, uu, x0, hy_bias)
    return _proj_residual(x, a, 0, hy, 0, fw, w_out, gate)


def kernel(x, c, ctx, c_ctx, ada_w, ada_b, norm1_g, norm2_g, ev_w_in, ev_w_out, hy_conv_w, hy_conv_b, hy_bias,
           hf_w0, hf_b0, hf_w1, hf_b1, hf_w2, hf_b2, hf_w3, hf_freq, ffn_w_gate, ffn_w_up, ffn_w_down, od_w_qkv,
           od_w_out, q_norm_g, k_norm_g, attn_sink, moe_router, moe_w_gate, moe_w_up, moe_w_down):
    b, l, d = x.shape
    lc = ctx.shape[1]
    assert ada_w.shape[0] == 2, "this implementation covers the two-layer (even, odd) stack"
    hw = hy_bias.shape[1]
    fw = ev_w_in.shape[2] - 3 * hw
    qw = od_w_out.shape[1]
    kw = (od_w_qkv.shape[2] - qw) // 2

    rows = (b + 1 + 7) // 8 * 8
    cc = jnp.concatenate([c, c_ctx[None, :], jnp.zeros((rows - b - 1, d), F32)], axis=0)
    mod = _ada_vectors(cc, ada_w, ada_b)
    ml = [[mod[i, :b, None, m * d:(m + 1) * d] for m in range(N_MOD)] for i in range(2)]
    mc = [[mod[i, b:b + 1, None, m * d:(m + 1) * d] for m in range(N_MOD)] for i in range(2)]

    w_in = ev_w_in[0]
    w_fnet = _matmul3(w_in[:, :fw], _group_dft_table(fw))
    w_in_f = jnp.concatenate([w_in[:, fw:], w_fnet], axis=1).astype(BF16)
    w_out0 = ev_w_out[0].astype(BF16)
    filt = lambda seq: _hyena_filter(seq, hw, hf_w0[0], hf_b0[0], hf_w1[0], hf_b1[0], hf_w2[0], hf_b2[0],
                                     hf_w3[0], hf_freq[0])
    wg, wu, wd = ffn_w_gate[0].astype(BF16), ffn_w_up[0].astype(BF16), ffn_w_down[0].astype(BF16)

    x = _even_mixer(x, norm1_g[0], ml[0][0], ml[0][1], ml[0][2], w_in_f, w_out0, hy_conv_w[0], hy_conv_b[0],
                    hy_bias[0], filt, fw, hw)
    x = _dense_ffn(x, norm2_g[0], ml[0][3], ml[0][4], ml[0][5], wg, wu, wd)

    bc = lambda v: jnp.broadcast_to(v, (b, 1, d))
    ctx = _even_mixer(ctx, norm1_g[0], bc(mc[0][0]), bc(mc[0][1]), bc(mc[0][2]), w_in_f, w_out0, hy_conv_w[0],
                      hy_conv_b[0], hy_bias[0], filt, fw, hw)
    ctx = _dense_ffn(ctx.reshape(1, b * lc, d), norm2_g[0], mc[0][3], mc[0][4], mc[0][5], wg, wu, wd)

    w_qkv = od_w_qkv[0].astype(BF16)
    q, k, v = _qkv_project(x, norm1_g[1], ml[1][0], ml[1][1], w_qkv, q_norm_g[0], k_norm_g[0], qw, kw, True)
    kx, vx = _qkv_project(ctx, norm1_g[1], mc[1][0], mc[1][1], w_qkv[:, qw:], q_norm_g[0], k_norm_g[0], 0, kw,
                             False)
    o = _window_attention(q, k, v, kx.reshape(b, lc, kw), vx.reshape(b, lc, kw), attn_sink[0])
    half = qw // 2
    x = _proj_residual(x, o, 0, o, 1, half, od_w_out[0].astype(BF16), ml[1][2])

    h2, gates, rank, rankt, cnt = _route(x, norm2_g[1], ml[1][3], ml[1][4], moe_router[0])
    n_slots, off, cnt, zero_blk, block_e, used = _slot_layout(cnt[:, 0, :N_EXPERTS], MOE_ROWS)
    xs = _dispatch(h2, rankt, n_slots, off, cnt, zero_blk, MOE_ROWS)
    ys = _expert_ffn(xs, block_e, used, moe_w_gate[0].astype(BF16), moe_w_up[0].astype(BF16),
                     moe_w_down[0].astype(BF16), MOE_ROWS)
    return _moe_combine(x, ml[1][5], gates, rank, ys, off, cnt)
```

```python
import functools
import math

import jax
import jax.numpy as jnp
from jax import lax
from jax.experimental import pallas as pl
from jax.experimental.pallas import tpu as pltpu

F32 = jnp.float32
BF16 = jnp.bfloat16
I32 = jnp.int32

NORM_EPS = 1e-6
NEG_INF = -1e30
N_MOD = 6

FNET_GROUP_DIM = 128
HY_EMB_BANDS = 16
HY_FILTER_HIDDEN = 64
HY_DECAY_TARGET = 1e-2
HY_FAST_DECAY = 0.3
HY_SLOW_DECAY = 1.5

HEAD_DIM = 64
N_KV_HEADS = 4
GRID_W = 64
WINDOW = 128
BLOCK_Q = 128
ROPE_THETA = 10000.0
AXIS_ROPE_DIM = HEAD_DIM // 2
ROPE_HALF = AXIS_ROPE_DIM // 2
N_EXPERTS = 8
TOP_K = 2

LANES = 128
SUBLANES = 8
VMEM_LIMIT = 56 * 1024 * 1024
MOE_TILE = 512
MOE_ROWS = 512
MOE_CAP = 256


def _cparams(*sem):
    return pltpu.CompilerParams(dimension_semantics=sem, vmem_limit_bytes=VMEM_LIMIT)


def _split(a):
    hi = a.astype(BF16)
    lo = (a - hi.astype(F32)).astype(BF16)
    return hi, lo


def _dot(a, b):
    return jnp.dot(a, b, preferred_element_type=F32)


def _dot3(a, b):
    ah, al = _split(a)
    bh, bl = _split(b)
    return _dot(ah, bh) + _dot(ah, bl) + _dot(al, bh)


def _silu(t):
    return t / (1.0 + jnp.exp(-t))


def _norm_mod(x, g, shift, scale):
    ms = jnp.mean(x * x, axis=-1, keepdims=True)
    y = x * lax.rsqrt(ms + NORM_EPS) * g
    return y * (1.0 + scale) + shift


def _tile(n, pref):
    t = min(n, pref)
    assert n % t == 0, (n, pref)
    return t


def _ada_kernel(c_ref, w_ref, b_ref, o_ref):
    o_ref[0] = _dot3(_silu(c_ref[...]), w_ref[0]) + b_ref[0]


def _ada_vectors(cc, ada_w, ada_b):
    depth, d, n = ada_w.shape
    rows = cc.shape[0]
    tn = _tile(n, 1536)
    return pl.pallas_call(
        _ada_kernel,
        grid=(depth, n // tn),
        in_specs=[pl.BlockSpec((rows, d), lambda l, j: (0, 0)),
                  pl.BlockSpec((1, d, tn), lambda l, j: (l, 0, j)),
                  pl.BlockSpec((1, 1, tn), lambda l, j: (l, 0, j))],
        out_specs=pl.BlockSpec((1, rows, tn), lambda l, j: (l, 0, j)),
        out_shape=jax.ShapeDtypeStruct((depth, rows, n), F32),
        compiler_params=_cparams("arbitrary", "arbitrary"),
    )(cc, ada_w, ada_b.reshape(depth, 1, n))


def _mm3_kernel(a_ref, b_ref, o_ref):
    o_ref[...] = _dot3(a_ref[...], b_ref[...])


def _matmul3(a, b):
    m, _ = a.shape
    n = b.shape[1]
    return pl.pallas_call(_mm3_kernel, out_shape=jax.ShapeDtypeStruct((m, n), F32),
                          compiler_params=_cparams())(a, b)


def _nmm_kernel(x_ref, g_ref, sh_ref, sc_ref, w_ref, *o_refs):
    h = _norm_mod(x_ref[0], g_ref[...], sh_ref[0], sc_ref[0])
    y = _dot(h.astype(BF16), w_ref[...])
    col = 0
    for o_ref in o_refs:
        n = o_ref.shape[2]
        o_ref[0] = y[:, col:col + n].astype(o_ref.dtype)
        col += n


def _norm_mod_matmul(x, g, shift, scale, w, widths, out_dtypes, tm=512):
    b, l, d = x.shape
    n = w.shape[1]
    assert sum(widths) == n
    tm = _tile(l, tm)
    return pl.pallas_call(
        _nmm_kernel,
        grid=(b, l // tm),
        in_specs=[pl.BlockSpec((1, tm, d), lambda bi, i: (bi, i, 0)),
                  pl.BlockSpec((1, d), lambda bi, i: (0, 0)),
                  pl.BlockSpec((1, 1, d), lambda bi, i: (bi, 0, 0)),
                  pl.BlockSpec((1, 1, d), lambda bi, i: (bi, 0, 0)),
                  pl.BlockSpec((d, n), lambda bi, i: (0, 0))],
        out_specs=[pl.BlockSpec((1, tm, wd), lambda bi, i: (bi, i, 0)) for wd in widths],
        out_shape=[jax.ShapeDtypeStruct((b, l, wd), dt) for wd, dt in zip(widths, out_dtypes)],
        compiler_params=_cparams("parallel", "parallel"),
    )(x, g.reshape(1, d), shift, scale, w)


def _ldft_kernel(w_ref, r_ref, o_ref, acc_ref, *, nk):
    k = pl.program_id(2)

    @pl.when(k == 0)
    def _():
        acc_ref[...] = jnp.zeros_like(acc_ref)

    acc_ref[...] += _dot(w_ref[...], r_ref[0].astype(BF16))

    @pl.when(k == nk - 1)
    def _():
        o_ref[0] = acc_ref[...].astype(o_ref.dtype)


def _ldft_scaled_kernel(w_ref, r_ref, rs_ref, ss_ref, o_ref, acc_ref, *, nk):
    k = pl.program_id(2)

    @pl.when(k == 0)
    def _():
        acc_ref[...] = jnp.zeros_like(acc_ref)

    acc_ref[...] += _dot(w_ref[...], r_ref[0].astype(BF16))

    @pl.when(k == nk - 1)
    def _():
        o_ref[0] = acc_ref[...] * rs_ref[...] * lax.rsqrt(ss_ref[...] + NORM_EPS)


def _left_dft(w, rhs, rhs_map, kdim, c, out_dtype, tm=2048, tk=1024, row_scale=None, col_sumsq=None):
    m = w.shape[0]
    nb = rhs.shape[0]
    tm = _tile(m, tm)
    tk = _tile(kdim, tk)
    nk = kdim // tk
    in_specs = [pl.BlockSpec((tm, tk), lambda b, i, k: (i, k)),
                pl.BlockSpec((1, tk, c), lambda b, i, k: rhs_map(b, k, tk))]
    args = [w, rhs]
    if row_scale is None:
        body = functools.partial(_ldft_kernel, nk=nk)
    else:
        body = functools.partial(_ldft_scaled_kernel, nk=nk)
        in_specs += [pl.BlockSpec((tm, 1), lambda b, i, k: (i, 0)),
                     pl.BlockSpec((1, c), lambda b, i, k: (0, 0))]
        args += [row_scale, col_sumsq]
    return pl.pallas_call(
        body,
        grid=(nb, m // tm, nk),
        in_specs=in_specs,
        out_specs=pl.BlockSpec((1, tm, c), lambda b, i, k: (b, i, 0)),
        out_shape=jax.ShapeDtypeStruct((nb, m, c), out_dtype),
        scratch_shapes=[pltpu.VMEM((tm, c), F32)],
        compiler_params=_cparams("parallel", "parallel", "arbitrary"),
    )(*args)


def _hy_pre_kernel(u_ref, up_ref, un_ref, cw_ref, cb_ref, uo_ref, x0_ref, *, nl, tl, hw):
    i = pl.program_id(1)
    u = u_ref[0].astype(F32)
    halo = up_ref.shape[1]
    prev = up_ref[0].astype(F32)[halo - 1:halo]
    nxt = un_ref[0].astype(F32)[0:1]
    prev = jnp.where(i == 0, 0.0, prev)
    nxt = jnp.where(i == nl - 1, 0.0, nxt)
    rows = lax.broadcasted_iota(I32, u.shape, 0)
    um = jnp.where(rows == 0, prev, pltpu.roll(u, 1, 0))
    up = jnp.where(rows == tl - 1, nxt, pltpu.roll(u, tl - 1, 0))
    cw = cw_ref[...]
    z = um * cw[0:1] + u * cw[1:2] + up * cw[2:3] + cb_ref[...]
    uo_ref[0] = (z[:, :hw] * z[:, hw:2 * hw]).astype(uo_ref.dtype)
    x0_ref[0] = z[:, 2 * hw:].astype(x0_ref.dtype)


def _hyena_pre(u, conv_w, conv_b, hw, out_dtype, tl=512):
    b, l, _ = u.shape
    tl = _tile(l, tl)
    nl = l // tl
    halo = 16
    hb = tl // halo
    nh = l // halo
    w3 = 3 * hw
    return pl.pallas_call(
        functools.partial(_hy_pre_kernel, nl=nl, tl=tl, hw=hw),
        grid=(b, nl),
        in_specs=[pl.BlockSpec((1, tl, w3), lambda bi, i: (bi, i, 0)),
                  pl.BlockSpec((1, halo, w3), lambda bi, i: (bi, jnp.maximum(i * hb - 1, 0), 0)),
                  pl.BlockSpec((1, halo, w3), lambda bi, i: (bi, jnp.minimum((i + 1) * hb, nh - 1), 0)),
                  pl.BlockSpec((3, w3), lambda bi, i: (0, 0)),
                  pl.BlockSpec((1, w3), lambda bi, i: (0, 0))],
        out_specs=[pl.BlockSpec((1, tl, hw), lambda bi, i: (bi, i, 0)),
                   pl.BlockSpec((1, tl, hw), lambda bi, i: (bi, i, 0))],
        out_shape=[jax.ShapeDtypeStruct((b, l, hw), out_dtype), jax.ShapeDtypeStruct((b, l, hw), out_dtype)],
        compiler_params=_cparams("parallel", "parallel"),
    )(u, u, u, conv_w, conv_b.reshape(1, w3))


def _hy_filter_kernel(z_ref, aux_ref, w0_ref, b0_ref, w1_ref, b1_ref, w2_ref, b2_ref, w3_ref, fr_ref, dl_ref,
                      k_ref, ss_ref, *, hw):
    i = pl.program_id(0)
    fr = fr_ref[...]
    h = jnp.sin(fr * (_dot3(z_ref[...], w0_ref[...]) + b0_ref[...]))
    h = jnp.sin(fr * (_dot3(h, w1_ref[...]) + b1_ref[...]))
    h = jnp.sin(fr * (_dot3(h, w2_ref[...]) + b2_ref[...]))
    h = _dot3(h, w3_ref[...])
    aux = aux_ref[...]
    t, m_fwd, m_bwd = aux[:, 0:1], aux[:, 1:2], aux[:, 2:3]
    k = (h[:, :hw] * m_fwd + h[:, hw:] * m_bwd) * jnp.exp(-t * dl_ref[...])
    k_ref[...] = k

    @pl.when(i == 0)
    def _():
        ss_ref[...] = jnp.zeros_like(ss_ref)

    ss_ref[...] += jnp.sum(k * k, axis=0, keepdims=True)


def _hyena_filter(l, hw, fw0, fb0, fw1, fb1, fw2, fb2, fw3, freq):
    n = 2 * l
    hid = HY_FILTER_HIDDEN
    r = jnp.arange(n)
    pos = jnp.where(r < l, r, n - r).astype(F32)
    t = pos / max(l - 1, 1)
    wv = (2.0 * math.pi / l) * pos
    bands = jnp.linspace(1e-4, HY_EMB_BANDS - 1, HY_EMB_BANDS, dtype=F32)
    ang = wv[:, None] * bands[None, :]
    ztab = jnp.concatenate([t[:, None], jnp.cos(ang), -jnp.sin(ang),
                            jnp.zeros((n, hid - 1 - 2 * HY_EMB_BANDS), F32)], axis=-1)
    aux = jnp.zeros((n, LANES), F32)
    aux = aux.at[:, 0].set(t).at[:, 1].set((r < l).astype(F32)).at[:, 2].set((r > l).astype(F32))
    w0 = jnp.concatenate([fw0, jnp.zeros((hid - fw0.shape[0], hid), F32)], axis=0)
    deltas = jnp.abs(jnp.linspace(math.log(HY_DECAY_TARGET) / HY_SLOW_DECAY,
                                  math.log(HY_DECAY_TARGET) / HY_FAST_DECAY, hw, dtype=F32)).reshape(1, hw)
    tr = _tile(n, 1024)
    full = lambda shape: pl.BlockSpec(shape, lambda i: (0, 0))
    return pl.pallas_call(
        functools.partial(_hy_filter_kernel, hw=hw),
        grid=(n // tr,),
        in_specs=[pl.BlockSpec((tr, hid), lambda i: (i, 0)),
                  pl.BlockSpec((tr, LANES), lambda i: (i, 0)),
                  full((hid, hid)), full((1, hid)), full((hid, hid)), full((1, hid)),
                  full((hid, hid)), full((1, hid)), full((hid, 2 * hw)), full((1, hid)), full((1, hw))],
        out_specs=[pl.BlockSpec((tr, hw), lambda i: (i, 0)), full((1, hw))],
        out_shape=[jax.ShapeDtypeStruct((n, hw), F32), jax.ShapeDtypeStruct((1, hw), F32)],
        compiler_params=_cparams("arbitrary"),
    )(ztab, aux, w0, fb0.reshape(1, hid), fw1, fb1.reshape(1, hid), fw2, fb2.reshape(1, hid), fw3,
      freq.reshape(1, hid), deltas)


def _hy_inv_kernel(wr_ref, wi_ref, ur_ref, ui_ref, kr_ref, ki_ref, u_ref, x0_ref, bias_ref, o_ref, acc_ref, *, nf):
    f = pl.program_id(2)

    @pl.when(f == 0)
    def _():
        acc_ref[...] = jnp.zeros_like(acc_ref)

    ur, ui, kr, ki = ur_ref[0], ui_ref[0], kr_ref[0], ki_ref[0]
    packed = jnp.logical_and(lax.broadcasted_iota(I32, ur.shape, 0) == 0, f == 0)
    yr = jnp.where(packed, ur * kr, ur * kr - ui * ki)
    yi = jnp.where(packed, ui * ki, ur * ki + ui * kr)
    acc_ref[...] += _dot(wr_ref[...], yr.astype(BF16)) + _dot(wi_ref[...], yi.astype(BF16))

    @pl.when(f == nf - 1)
    def _():
        u = u_ref[0].astype(F32)
        o_ref[0] = (x0_ref[0].astype(F32) * (acc_ref[...] + u * bias_ref[...])).astype(o_ref.dtype)


def _hyena_inverse(winv, uhat, kspec, u, x0, bias, tm=2048, tf=512):
    b, l, c = u.shape
    nfreq = uhat.shape[1] // 2
    tm = _tile(l, tm)
    tf = _tile(nfreq, tf)
    nf = nfreq // tf
    return pl.pallas_call(
        functools.partial(_hy_inv_kernel, nf=nf),
        grid=(b, l // tm, nf),
        in_specs=[pl.BlockSpec((tm, tf), lambda bi, i, f: (i, f)),
                  pl.BlockSpec((tm, tf), lambda bi, i, f: (i, nf + f)),
                  pl.BlockSpec((1, tf, c), lambda bi, i, f: (bi, f, 0)),
                  pl.BlockSpec((1, tf, c), lambda bi, i, f: (bi, nf + f, 0)),
                  pl.BlockSpec((1, tf, c), lambda bi, i, f: (0, f, 0)),
                  pl.BlockSpec((1, tf, c), lambda bi, i, f: (0, nf + f, 0)),
                  pl.BlockSpec((1, tm, c), lambda bi, i, f: (bi, i, 0)),
                  pl.BlockSpec((1, tm, c), lambda bi, i, f: (bi, i, 0)),
                  pl.BlockSpec((1, c), lambda bi, i, f: (0, 0))],
        out_specs=pl.BlockSpec((1, tm, c), lambda bi, i, f: (bi, i, 0)),
        out_shape=jax.ShapeDtypeStruct((b, l, c), BF16),
        scratch_shapes=[pltpu.VMEM((tm, c), F32)],
        compiler_params=_cparams("parallel", "parallel", "arbitrary"),
    )(winv, winv, uhat, uhat, kspec, kspec, u, x0, bias.reshape(1, c))


LONG_SEQ = 2048
FFT_N2 = 128
FNET_N2 = 64
K1_GROUP = 8
HY_K1_STEP = 4


def _cos_sin(m, n):
    ang = (2.0 * math.pi / n) * (m % n).astype(F32)
    return jnp.cos(ang), jnp.sin(ang)


def _rows(x):
    return x.reshape(-1, x.shape[-1]).astype(BF16)


def _per_sublane(t):
    return jnp.kron(t, jnp.eye(SUBLANES, dtype=F32)).astype(BF16)


def _stage1_kernel(*refs, n_in):
    tabs, ins, o_ref = refs[:n_in], refs[n_in:2 * n_in], refs[2 * n_in]
    acc = _dot(tabs[0][...], _rows(ins[0][0]))
    for t_ref, x_ref in zip(tabs[1:], ins[1:]):
        acc = acc + _dot(t_ref[...], _rows(x_ref[0]))
    o_ref[0] = acc.reshape(o_ref.shape[1:])


def _dft_stage1(tables, inputs, batch_maps, nb):
    k1 = tables[0].shape[0] // 2
    n2w, c = inputs[0].shape[2:]
    tables = [_per_sublane(t) for t in tables]
    in_specs = [pl.BlockSpec(t.shape, lambda p, j: (0, 0)) for t in tables]
    in_specs += [pl.BlockSpec((1, a.shape[1], SUBLANES, c), functools.partial(lambda p, j, m: (m(p), 0, j, 0), m=m))
                 for a, m in zip(inputs, batch_maps)]
    return pl.pallas_call(
        functools.partial(_stage1_kernel, n_in=len(inputs)),
        grid=(nb, n2w // SUBLANES),
        in_specs=in_specs,
        out_specs=pl.BlockSpec((1, 2, k1, SUBLANES, c), lambda p, j: (p, 0, 0, j, 0)),
        out_shape=jax.ShapeDtypeStruct((nb, 2, k1, n2w, c), F32),
        compiler_params=_cparams("parallel", "parallel"),
    )(*tables, *inputs)


def _fnet2_kernel(a_ref, g_ref, o_ref):
    o_ref[0] = _dot(g_ref[0], _rows(a_ref[0])).reshape(o_ref.shape[1:])


def _fnet_long(y_cos, y_sin):
    b, l, c = y_cos.shape
    n2w, n1w, kg = FNET_N2, l // FNET_N2, K1_GROUP
    idx = jnp.arange(n1w, dtype=I32)
    c1, s1 = _cos_sin(idx[:, None] * idx[None, :], n1w)
    t_cos = jnp.concatenate([c1, -s1], axis=0)
    t_sin = jnp.concatenate([-s1, -c1], axis=0)
    view = lambda a: a.reshape(b, n1w, n2w, c)
    a = _dft_stage1([t_cos, t_sin], [view(y_cos), view(y_sin)], [lambda p: p, lambda p: p], b)
    grp = jnp.arange(n1w // kg, dtype=I32)[:, None, None, None]
    k2 = jnp.arange(n2w, dtype=I32)[None, :, None, None]
    j = jnp.arange(kg, dtype=I32)[None, None, :, None]
    n2 = jnp.arange(n2w, dtype=I32)[None, None, None, :]
    c2, s2 = _cos_sin((n1w * k2 + kg * grp + j) * n2, l)
    eye = jnp.eye(kg, dtype=F32)
    expand = lambda t: jnp.einsum('gkjn,ji->gkjin', t, eye).reshape(n1w // kg, n2w * kg, kg * n2w)
    gbig = (jnp.concatenate([expand(c2), expand(s2)], axis=2) * (1.0 / math.sqrt(l))).astype(BF16)
    out = pl.pallas_call(
        _fnet2_kernel,
        grid=(b, n1w // kg),
        in_specs=[pl.BlockSpec((1, 2, kg, n2w, c), lambda bi, i: (bi, 0, i, 0, 0)),
                  pl.BlockSpec((1,) + gbig.shape[1:], lambda bi, i: (i, 0, 0))],
        out_specs=pl.BlockSpec((1, n2w, kg, c), lambda bi, i: (bi, 0, i, 0)),
        out_shape=jax.ShapeDtypeStruct((b, n2w, n1w, c), F32),
        compiler_params=_cparams("parallel", "parallel"),
    )(a, gbig)
    return out.reshape(b, l, c)


def _stage2(g, a):
    return _dot(g, _rows(a))


def _spec_kernel(a_ref, g_ref, ss_ref, o_ref, *, scale):
    n2w = a_ref.shape[3]
    col = lax.rsqrt(ss_ref[...] + NORM_EPS) * scale
    for j in range(a_ref.shape[2]):
        o = _stage2(g_ref[j], a_ref[0, :, j]) * col
        o_ref[0, j] = o[:n2w]
        o_ref[1, j] = o[n2w:]


def _hy_mid_kernel(a_ref, g_ref, gh_ref, k_ref, o_ref):
    n2w = a_ref.shape[3]
    for j in range(a_ref.shape[2]):
        o = _stage2(g_ref[j], a_ref[0, :, j])
        o_r, o_i = o[:n2w], o[n2w:]
        k_r, k_i = k_ref[0, j], k_ref[1, j]
        y = jnp.concatenate([o_r * k_r - o_i * k_i, o_r * k_i + o_i * k_r], axis=0).astype(BF16)
        z = _dot(gh_ref[j], y)
        o_ref[0, 0, j] = z[:n2w]
        o_ref[0, 1, j] = z[n2w:]


def _hy_out_kernel(c_ref, t_ref, u_ref, x0_ref, bias_ref, o_ref):
    y = _dot(t_ref[...], _rows(c_ref[0])).reshape(o_ref.shape)
    o_ref[...] = x0_ref[...] * (y + u_ref[...] * bias_ref[...])


def _hyena_long(uu, x0, bias, k_raw, k_ss):
    b, l, c = uu.shape
    assert b % 2 == 0
    n = 2 * l
    n2w, n1w = FFT_N2, n // FFT_N2
    half, kb = n1w // 2, HY_K1_STEP
    k1 = jnp.arange(n1w, dtype=I32)
    c1, s1 = _cos_sin(k1[:, None] * k1[None, :], n1w)
    bf = lambda t: t.astype(BF16)
    k2 = jnp.arange(n2w, dtype=I32)
    c2, s2 = _cos_sin((n1w * k2[None, :, None] + k1[:, None, None]) * k2[None, None, :], n)
    g = bf(jnp.concatenate([jnp.concatenate([c2, s2], axis=2), jnp.concatenate([-s2, c2], axis=2)], axis=1))
    c2t, s2t = jnp.swapaxes(c2, 1, 2), jnp.swapaxes(s2, 1, 2)
    gh = bf(jnp.concatenate([jnp.concatenate([c2t, -s2t], axis=2), jnp.concatenate([s2t, c2t], axis=2)], axis=1))

    ak = _dft_stage1([jnp.concatenate([c1, -s1], axis=0)], [k_raw.reshape(1, n1w, n2w, c)], [lambda p: 0], 1)
    kspec = pl.pallas_call(
        functools.partial(_spec_kernel, scale=1.0 / n),
        grid=(n1w // kb,),
        in_specs=[pl.BlockSpec((1, 2, kb, n2w, c), lambda i: (0, 0, i, 0, 0)),
                  pl.BlockSpec((kb, 2 * n2w, 2 * n2w), lambda i: (i, 0, 0)),
                  pl.BlockSpec((1, c), lambda i: (0, 0))],
        out_specs=pl.BlockSpec((2, kb, n2w, c), lambda i: (0, i, 0, 0)),
        out_shape=jax.ShapeDtypeStruct((2, n1w, n2w, c), F32),
        compiler_params=_cparams("parallel"),
    )(ak, g, k_ss)

    ch, sh = c1[:, :half], s1[:, :half]
    view = lambda a: a.reshape(b, half, n2w, c)
    a = _dft_stage1([jnp.concatenate([ch, -sh], axis=0), jnp.concatenate([sh, ch], axis=0)],
                    [view(uu), view(uu)], [lambda p: 2 * p, lambda p: 2 * p + 1], b // 2)
    z = pl.pallas_call(
        _hy_mid_kernel,
        grid=(b // 2, n1w // kb),
        in_specs=[pl.BlockSpec((1, 2, kb, n2w, c), lambda p, i: (p, 0, i, 0, 0)),
                  pl.BlockSpec((kb, 2 * n2w, 2 * n2w), lambda p, i: (i, 0, 0)),
                  pl.BlockSpec((kb, 2 * n2w, 2 * n2w), lambda p, i: (i, 0, 0)),
                  pl.BlockSpec((2, kb, n2w, c), lambda p, i: (0, i, 0, 0))],
        out_specs=pl.BlockSpec((1, 2, kb, n2w, c), lambda p, i: (p, 0, i, 0, 0)),
        out_shape=jax.ShapeDtypeStruct((b // 2, 2, n1w, n2w, c), F32),
        compiler_params=_cparams("parallel", "parallel"),
    )(a, g, gh, kspec)
    cht, sht = ch.T, sh.T
    t_inv = _per_sublane(
        jnp.concatenate([jnp.concatenate([cht, -sht], axis=1), jnp.concatenate([sht, cht], axis=1)], axis=0))
    pair = pl.BlockSpec((2, half, SUBLANES, c), lambda p, j: (p, 0, j, 0))
    out = pl.pallas_call(
        _hy_out_kernel,
        grid=(b // 2, n2w // SUBLANES),
        in_specs=[pl.BlockSpec((1, 2, n1w, SUBLANES, c), lambda p, j: (p, 0, 0, j, 0)),
                  pl.BlockSpec(t_inv.shape, lambda p, j: (0, 0)),
                  pair, pair, pl.BlockSpec((1, c), lambda p, j: (0, 0))],
        out_specs=pair,
        out_shape=jax.ShapeDtypeStruct((b, half, n2w, c), F32),
        compiler_params=_cparams("parallel", "parallel"),
    )(z, t_inv, view(uu), view(x0), bias.reshape(1, c))
    return out.reshape(b, l, c)


def _proj_res_kernel(x_ref, a_ref, b_ref, w_ref, gate_ref, o_ref, *, ka):
    y = _dot(a_ref[0].astype(BF16), w_ref[:ka]) + _dot(b_ref[0].astype(BF16), w_ref[ka:])
    o_ref[0] = x_ref[0] + gate_ref[0] * y


def _proj_residual(x, a, a_blk, b2, b_blk, ka, w, gate, tm=512):
    b, l, d = x.shape
    tm = _tile(l, tm)
    return pl.pallas_call(
        functools.partial(_proj_res_kernel, ka=ka),
        grid=(b, l // tm),
        in_specs=[pl.BlockSpec((1, tm, d), lambda bi, i: (bi, i, 0)),
                  pl.BlockSpec((1, tm, ka), lambda bi, i: (bi, i, a_blk)),
                  pl.BlockSpec((1, tm, ka), lambda bi, i: (bi, i, b_blk)),
                  pl.BlockSpec(w.shape, lambda bi, i: (0, 0)),
                  pl.BlockSpec((1, 1, d), lambda bi, i: (bi, 0, 0))],
        out_specs=pl.BlockSpec((1, tm, d), lambda bi, i: (bi, i, 0)),
        out_shape=jax.ShapeDtypeStruct((b, l, d), F32),
        compiler_params=_cparams("parallel", "parallel"),
    )(x, a, b2, w, gate)


def _ffn_kernel(x_ref, g_ref, sh_ref, sc_ref, gate_ref, wg_ref, wu_ref, wd_ref, o_ref):
    x = x_ref[0]
    h = _norm_mod(x, g_ref[...], sh_ref[0], sc_ref[0]).astype(BF16)
    mid = _silu(_dot(h, wg_ref[...])) * _dot(h, wu_ref[...])
    o_ref[0] = x + gate_ref[0] * _dot(mid.astype(BF16), wd_ref[...])


def _dense_ffn(x, g, shift, scale, gate, wg, wu, wd, tm=512):
    b, l, d = x.shape
    tm = _tile(l, tm)
    vec = pl.BlockSpec((1, 1, d), lambda bi, i: (bi, 0, 0))
    resident = lambda w: pl.BlockSpec(w.shape, lambda bi, i: (0, 0), pipeline_mode=pl.Buffered(1))
    return pl.pallas_call(
        _ffn_kernel,
        grid=(b, l // tm),
        in_specs=[pl.BlockSpec((1, tm, d), lambda bi, i: (bi, i, 0)),
                  pl.BlockSpec((1, d), lambda bi, i: (0, 0)),
                  vec, vec, vec, resident(wg), resident(wu), resident(wd)],
        out_specs=pl.BlockSpec((1, tm, d), lambda bi, i: (bi, i, 0)),
        out_shape=jax.ShapeDtypeStruct((b, l, d), F32),
        compiler_params=_cparams("parallel", "parallel"),
    )(x, g.reshape(1, d), shift, scale, gate, wg, wu, wd)


def _head_norm(t, e, et, g_full):
    ss = _dot((t * t).astype(BF16), e)
    rinv = lax.rsqrt(ss * (1.0 / HEAD_DIM) + NORM_EPS)
    hi, lo = _split(rinv)
    return t * (_dot(hi, et) + _dot(lo, et)) * g_full


def _rope(t, cos, sin_lo, sin_hi):
    w = t.shape[1]
    rep = w // LANES
    tile = lambda a: jnp.concatenate([a] * rep, axis=1)
    return (t * tile(cos) + pltpu.roll(t, w - ROPE_HALF, 1) * tile(sin_lo)
            + pltpu.roll(t, ROPE_HALF, 1) * tile(sin_hi))


def _qkv_kernel(x_ref, g_ref, sh_ref, sc_ref, w_ref, e_ref, et_ref, qg_ref, kg_ref, cos_ref, sl_ref, sh2_ref,
                *out_refs, qw, kw, rope):
    k_ref, v_ref = out_refs[-2:]
    h = _norm_mod(x_ref[0], g_ref[...], sh_ref[0], sc_ref[0])
    t = _dot(h.astype(BF16), w_ref[...])
    e, et = e_ref[...], et_ref[...]
    k = _head_norm(t[:, qw:qw + kw], e[:kw], et[:, :kw], kg_ref[...])
    if rope:
        k = _rope(k, cos_ref[...], sl_ref[...], sh2_ref[...])
    k_ref[0] = k.astype(BF16)
    v_ref[0] = t[:, qw + kw:].astype(BF16)
    if qw:
        q = _head_norm(t[:, :qw], e, et, qg_ref[...])
        q = _rope(q, cos_ref[...], sl_ref[...], sh2_ref[...]) * (HEAD_DIM ** -0.5)
        out_refs[0][0] = q.astype(BF16)


def _head_tables(qw):
    lane = jnp.arange(qw)
    e = (lane[:, None] // HEAD_DIM == jnp.arange(LANES)[None, :]).astype(BF16)
    return e, e.T


def _rope_tables(l):
    rows = l // GRID_W
    row = jnp.repeat(jnp.arange(rows, dtype=F32), GRID_W)
    col = jnp.tile(jnp.arange(GRID_W, dtype=F32), rows)
    inv = ROPE_THETA ** (-jnp.arange(0, AXIS_ROPE_DIM, 2, dtype=F32) / AXIS_ROPE_DIM)
    lane = jnp.arange(LANES)
    in_head = lane % HEAD_DIM
    use_col = (in_head // AXIS_ROPE_DIM) == 1
    hi_half = ((in_head % AXIS_ROPE_DIM) // ROPE_HALF) == 1
    freq = inv[in_head % ROPE_HALF]
    ang = jnp.where(use_col[None, :], col[:, None], row[:, None]) * freq[None, :]
    cos, sin = jnp.cos(ang), jnp.sin(ang)
    return cos, jnp.where(hi_half[None, :], 0.0, -sin), jnp.where(hi_half[None, :], sin, 0.0)


def _qkv_project(x, g, shift, scale, w, q_g, k_g, qw, kw, rope, tm=512):
    b, l, d = x.shape
    tm = _tile(l, tm)
    e, et = _head_tables(max(qw, kw))
    n_q = max(qw, kw) // HEAD_DIM
    qg = jnp.tile(q_g, n_q).reshape(1, -1)
    kg = jnp.tile(k_g, kw // HEAD_DIM).reshape(1, kw)
    if rope:
        cos, s_lo, s_hi = _rope_tables(l)
    else:
        cos = s_lo = s_hi = jnp.zeros((l, LANES), F32)
    vec = pl.BlockSpec((1, 1, d), lambda bi, i: (bi, 0, 0))
    full = lambda a: pl.BlockSpec(a.shape, lambda bi, i: (0,) * a.ndim)
    tab = pl.BlockSpec((tm, LANES), lambda bi, i: (i, 0))
    widths = ([qw] if qw else []) + [kw, kw]
    return pl.pallas_call(
        functools.partial(_qkv_kernel, qw=qw, kw=kw, rope=rope),
        grid=(b, l // tm),
        in_specs=[pl.BlockSpec((1, tm, d), lambda bi, i: (bi, i, 0)),
                  pl.BlockSpec((1, d), lambda bi, i: (0, 0)), vec, vec,
                  full(w), full(e), full(et), full(qg), full(kg), tab, tab, tab],
        out_specs=[pl.BlockSpec((1, tm, n), lambda bi, i: (bi, i, 0)) for n in widths],
        out_shape=[jax.ShapeDtypeStruct((b, l, n), BF16) for n in widths],
        compiler_params=_cparams("parallel", "parallel"),
    )(x, g.reshape(1, d), shift, scale, w, e, et, qg, kg, cos, s_lo, s_hi)


def _attn_kernel(q_ref, kp_ref, kc_ref, kn_ref, vp_ref, vc_ref, vn_ref, kx_ref, vx_ref, sink_ref, o_ref, *,
                 seq, group):
    qb = pl.program_id(1)
    bq = q_ref.shape[1]
    lc = kx_ref.shape[1]
    nkeys = 3 * bq + lc
    rows = lax.broadcasted_iota(I32, (group * bq, nkeys), 0)
    cols = lax.broadcasted_iota(I32, (group * bq, nkeys), 1)
    qpos = qb * bq + rows % bq
    kpos = (qb - 1) * bq + cols
    valid = jnp.logical_or(
        cols >= 3 * bq,
        jnp.logical_and(jnp.logical_and(kpos >= 0, kpos < seq), jnp.abs(qpos - kpos) <= WINDOW))
    for h in range(N_KV_HEADS):
        ks = slice(h * HEAD_DIM, (h + 1) * HEAD_DIM)
        kh = jnp.concatenate([kp_ref[0, :, ks], kc_ref[0, :, ks], kn_ref[0, :, ks], kx_ref[0, :, ks]], axis=0)
        vh = jnp.concatenate([vp_ref[0, :, ks], vc_ref[0, :, ks], vn_ref[0, :, ks], vx_ref[0, :, ks]], axis=0)
        qh = jnp.concatenate(
            [q_ref[0, :, (h * group + g) * HEAD_DIM:(h * group + g + 1) * HEAD_DIM] for g in range(group)], axis=0)
        s = lax.dot_general(qh, kh, (((1,), (1,)), ((), ())), preferred_element_type=F32)
        s = jnp.where(valid, s, NEG_INF)
        sk = sink_ref[h]
        m = jnp.maximum(jnp.max(s, axis=1, keepdims=True), sk)
        p = jnp.exp(s - m)
        den = jnp.sum(p, axis=1, keepdims=True) + jnp.exp(sk - m)
        o = _dot(p.astype(BF16), vh) / den
        for g in range(group):
            hq = h * group + g
            o_ref[0, :, hq * HEAD_DIM:(hq + 1) * HEAD_DIM] = o[g * bq:(g + 1) * bq].astype(o_ref.dtype)


def _window_attention(q, k, v, kx, vx, sink):
    b, l, qw = q.shape
    kw = k.shape[2]
    lc = kx.shape[1]
    bq = BLOCK_Q
    nb = l // bq
    group = qw // kw
    sink_tab = jnp.repeat(sink.astype(F32).reshape(N_KV_HEADS, group), bq, axis=1)[..., None]
    kv_prev = pl.BlockSpec((1, bq, kw), lambda bi, i: (bi, jnp.maximum(i - 1, 0), 0))
    kv_cur = pl.BlockSpec((1, bq, kw), lambda bi, i: (bi, i, 0))
    kv_next = pl.BlockSpec((1, bq, kw), lambda bi, i: (bi, jnp.minimum(i + 1, nb - 1), 0))
    kv_ctx = pl.BlockSpec((1, lc, kw), lambda bi, i: (bi, 0, 0))
    return pl.pallas_call(
        functools.partial(_attn_kernel, seq=l, group=group),
        grid=(b, nb),
        in_specs=[pl.BlockSpec((1, bq, qw), lambda bi, i: (bi, i, 0)),
                  kv_prev, kv_cur, kv_next, kv_prev, kv_cur, kv_next, kv_ctx, kv_ctx,
                  pl.BlockSpec(sink_tab.shape, lambda bi, i: (0, 0, 0))],
        out_specs=pl.BlockSpec((1, bq, qw), lambda bi, i: (bi, i, 0)),
        out_shape=jax.ShapeDtypeStruct((b, l, qw), BF16),
        compiler_params=_cparams("parallel", "parallel"),
    )(q, k, k, k, v, v, v, kx, vx, sink_tab)


def _router_kernel(x_ref, g_ref, sh_ref, sc_ref, wh_ref, wl_ref, tri_ref, h_ref, gate_ref, rank_ref, rankt_ref,
                   cnt_ref):
    h = _norm_mod(x_ref[0], g_ref[...], sh_ref[0], sc_ref[0])
    h_ref[0] = h.astype(BF16)
    hi, lo = _split(h)
    logits = _dot(hi, wh_ref[...]) + _dot(hi, wl_ref[...]) + _dot(lo, wh_ref[...])
    lane = lax.broadcasted_iota(I32, logits.shape, 1)
    logits = jnp.where(lane < N_EXPERTS, logits, -jnp.inf)
    m1 = jnp.max(logits, axis=1, keepdims=True)
    i1 = jnp.min(jnp.where(logits == m1, lane, LANES), axis=1, keepdims=True)
    rest = jnp.where(lane == i1, -jnp.inf, logits)
    m2 = jnp.max(rest, axis=1, keepdims=True)
    i2 = jnp.min(jnp.where(rest == m2, lane, LANES), axis=1, keepdims=True)
    e = jnp.exp(m2 - m1)
    g1 = 1.0 / (1.0 + e)
    g2 = e / (1.0 + e)
    pick1, pick2 = lane == i1, lane == i2
    member = jnp.logical_or(pick1, pick2)
    gate_ref[0] = jnp.where(pick1, g1, jnp.where(pick2, g2, 0.0))
    m = jnp.where(member, 1.0, 0.0)
    rank = jnp.where(member, _dot(tri_ref[...], m.astype(BF16)), -1.0)
    rank_ref[0] = rank.astype(I32)
    rankt_ref[0] = jnp.transpose(rank)[:SUBLANES].astype(I32)
    cnt_ref[0] = jnp.sum(m, axis=0, keepdims=True).astype(I32)


def _route(x, g, shift, scale, w_router):
    b, l, d = x.shape
    tm = _tile(l, MOE_TILE)
    nt = l // tm
    wr = jnp.concatenate([w_router, jnp.zeros((d, LANES - w_router.shape[1]), F32)], axis=1)
    wh, wl = _split(wr)
    tri = jnp.tril(jnp.ones((tm, tm), BF16), -1)
    vec = pl.BlockSpec((1, 1, d), lambda bi, i: (bi, 0, 0))
    tok = lambda n: pl.BlockSpec((1, tm, n), lambda bi, i: (bi, i, 0))
    return pl.pallas_call(
        _router_kernel,
        grid=(b, nt),
        in_specs=[tok(d), pl.BlockSpec((1, d), lambda bi, i: (0, 0)), vec, vec,
                  pl.BlockSpec((d, LANES), lambda bi, i: (0, 0)),
                  pl.BlockSpec((d, LANES), lambda bi, i: (0, 0)),
                  pl.BlockSpec((tm, tm), lambda bi, i: (0, 0))],
        out_specs=[tok(d), tok(LANES), tok(LANES),
                   pl.BlockSpec((1, SUBLANES, tm), lambda bi, i: (bi * nt + i, 0, 0)),
                   pl.BlockSpec((1, 1, LANES), lambda bi, i: (bi * nt + i, 0, 0))],
        out_shape=[jax.ShapeDtypeStruct((b, l, d), BF16),
                   jax.ShapeDtypeStruct((b, l, LANES), F32),
                   jax.ShapeDtypeStruct((b, l, LANES), I32),
                   jax.ShapeDtypeStruct((b * nt, SUBLANES, tm), I32),
                   jax.ShapeDtypeStruct((b * nt, 1, LANES), I32)],
        compiler_params=_cparams("parallel", "parallel"),
    )(x, g.reshape(1, d), shift, scale, wh, wl, tri)


def _slot_layout(cnt, rows):
    nt = cnt.shape[0]
    seg = (cnt + SUBLANES - 1) // SUBLANES * SUBLANES
    padded = (jnp.sum(seg, axis=0) + rows - 1) // rows * rows
    stride = padded + rows
    pstart = jnp.cumsum(stride) - stride
    off = pstart[None, :] + jnp.cumsum(seg, axis=0) - seg
    n_slots = (nt * N_EXPERTS * (SUBLANES - 1) + nt * MOE_TILE * TOP_K + rows - 1) // rows * rows \
        + 2 * N_EXPERTS * rows
    nblk = n_slots // rows
    bstart = jnp.arange(nblk, dtype=I32) * rows
    be = jnp.minimum(jnp.sum((bstart[:, None] >= (pstart + stride)[None, :]).astype(I32), axis=1), N_EXPERTS - 1)
    onehot = (be[:, None] == jnp.arange(N_EXPERTS, dtype=I32)[None, :]).astype(I32)
    lo = jnp.sum(onehot * pstart[None, :], axis=1)
    hi = jnp.sum(onehot * (pstart + padded)[None, :], axis=1)
    used = jnp.logical_and(bstart >= lo, bstart < hi)
    zero_blk = jnp.logical_or(jnp.logical_not(used), bstart == hi - rows)
    as_i32 = lambda a: a.reshape(-1).astype(I32)
    return n_slots, as_i32(off), as_i32(cnt), as_i32(zero_blk), as_i32(be), as_i32(used)


def _seg_copy(src_ref, dst_ref, row, sem):
    n = src_ref.shape[0]
    return pltpu.make_async_copy(src_ref, dst_ref.at[pl.ds(pl.multiple_of(row, SUBLANES), n)], sem)


def _chunk_copy(src_ref, dst_ref, row, sem):
    n = dst_ref.shape[0]
    return pltpu.make_async_copy(src_ref.at[pl.ds(pl.multiple_of(row, SUBLANES), n)], dst_ref, sem)


def _dispatch_kernel(off_ref, cnt_ref, zero_ref, h_ref, rt_ref, xs_ref, xbuf, xbuf2, sem, sem2, *, rows):
    i = pl.program_id(0)
    cap = xbuf.shape[1]
    tm = h_ref.shape[1]

    @pl.when(i == 0)
    def _():
        xbuf[0] = jnp.zeros(xbuf.shape[1:], F32)

        def zero_block(j, carry):
            @pl.when(zero_ref[j] > 0)
            def _():
                for part in range(rows // cap):
                    c = _seg_copy(xbuf.at[0], xs_ref, j * rows + part * cap, sem2)
                    c.start()
                    c.wait()
            return carry

        lax.fori_loop(0, zero_ref.shape[0], zero_block, 0)

    h = h_ref[0]
    rt = rt_ref[0]
    riota = lax.broadcasted_iota(I32, (cap, tm), 0)

    def copy(e):
        return _seg_copy(xbuf.at[e], xs_ref, off_ref[i * N_EXPERTS + e], sem.at[e])

    for e in range(N_EXPERTS):
        sel = rt[e:e + 1, :]

        def segment(base):
            return _dot(jnp.where(sel == riota + base, 1.0, 0.0).astype(BF16), h)

        xbuf[e] = segment(0)
        copy(e).start()

        @pl.when(cnt_ref[i * N_EXPERTS + e] > cap)
        def _():
            xbuf2[...] = segment(cap)
            c = _seg_copy(xbuf2, xs_ref, off_ref[i * N_EXPERTS + e] + cap, sem2)
            c.start()
            c.wait()

    for e in range(N_EXPERTS):
        copy(e).wait()


def _dispatch(h, rankt, n_slots, off, cnt, zero_blk, rows):
    b, l, d = h.shape
    tm = _tile(l, MOE_TILE)
    nt = l // tm
    cap = MOE_CAP
    assert tm <= 2 * cap and rows % cap == 0
    grid_spec = pltpu.PrefetchScalarGridSpec(
        num_scalar_prefetch=3,
        grid=(b * nt,),
        in_specs=[pl.BlockSpec((1, tm, d), lambda i, *_: (i // nt, i % nt, 0)),
                  pl.BlockSpec((1, SUBLANES, tm), lambda i, *_: (i, 0, 0))],
        out_specs=pl.BlockSpec(memory_space=pl.ANY),
        scratch_shapes=[pltpu.VMEM((N_EXPERTS, cap, d), F32), pltpu.VMEM((cap, d), F32),
                        pltpu.SemaphoreType.DMA((N_EXPERTS,)), pltpu.SemaphoreType.DMA(())],
    )
    return pl.pallas_call(
        functools.partial(_dispatch_kernel, rows=rows),
        grid_spec=grid_spec,
        out_shape=jax.ShapeDtypeStruct((n_slots, d), F32),
        compiler_params=_cparams("arbitrary"),
    )(off, cnt, zero_blk, h, rankt)


def _moe_kernel(be_ref, used_ref, x_ref, wg_ref, wu_ref, wd_ref, o_ref):
    i = pl.program_id(0)

    @pl.when(used_ref[i] > 0)
    def _():
        x = x_ref[...].astype(BF16)
        mid = _silu(_dot(x, wg_ref[0])) * _dot(x, wu_ref[0])
        o_ref[...] = _dot(mid.astype(BF16), wd_ref[0])

    @pl.when(used_ref[i] == 0)
    def _():
        o_ref[...] = jnp.zeros_like(o_ref)


def _expert_ffn(xs, block_e, used, wg, wu, wd, rows):
    s, d = xs.shape
    nblk = s // rows
    expert = lambda w: pl.BlockSpec((1,) + w.shape[1:], lambda i, be, us: (be[i], 0, 0),
                                    pipeline_mode=pl.Buffered(1))
    grid_spec = pltpu.PrefetchScalarGridSpec(
        num_scalar_prefetch=2,
        grid=(nblk,),
        in_specs=[pl.BlockSpec((rows, d), lambda i, be, us: (i, 0)), expert(wg), expert(wu), expert(wd)],
        out_specs=pl.BlockSpec((rows, d), lambda i, be, us: (i, 0)),
    )
    return pl.pallas_call(
        _moe_kernel,
        grid_spec=grid_spec,
        out_shape=jax.ShapeDtypeStruct((s, d), F32),
        compiler_params=_cparams("arbitrary"),
    )(block_e, used, xs, wg, wu, wd)


def _combine_kernel(off_ref, cnt_ref, x_ref, gate_ref, rg_ref, rank_ref, ys_ref, o_ref, buf, buf2, acc_ref, sem,
                    sem2, *, n_tiles):
    i = pl.program_id(0)
    slot = i % 2
    cap = buf.shape[2]
    tm = x_ref.shape[1]

    def chunk(t, s, e):
        return _chunk_copy(ys_ref, buf.at[s, e], off_ref[t * N_EXPERTS + e], sem.at[s, e])

    @pl.when(i == 0)
    def _():
        for e in range(N_EXPERTS):
            chunk(0, 0, e).start()

    @pl.when(i + 1 < n_tiles)
    def _():
        for e in range(N_EXPERTS):
            chunk(i + 1, 1 - slot, e).start()

    rank = rank_ref[0]
    rg = rg_ref[0]
    liota = lax.broadcasted_iota(I32, (tm, cap), 1)

    def picked(e, base, rows_ref):
        q = jnp.where(rank[:, e:e + 1] == liota + base, 1.0, 0.0).astype(BF16)
        return rg[:, e:e + 1] * _dot(q, rows_ref[...].astype(BF16))

    y = jnp.zeros(acc_ref.shape, F32)
    for e in range(N_EXPERTS):
        chunk(i, slot, e).wait()
        y = y + picked(e, 0, buf.at[slot, e])
    acc_ref[...] = y

    for e in range(N_EXPERTS):
        @pl.when(cnt_ref[i * N_EXPERTS + e] > cap)
        def _():
            c = _chunk_copy(ys_ref, buf2, off_ref[i * N_EXPERTS + e] + cap, sem2)
            c.start()
            c.wait()
            acc_ref[...] += picked(e, cap, buf2)

    o_ref[0] = x_ref[0] + gate_ref[0] * acc_ref[...]


def _moe_combine(x, gate, route_gates, rank, ys, off, cnt):
    b, l, d = x.shape
    tm = _tile(l, MOE_TILE)
    nt = l // tm
    cap = MOE_CAP
    tok = lambda n: pl.BlockSpec((1, tm, n), lambda i, *_: (i // nt, i % nt, 0))
    grid_spec = pltpu.PrefetchScalarGridSpec(
        num_scalar_prefetch=2,
        grid=(b * nt,),
        in_specs=[tok(d), pl.BlockSpec((1, 1, d), lambda i, *_: (i // nt, 0, 0)), tok(LANES), tok(LANES),
                  pl.BlockSpec(memory_space=pl.ANY)],
        out_specs=tok(d),
        scratch_shapes=[pltpu.VMEM((2, N_EXPERTS, cap, d), F32), pltpu.VMEM((cap, d), F32),
                        pltpu.VMEM((tm, d), F32),
                        pltpu.SemaphoreType.DMA((2, N_EXPERTS)), pltpu.SemaphoreType.DMA(())],
    )
    return pl.pallas_call(
        functools.partial(_combine_kernel, n_tiles=b * nt),
        grid_spec=grid_spec,
        out_shape=jax.ShapeDtypeStruct((b, l, d), F32),
        compiler_params=_cparams("arbitrary"),
    )(off, cnt, x, gate, route_gates, rank, ys)


def _angles(rows, cols, n):
    m = (rows[:, None] * cols[None, :]) % n
    return (2.0 * math.pi / n) * m.astype(F32)


def _fnet_table(l):
    idx = jnp.arange(l, dtype=I32)
    ang = _angles(idx, idx, l)
    s = 1.0 / math.sqrt(l)
    return jnp.concatenate([jnp.cos(ang) * s, jnp.sin(ang) * (-s)], axis=1).astype(BF16)


def _rfft_table(n):
    half = n // 2
    f = jnp.arange(half, dtype=I32)
    t = jnp.arange(n, dtype=I32)
    ang = _angles(f, t, n)
    top = jnp.cos(ang)
    bot = -jnp.sin(ang)
    nyq = jnp.cos(_angles(jnp.full((1,), half, I32), t, n))
    bot = jnp.concatenate([nyq, bot[1:]], axis=0)
    return jnp.concatenate([top, bot], axis=0).astype(BF16)


def _group_dft_table(width):
    gd = FNET_GROUP_DIM
    idx = jnp.arange(gd, dtype=I32)
    ang = _angles(idx, idx, gd)
    s = 1.0 / math.sqrt(gd)
    eye = jnp.eye(width // gd, dtype=F32)
    return jnp.concatenate([jnp.kron(eye, jnp.cos(ang) * s), jnp.kron(eye, jnp.sin(ang) * s)], axis=1)


def _even_mixer(x, g1, shift, scale, gate, w_in_f, w_out, conv_w, conv_b, hy_bias, filt, fw, hw):
    b, l, d = x.shape
    n = 2 * l
    long_seq = l >= LONG_SEQ
    seq_dtype = F32 if long_seq else BF16
    u, y_cos, y_sin = _norm_mod_matmul(x, g1, shift, scale, w_in_f, [3 * hw, fw, fw], [BF16, seq_dtype, seq_dtype])
    uu, x0 = _hyena_pre(u, conv_w, conv_b, hw, seq_dtype)
    k_raw, k_ss = filt(l)
    if long_seq:
        a = _fnet_long(y_cos, y_sin)
        hy = _hyena_long(uu, x0, hy_bias, k_raw, k_ss)
    else:
        y_both = jnp.concatenate([y_cos, y_sin], axis=2)
        a = _left_dft(_fnet_table(l), y_both, lambda bi, k, tk: (bi, k % (l // tk), k // (l // tk)), n, fw, BF16,
                      tk=min(l, 1024))
        wf = _rfft_table(n)
        wts = jnp.concatenate([jnp.ones((1,), F32), jnp.full((l - 1,), 2.0, F32)]) / n
        row_scale = jnp.concatenate([wts, wts]).reshape(n, 1)
        kspec = _left_dft(wf, k_raw[None], lambda bi, k, tk: (0, k, 0), n, hw, F32,
                          row_scale=row_scale, col_sumsq=k_ss)
        uhat = _left_dft(wf, uu, lambda bi, k, tk: (bi, k, 0), l, hw, F32)
        hy = _hyena_inverse(wf[:, :l].T, uhat, kspec, uu, x0, hy_bias)
    return _proj_residual(x, a, 0, hy, 0, fw, w_out, gate)


def kernel(x, c, ctx, c_ctx, ada_w, ada_b, norm1_g, norm2_g, ev_w_in, ev_w_out, hy_conv_w, hy_conv_b, hy_bias,
           hf_w0, hf_b0, hf_w1, hf_b1, hf_w2, hf_b2, hf_w3, hf_freq, ffn_w_gate, ffn_w_up, ffn_w_down, od_w_qkv,
           od_w_out, q_norm_g, k_norm_g, attn_sink, moe_router, moe_w_gate, moe_w_up, moe_w_down):
    b, l, d = x.shape
    lc = ctx.shape[1]
    assert ada_w.shape[0] == 2, "this implementation covers the two-layer (even, odd) stack"
    hw = hy_bias.shape[1]
    fw = ev_w_in.shape[2] - 3 * hw
    qw = od_w_out.shape[1]
    kw = (od_w_qkv.shape[2] - qw) // 2

    rows = (b + 1 + 7) // 8 * 8
    cc = jnp.concatenate([c, c_ctx[None, :], jnp.zeros((rows - b - 1, d), F32)], axis=0)
    mod = _ada_vectors(cc, ada_w, ada_b)
    ml = [[mod[i, :b, None, m * d:(m + 1) * d] for m in range(N_MOD)] for i in range(2)]
    mc = [[mod[i, b:b + 1, None, m * d:(m + 1) * d] for m in range(N_MOD)] for i in range(2)]

    w_in = ev_w_in[0]
    w_fnet = _matmul3(w_in[:, :fw], _group_dft_table(fw))
    w_in_f = jnp.concatenate([w_in[:, fw:], w_fnet], axis=1).astype(BF16)
    w_out0 = ev_w_out[0].astype(BF16)
    filt = lambda seq: _hyena_filter(seq, hw, hf_w0[0], hf_b0[0], hf_w1[0], hf_b1[0], hf_w2[0], hf_b2[0],
                                     hf_w3[0], hf_freq[0])
    wg, wu, wd = ffn_w_gate[0].astype(BF16), ffn_w_up[0].astype(BF16), ffn_w_down[0].astype(BF16)

    x = _even_mixer(x, norm1_g[0], ml[0][0], ml[0][1], ml[0][2], w_in_f, w_out0, hy_conv_w[0], hy_conv_b[0],
                    hy_bias[0], filt, fw, hw)
    x = _dense_ffn(x, norm2_g[0], ml[0][3], ml[0][4], ml[0][5], wg, wu, wd)

    bc = lambda v: jnp.broadcast_to(v, (b, 1, d))
    ctx = _even_mixer(ctx, norm1_g[0], bc(mc[0][0]), bc(mc[0][1]), bc(mc[0][2]), w_in_f, w_out0, hy_conv_w[0],
                      hy_conv_b[0], hy_bias[0], filt, fw, hw)
    ctx = _dense_ffn(ctx.reshape(1, b * lc, d), norm2_g[0], mc[0][3], mc[0][4], mc[0][5], wg, wu, wd)

    w_qkv = od_w_qkv[0].astype(BF16)
    q, k, v = _qkv_project(x, norm1_g[1], ml[1][0], ml[1][1], w_qkv, q_norm_g[0], k_norm_g[0], qw, kw, True)
    kx, vx = _qkv_project(ctx, norm1_g[1], mc[1][0], mc[1][1], w_qkv[:, qw:], q_norm_g[0], k_norm_g[0], 0, kw,
                          False)
    o = _window_attention(q, k, v, kx.reshape(b, lc, kw), vx.reshape(b, lc, kw), attn_sink[0])
    half = qw // 2
    x = _proj_residual(x, o, 0, o, 1, half, od_w_out[0].astype(BF16), ml[1][2])

    h2, gates, rank, rankt, cnt = _route(x, norm2_g[1], ml[1][3], ml[1][4], moe_router[0])
    n_slots, off, cnt, zero_blk, block_e, used = _slot_layout(cnt[:, 0, :N_EXPERTS], MOE_ROWS)
    xs = _dispatch(h2, rankt, n_slots, off, cnt, zero_blk, MOE_ROWS)
    ys = _expert_ffn(xs, block_e, used, moe_w_gate[0].astype(BF16), moe_w_up[0].astype(BF16),
                     moe_w_down[0].astype(BF16), MOE_ROWS)
    return _moe_combine(x, ml[1][5], gates, rank, ys, off, cnt)
```

```python
import functools
import math

import jax
import jax.numpy as jnp
from jax import lax
from jax.experimental import pallas as pl
from jax.experimental.pallas import tpu as pltpu

F32 = jnp.float32
BF16 = jnp.bfloat16
I32 = jnp.int32

NORM_EPS = 1e-6
NEG_INF = -1e30
N_MOD = 6

FNET_GROUP_DIM = 128
HY_EMB_BANDS = 16
HY_FILTER_HIDDEN = 64
HY_DECAY_TARGET = 1e-2
HY_FAST_DECAY = 0.3
HY_SLOW_DECAY = 1.5

HEAD_DIM = 64
N_KV_HEADS = 4
GRID_W = 64
WINDOW = 128
BLOCK_Q = 128
ROPE_THETA = 10000.0
AXIS_ROPE_DIM = HEAD_DIM // 2
ROPE_HALF = AXIS_ROPE_DIM // 2
N_EXPERTS = 8
TOP_K = 2

LANES = 128
SUBLANES = 8
VMEM_LIMIT = 56 * 1024 * 1024
MOE_TILE = 512
MOE_ROWS = 512
MOE_CAP = 256


def _cparams(*sem):
    return pltpu.CompilerParams(dimension_semantics=sem, vmem_limit_bytes=VMEM_LIMIT)


def _split(a):
    hi = a.astype(BF16)
    lo = (a - hi.astype(F32)).astype(BF16)
    return hi, lo


def _dot(a, b):
    return jnp.dot(a, b, preferred_element_type=F32)


def _dot3(a, b):
    ah, al = _split(a)
    bh, bl = _split(b)
    return _dot(ah, bh) + _dot(ah, bl) + _dot(al, bh)


def _silu(t):
    return t / (1.0 + jnp.exp(-t))


def _norm_mod(x, g, shift, scale):
    ms = jnp.mean(x * x, axis=-1, keepdims=True)
    y = x * lax.rsqrt(ms + NORM_EPS) * g
    return y * (1.0 + scale) + shift


def _tile(n, pref):
    t = min(n, pref)
    assert n % t == 0, (n, pref)
    return t


def _ada_kernel(c_ref, w_ref, b_ref, o_ref):
    o_ref[0] = _dot3(_silu(c_ref[...]), w_ref[0]) + b_ref[0]


def _ada_vectors(cc, ada_w, ada_b):
    depth, d, n = ada_w.shape
    rows = cc.shape[0]
    tn = _tile(n, 1536)
    return pl.pallas_call(
        _ada_kernel,
        grid=(depth, n // tn),
        in_specs=[pl.BlockSpec((rows, d), lambda l, j: (0, 0)),
                  pl.BlockSpec((1, d, tn), lambda l, j: (l, 0, j)),
                  pl.BlockSpec((1, 1, tn), lambda l, j: (l, 0, j))],
        out_specs=pl.BlockSpec((1, rows, tn), lambda l, j: (l, 0, j)),
        out_shape=jax.ShapeDtypeStruct((depth, rows, n), F32),
        compiler_params=_cparams("arbitrary", "arbitrary"),
    )(cc, ada_w, ada_b.reshape(depth, 1, n))


def _mm3_kernel(a_ref, b_ref, o_ref):
    o_ref[...] = _dot3(a_ref[...], b_ref[...])


def _matmul3(a, b):
    m, _ = a.shape
    n = b.shape[1]
    return pl.pallas_call(_mm3_kernel, out_shape=jax.ShapeDtypeStruct((m, n), F32),
                          compiler_params=_cparams())(a, b)


def _nmm_kernel(x_ref, g_ref, sh_ref, sc_ref, w_ref, *o_refs):
    h = _norm_mod(x_ref[0], g_ref[...], sh_ref[0], sc_ref[0])
    y = _dot(h.astype(BF16), w_ref[...])
    col = 0
    for o_ref in o_refs:
        n = o_ref.shape[2]
        o_ref[0] = y[:, col:col + n].astype(o_ref.dtype)
        col += n


def _norm_mod_matmul(x, g, shift, scale, w, widths, out_dtypes, tm=512):
    b, l, d = x.shape
    n = w.shape[1]
    assert sum(widths) == n
    tm = _tile(l, tm)
    return pl.pallas_call(
        _nmm_kernel,
        grid=(b, l // tm),
        in_specs=[pl.BlockSpec((1, tm, d), lambda bi, i: (bi, i, 0)),
                  pl.BlockSpec((1, d), lambda bi, i: (0, 0)),
                  pl.BlockSpec((1, 1, d), lambda bi, i: (bi, 0, 0)),
                  pl.BlockSpec((1, 1, d), lambda bi, i: (bi, 0, 0)),
                  pl.BlockSpec((d, n), lambda bi, i: (0, 0))],
        out_specs=[pl.BlockSpec((1, tm, wd), lambda bi, i: (bi, i, 0)) for wd in widths],
        out_shape=[jax.ShapeDtypeStruct((b, l, wd), dt) for wd, dt in zip(widths, out_dtypes)],
        compiler_params=_cparams("parallel", "parallel"),
    )(x, g.reshape(1, d), shift, scale, w)


def _ldft_kernel(w_ref, r_ref, o_ref, acc_ref, *, nk):
    k = pl.program_id(2)

    @pl.when(k == 0)
    def _():
        acc_ref[...] = jnp.zeros_like(acc_ref)

    acc_ref[...] += _dot(w_ref[...], r_ref[0].astype(BF16))

    @pl.when(k == nk - 1)
    def _():
        o_ref[0] = acc_ref[...].astype(o_ref.dtype)


def _ldft_scaled_kernel(w_ref, r_ref, rs_ref, ss_ref, o_ref, acc_ref, *, nk):
    k = pl.program_id(2)

    @pl.when(k == 0)
    def _():
        acc_ref[...] = jnp.zeros_like(acc_ref)

    acc_ref[...] += _dot(w_ref[...], r_ref[0].astype(BF16))

    @pl.when(k == nk - 1)
    def _():
        o_ref[0] = acc_ref[...] * rs_ref[...] * lax.rsqrt(ss_ref[...] + NORM_EPS)


def _left_dft(w, rhs, rhs_map, kdim, c, out_dtype, tm=2048, tk=1024, row_scale=None, col_sumsq=None):
    m = w.shape[0]
    nb = rhs.shape[0]
    tm = _tile(m, tm)
    tk = _tile(kdim, tk)
    nk = kdim // tk
    in_specs = [pl.BlockSpec((tm, tk), lambda b, i, k: (i, k)),
                pl.BlockSpec((1, tk, c), lambda b, i, k: rhs_map(b, k, tk))]
    args = [w, rhs]
    if row_scale is None:
        body = functools.partial(_ldft_kernel, nk=nk)
    else:
        body = functools.partial(_ldft_scaled_kernel, nk=nk)
        in_specs += [pl.BlockSpec((tm, 1), lambda b, i, k: (i, 0)),
                     pl.BlockSpec((1, c), lambda b, i, k: (0, 0))]
        args += [row_scale, col_sumsq]
    return pl.pallas_call(
        body,
        grid=(nb, m // tm, nk),
        in_specs=in_specs,
        out_specs=pl.BlockSpec((1, tm, c), lambda b, i, k: (b, i, 0)),
        out_shape=jax.ShapeDtypeStruct((nb, m, c), out_dtype),
        scratch_shapes=[pltpu.VMEM((tm, c), F32)],
        compiler_params=_cparams("parallel", "parallel", "arbitrary"),
    )(*args)


def _hy_pre_kernel(u_ref, up_ref, un_ref, cw_ref, cb_ref, uo_ref, x0_ref, *, nl, tl, hw):
    i = pl.program_id(1)
    u = u_ref[0].astype(F32)
    halo = up_ref.shape[1]
    prev = up_ref[0].astype(F32)[halo - 1:halo]
    nxt = un_ref[0].astype(F32)[0:1]
    prev = jnp.where(i == 0, 0.0, prev)
    nxt = jnp.where(i == nl - 1, 0.0, nxt)
    rows = lax.broadcasted_iota(I32, u.shape, 0)
    um = jnp.where(rows == 0, prev, pltpu.roll(u, 1, 0))
    up = jnp.where(rows == tl - 1, nxt, pltpu.roll(u, tl - 1, 0))
    cw = cw_ref[...]
    z = um * cw[0:1] + u * cw[1:2] + up * cw[2:3] + cb_ref[...]
    uo_ref[0] = (z[:, :hw] * z[:, hw:2 * hw]).astype(uo_ref.dtype)
    x0_ref[0] = z[:, 2 * hw:].astype(x0_ref.dtype)


def _hyena_pre(u, conv_w, conv_b, hw, out_dtype, tl=512):
    b, l, _ = u.shape
    tl = _tile(l, tl)
    nl = l // tl
    halo = 16
    hb = tl // halo
    nh = l // halo
    w3 = 3 * hw
    return pl.pallas_call(
        functools.partial(_hy_pre_kernel, nl=nl, tl=tl, hw=hw),
        grid=(b, nl),
        in_specs=[pl.BlockSpec((1, tl, w3), lambda bi, i: (bi, i, 0)),
                  pl.BlockSpec((1, halo, w3), lambda bi, i: (bi, jnp.maximum(i * hb - 1, 0), 0)),
                  pl.BlockSpec((1, halo, w3), lambda bi, i: (bi, jnp.minimum((i + 1) * hb, nh - 1), 0)),
                  pl.BlockSpec((3, w3), lambda bi, i: (0, 0)),
                  pl.BlockSpec((1, w3), lambda bi, i: (0, 0))],
        out_specs=[pl.BlockSpec((1, tl, hw), lambda bi, i: (bi, i, 0)),
                   pl.BlockSpec((1, tl, hw), lambda bi, i: (bi, i, 0))],
        out_shape=[jax.ShapeDtypeStruct((b, l, hw), out_dtype), jax.ShapeDtypeStruct((b, l, hw), out_dtype)],
        compiler_params=_cparams("parallel", "parallel"),
    )(u, u, u, conv_w, conv_b.reshape(1, w3))


def _hy_filter_kernel(z_ref, aux_ref, w0_ref, b0_ref, w1_ref, b1_ref, w2_ref, b2_ref, w3_ref, fr_ref, dl_ref,
                      k_ref, ss_ref, *, hw):
    i = pl.program_id(0)
    fr = fr_ref[...]
    h = jnp.sin(fr * (_dot3(z_ref[...], w0_ref[...]) + b0_ref[...]))
    h = jnp.sin(fr * (_dot3(h, w1_ref[...]) + b1_ref[...]))
    h = jnp.sin(fr * (_dot3(h, w2_ref[...]) + b2_ref[...]))
    h = _dot3(h, w3_ref[...])
    aux = aux_ref[...]
    t, m_fwd, m_bwd = aux[:, 0:1], aux[:, 1:2], aux[:, 2:3]
    k = (h[:, :hw] * m_fwd + h[:, hw:] * m_bwd) * jnp.exp(-t * dl_ref[...])
    k_ref[...] = k

    @pl.when(i == 0)
    def _():
        ss_ref[...] = jnp.zeros_like(ss_ref)

    ss_ref[...] += jnp.sum(k * k, axis=0, keepdims=True)


def _hyena_filter(l, hw, fw0, fb0, fw1, fb1, fw2, fb2, fw3, freq):
    n = 2 * l
    hid = HY_FILTER_HIDDEN
    r = jnp.arange(n)
    pos = jnp.where(r < l, r, n - r).astype(F32)
    t = pos / max(l - 1, 1)
    wv = (2.0 * math.pi / l) * pos
    bands = jnp.linspace(1e-4, HY_EMB_BANDS - 1, HY_EMB_BANDS, dtype=F32)
    ang = wv[:, None] * bands[None, :]
    ztab = jnp.concatenate([t[:, None], jnp.cos(ang), -jnp.sin(ang),
                            jnp.zeros((n, hid - 1 - 2 * HY_EMB_BANDS), F32)], axis=-1)
    aux = jnp.zeros((n, LANES), F32)
    aux = aux.at[:, 0].set(t).at[:, 1].set((r < l).astype(F32)).at[:, 2].set((r > l).astype(F32))
    w0 = jnp.concatenate([fw0, jnp.zeros((hid - fw0.shape[0], hid), F32)], axis=0)
    deltas = jnp.abs(jnp.linspace(math.log(HY_DECAY_TARGET) / HY_SLOW_DECAY,
                                  math.log(HY_DECAY_TARGET) / HY_FAST_DECAY, hw, dtype=F32)).reshape(1, hw)
    tr = _tile(n, 1024)
    full = lambda shape: pl.BlockSpec(shape, lambda i: (0, 0))
    return pl.pallas_call(
        functools.partial(_hy_filter_kernel, hw=hw),
        grid=(n // tr,),
        in_specs=[pl.BlockSpec((tr, hid), lambda i: (i, 0)),
                  pl.BlockSpec((tr, LANES), lambda i: (i, 0)),
                  full((hid, hid)), full((1, hid)), full((hid, hid)), full((1, hid)),
                  full((hid, hid)), full((1, hid)), full((hid, 2 * hw)), full((1, hid)), full((1, hw))],
        out_specs=[pl.BlockSpec((tr, hw), lambda i: (i, 0)), full((1, hw))],
        out_shape=[jax.ShapeDtypeStruct((n, hw), F32), jax.ShapeDtypeStruct((1, hw), F32)],
        compiler_params=_cparams("arbitrary"),
    )(ztab, aux, w0, fb0.reshape(1, hid), fw1, fb1.reshape(1, hid), fw2, fb2.reshape(1, hid), fw3,
      freq.reshape(1, hid), deltas)


def _hy_inv_kernel(wr_ref, wi_ref, ur_ref, ui_ref, kr_ref, ki_ref, u_ref, x0_ref, bias_ref, o_ref, acc_ref, *, nf):
    f = pl.program_id(2)

    @pl.when(f == 0)
    def _():
        acc_ref[...] = jnp.zeros_like(acc_ref)

    ur, ui, kr, ki = ur_ref[0], ui_ref[0], kr_ref[0], ki_ref[0]
    packed = jnp.logical_and(lax.broadcasted_iota(I32, ur.shape, 0) == 0, f == 0)
    yr = jnp.where(packed, ur * kr, ur * kr - ui * ki)
    yi = jnp.where(packed, ui * ki, ur * ki + ui * kr)
    acc_ref[...] += _dot(wr_ref[...], yr.astype(BF16)) + _dot(wi_ref[...], yi.astype(BF16))

    @pl.when(f == nf - 1)
    def _():
        u = u_ref[0].astype(F32)
        o_ref[0] = (x0_ref[0].astype(F32) * (acc_ref[...] + u * bias_ref[...])).astype(o_ref.dtype)


def _hyena_inverse(winv, uhat, kspec, u, x0, bias, tm=2048, tf=512):
    b, l, c = u.shape
    nfreq = uhat.shape[1] // 2
    tm = _tile(l, tm)
    tf = _tile(nfreq, tf)
    nf = nfreq // tf
    return pl.pallas_call(
        functools.partial(_hy_inv_kernel, nf=nf),
        grid=(b, l // tm, nf),
        in_specs=[pl.BlockSpec((tm, tf), lambda bi, i, f: (i, f)),
                  pl.BlockSpec((tm, tf), lambda bi, i, f: (i, nf + f)),
                  pl.BlockSpec((1, tf, c), lambda bi, i, f: (bi, f, 0)),
                  pl.BlockSpec((1, tf, c), lambda bi, i, f: (bi, nf + f, 0)),
                  pl.BlockSpec((1, tf, c), lambda bi, i, f: (0, f, 0)),
                  pl.BlockSpec((1, tf, c), lambda bi, i, f: (0, nf + f, 0)),
                  pl.BlockSpec((1, tm, c), lambda bi, i, f: (bi, i, 0)),
                  pl.BlockSpec((1, tm, c), lambda bi, i, f: (bi, i, 0)),
                  pl.BlockSpec((1, c), lambda bi, i, f: (0, 0))],
        out_specs=pl.BlockSpec((1, tm, c), lambda bi, i, f: (bi, i, 0)),
        out_shape=jax.ShapeDtypeStruct((b, l, c), BF16),
        scratch_shapes=[pltpu.VMEM((tm, c), F32)],
        compiler_params=_cparams("parallel", "parallel", "arbitrary"),
    )(winv, winv, uhat, uhat, kspec, kspec, u, x0, bias.reshape(1, c))


LONG_SEQ = 2048
FFT_N2 = 128
FNET_N2 = 64
K1_GROUP = 8
HY_K1_STEP = 4


def _cos_sin(m, n):
    ang = (2.0 * math.pi / n) * (m % n).astype(F32)
    return jnp.cos(ang), jnp.sin(ang)


def _rows(x):
    return x.reshape(-1, x.shape[-1]).astype(BF16)


def _per_sublane(t):
    return jnp.kron(t, jnp.eye(SUBLANES, dtype=F32)).astype(BF16)


def _stage1_kernel(*refs, n_in):
    tabs, ins, o_ref = refs[:n_in], refs[n_in:2 * n_in], refs[2 * n_in]
    acc = _dot(tabs[0][...], _rows(ins[0][0]))
    for t_ref, x_ref in zip(tabs[1:], ins[1:]):
        acc = acc + _dot(t_ref[...], _rows(x_ref[0]))
    o_ref[0] = acc.reshape(o_ref.shape[1:])


def _dft_stage1(tables, inputs, batch_maps, nb):
    k1 = tables[0].shape[0] // 2
    n2w, c = inputs[0].shape[2:]
    tables = [_per_sublane(t) for t in tables]
    in_specs = [pl.BlockSpec(t.shape, lambda p, j: (0, 0)) for t in tables]
    in_specs += [pl.BlockSpec((1, a.shape[1], SUBLANES, c), functools.partial(lambda p, j, m: (m(p), 0, j, 0), m=m))
                 for a, m in zip(inputs, batch_maps)]
    return pl.pallas_call(
        functools.partial(_stage1_kernel, n_in=len(inputs)),
        grid=(nb, n2w // SUBLANES),
        in_specs=in_specs,
        out_specs=pl.BlockSpec((1, 2, k1, SUBLANES, c), lambda p, j: (p, 0, 0, j, 0)),
        out_shape=jax.ShapeDtypeStruct((nb, 2, k1, n2w, c), F32),
        compiler_params=_cparams("parallel", "parallel"),
    )(*tables, *inputs)


def _fnet2_kernel(a_ref, g_ref, o_ref):
    o_ref[0] = _dot(g_ref[0], _rows(a_ref[0])).reshape(o_ref.shape[1:])


def _fnet_long(y_cos, y_sin):
    b, l, c = y_cos.shape
    n2w, n1w, kg = FNET_N2, l // FNET_N2, K1_GROUP
    idx = jnp.arange(n1w, dtype=I32)
    c1, s1 = _cos_sin(idx[:, None] * idx[None, :], n1w)
    t_cos = jnp.concatenate([c1, -s1], axis=0)
    t_sin = jnp.concatenate([-s1, -c1], axis=0)
    view = lambda a: a.reshape(b, n1w, n2w, c)
    a = _dft_stage1([t_cos, t_sin], [view(y_cos), view(y_sin)], [lambda p: p, lambda p: p], b)
    grp = jnp.arange(n1w // kg, dtype=I32)[:, None, None, None]
    k2 = jnp.arange(n2w, dtype=I32)[None, :, None, None]
    j = jnp.arange(kg, dtype=I32)[None, None, :, None]
    n2 = jnp.arange(n2w, dtype=I32)[None, None, None, :]
    c2, s2 = _cos_sin((n1w * k2 + kg * grp + j) * n2, l)
    eye = jnp.eye(kg, dtype=F32)
    expand = lambda t: jnp.einsum('gkjn,ji->gkjin', t, eye).reshape(n1w // kg, n2w * kg, kg * n2w)
    gbig = (jnp.concatenate([expand(c2), expand(s2)], axis=2) * (1.0 / math.sqrt(l))).astype(BF16)
    out = pl.pallas_call(
        _fnet2_kernel,
        grid=(n1w // kg, b),
        in_specs=[pl.BlockSpec((1, 2, kg, n2w, c), lambda i, bi: (bi, 0, i, 0, 0)),
                  pl.BlockSpec((1,) + gbig.shape[1:], lambda i, bi: (i, 0, 0))],
        out_specs=pl.BlockSpec((1, n2w, kg, c), lambda i, bi: (bi, 0, i, 0)),
        out_shape=jax.ShapeDtypeStruct((b, n2w, n1w, c), F32),
        compiler_params=_cparams("parallel", "parallel"),
    )(a, gbig)
    return out.reshape(b, l, c)


def _stage2(g, a):
    return _dot(g, _rows(a))


def _spec_kernel(a_ref, g_ref, ss_ref, o_ref, *, scale):
    n2w = a_ref.shape[3]
    col = lax.rsqrt(ss_ref[...] + NORM_EPS) * scale
    for j in range(a_ref.shape[2]):
        o = _stage2(g_ref[j], a_ref[0, :, j]) * col
        o_ref[0, j] = o[:n2w]
        o_ref[1, j] = o[n2w:]


def _hy_mid_kernel(a_ref, g_ref, gh_ref, k_ref, o_ref):
    n2w = a_ref.shape[3]
    for j in range(a_ref.shape[2]):
        o = _stage2(g_ref[j], a_ref[0, :, j])
        o_r, o_i = o[:n2w], o[n2w:]
        k_r, k_i = k_ref[0, j], k_ref[1, j]
        y = jnp.concatenate([o_r * k_r - o_i * k_i, o_r * k_i + o_i * k_r], axis=0).astype(BF16)
        z = _dot(gh_ref[j], y)
        o_ref[0, 0, j] = z[:n2w]
        o_ref[0, 1, j] = z[n2w:]


def _hy_out_kernel(c_ref, t_ref, u_ref, x0_ref, bias_ref, o_ref):
    y = _dot(t_ref[...], _rows(c_ref[0])).reshape(o_ref.shape)
    o_ref[...] = x0_ref[...] * (y + u_ref[...] * bias_ref[...])


def _hyena_long(uu, x0, bias, k_raw, k_ss):
    b, l, c = uu.shape
    assert b % 2 == 0
    n = 2 * l
    n2w, n1w = FFT_N2, n // FFT_N2
    half, kb = n1w // 2, HY_K1_STEP
    k1 = jnp.arange(n1w, dtype=I32)
    c1, s1 = _cos_sin(k1[:, None] * k1[None, :], n1w)
    bf = lambda t: t.astype(BF16)
    k2 = jnp.arange(n2w, dtype=I32)
    c2, s2 = _cos_sin((n1w * k2[None, :, None] + k1[:, None, None]) * k2[None, None, :], n)
    g = bf(jnp.concatenate([jnp.concatenate([c2, s2], axis=2), jnp.concatenate([-s2, c2], axis=2)], axis=1))
    c2t, s2t = jnp.swapaxes(c2, 1, 2), jnp.swapaxes(s2, 1, 2)
    gh = bf(jnp.concatenate([jnp.concatenate([c2t, -s2t], axis=2), jnp.concatenate([s2t, c2t], axis=2)], axis=1))

    ak = _dft_stage1([jnp.concatenate([c1, -s1], axis=0)], [k_raw.reshape(1, n1w, n2w, c)], [lambda p: 0], 1)
    kspec = pl.pallas_call(
        functools.partial(_spec_kernel, scale=1.0 / n),
        grid=(n1w // kb,),
        in_specs=[pl.BlockSpec((1, 2, kb, n2w, c), lambda i: (0, 0, i, 0, 0)),
                  pl.BlockSpec((kb, 2 * n2w, 2 * n2w), lambda i: (i, 0, 0)),
                  pl.BlockSpec((1, c), lambda i: (0, 0))],
        out_specs=pl.BlockSpec((2, kb, n2w, c), lambda i: (0, i, 0, 0)),
        out_shape=jax.ShapeDtypeStruct((2, n1w, n2w, c), F32),
        compiler_params=_cparams("parallel"),
    )(ak, g, k_ss)

    ch, sh = c1[:, :half], s1[:, :half]
    view = lambda a: a.reshape(b, half, n2w, c)
    a = _dft_stage1([jnp.concatenate([ch, -sh], axis=0), jnp.concatenate([sh, ch], axis=0)],
                    [view(uu), view(uu)], [lambda p: 2 * p, lambda p: 2 * p + 1], b // 2)
    z = pl.pallas_call(
        _hy_mid_kernel,
        grid=(n1w // kb, b // 2),
        in_specs=[pl.BlockSpec((1, 2, kb, n2w, c), lambda i, p: (p, 0, i, 0, 0)),
                  pl.BlockSpec((kb, 2 * n2w, 2 * n2w), lambda i, p: (i, 0, 0)),
                  pl.BlockSpec((kb, 2 * n2w, 2 * n2w), lambda i, p: (i, 0, 0)),
                  pl.BlockSpec((2, kb, n2w, c), lambda i, p: (0, i, 0, 0))],
        out_specs=pl.BlockSpec((1, 2, kb, n2w, c), lambda i, p: (p, 0, i, 0, 0)),
        out_shape=jax.ShapeDtypeStruct((b // 2, 2, n1w, n2w, c), F32),
        compiler_params=_cparams("parallel", "parallel"),
    )(a, g, gh, kspec)
    cht, sht = ch.T, sh.T
    t_inv = _per_sublane(
        jnp.concatenate([jnp.concatenate([cht, -sht], axis=1), jnp.concatenate([sht, cht], axis=1)], axis=0))
    pair = pl.BlockSpec((2, half, SUBLANES, c), lambda p, j: (p, 0, j, 0))
    out = pl.pallas_call(
        _hy_out_kernel,
        grid=(b // 2, n2w // SUBLANES),
        in_specs=[pl.BlockSpec((1, 2, n1w, SUBLANES, c), lambda p, j: (p, 0, 0, j, 0)),
                  pl.BlockSpec(t_inv.shape, lambda p, j: (0, 0)),
                  pair, pair, pl.BlockSpec((1, c), lambda p, j: (0, 0))],
        out_specs=pair,
        out_shape=jax.ShapeDtypeStruct((b, half, n2w, c), F32),
        compiler_params=_cparams("parallel", "parallel"),
    )(z, t_inv, view(uu), view(x0), bias.reshape(1, c))
    return out.reshape(b, l, c)


def _mixer_residual(x_ref, part_refs, wo_ref, gate_ref):
    y, row = None, 0
    for p_ref in part_refs:
        n = p_ref.shape[2]
        t = _dot(p_ref[0].astype(BF16), wo_ref[row:row + n])
        y = t if y is None else y + t
        row += n
    return x_ref[0] + gate_ref[0] * y


def _proj_res_kernel(x_ref, a_ref, wo_ref, gate_ref, o_ref):
    o_ref[0] = _mixer_residual(x_ref, [a_ref], wo_ref, gate_ref)


def _proj_residual(x, a, w_out, gate, tm=512):
    b, l, d = x.shape
    tm = _tile(l, tm)
    tok = lambda n: pl.BlockSpec((1, tm, n), lambda bi, i: (bi, i, 0))
    return pl.pallas_call(
        _proj_res_kernel,
        grid=(b, l // tm),
        in_specs=[tok(d), tok(a.shape[2]), pl.BlockSpec(w_out.shape, lambda bi, i: (0, 0)),
                  pl.BlockSpec((1, 1, d), lambda bi, i: (bi, 0, 0))],
        out_specs=tok(d),
        out_shape=jax.ShapeDtypeStruct((b, l, d), F32),
        compiler_params=_cparams("parallel", "parallel"),
    )(x, a, w_out, gate)


def _ffn_kernel(*refs, n_parts):
    x_ref, part_refs = refs[0], refs[1:1 + n_parts]
    wo_ref, gate1_ref, g_ref, sh_ref, sc_ref, gate_ref, wg_ref, wu_ref, wd_ref, o_ref = refs[1 + n_parts:]
    x = _mixer_residual(x_ref, part_refs, wo_ref, gate1_ref)
    h = _norm_mod(x, g_ref[...], sh_ref[0], sc_ref[0]).astype(BF16)
    mid = _silu(_dot(h, wg_ref[...])) * _dot(h, wu_ref[...])
    o_ref[0] = x + gate_ref[0] * _dot(mid.astype(BF16), wd_ref[...])


def _mixer_out_ffn(x, parts, w_out, gate1, g, shift, scale, gate, wg, wu, wd, tm=512):
    b, l, d = x.shape
    tm = _tile(l, tm)
    vec = pl.BlockSpec((1, 1, d), lambda bi, i: (bi, 0, 0))
    tok = lambda n: pl.BlockSpec((1, tm, n), lambda bi, i: (bi, i, 0))
    resident = lambda w: pl.BlockSpec(w.shape, lambda bi, i: (0, 0), pipeline_mode=pl.Buffered(1))
    return pl.pallas_call(
        functools.partial(_ffn_kernel, n_parts=len(parts)),
        grid=(b, l // tm),
        in_specs=[tok(d)] + [tok(p.shape[2]) for p in parts]
        + [resident(w_out), vec, pl.BlockSpec((1, d), lambda bi, i: (0, 0)), vec, vec, vec,
           resident(wg), resident(wu), resident(wd)],
        out_specs=tok(d),
        out_shape=jax.ShapeDtypeStruct((b, l, d), F32),
        compiler_params=_cparams("parallel", "parallel"),
    )(x, *parts, w_out, gate1, g.reshape(1, d), shift, scale, gate, wg, wu, wd)


def _head_norm(t, e, et, g_full):
    ss = _dot((t * t).astype(BF16), e)
    rinv = lax.rsqrt(ss * (1.0 / HEAD_DIM) + NORM_EPS)
    hi, lo = _split(rinv)
    return t * (_dot(hi, et) + _dot(lo, et)) * g_full


def _rope(t, cos, sin_lo, sin_hi):
    w = t.shape[1]
    rep = w // LANES
    tile = lambda a: jnp.concatenate([a] * rep, axis=1)
    return (t * tile(cos) + pltpu.roll(t, w - ROPE_HALF, 1) * tile(sin_lo)
            + pltpu.roll(t, ROPE_HALF, 1) * tile(sin_hi))


def _qkv_kernel(x_ref, g_ref, sh_ref, sc_ref, w_ref, e_ref, et_ref, qg_ref, kg_ref, cos_ref, sl_ref, sh2_ref,
                *out_refs, qw, kw, rope):
    k_ref, v_ref = out_refs[-2:]
    h = _norm_mod(x_ref[0], g_ref[...], sh_ref[0], sc_ref[0])
    t = _dot(h.astype(BF16), w_ref[...])
    e, et = e_ref[...], et_ref[...]
    k = _head_norm(t[:, qw:qw + kw], e[:kw], et[:, :kw], kg_ref[...])
    if rope:
        k = _rope(k, cos_ref[...], sl_ref[...], sh2_ref[...])
    k_ref[0] = k.astype(BF16)
    v_ref[0] = t[:, qw + kw:].astype(BF16)
    if qw:
        q = _head_norm(t[:, :qw], e, et, qg_ref[...])
        q = _rope(q, cos_ref[...], sl_ref[...], sh2_ref[...]) * (HEAD_DIM ** -0.5)
        out_refs[0][0] = q.astype(BF16)


def _head_tables(qw):
    lane = jnp.arange(qw)
    e = (lane[:, None] // HEAD_DIM == jnp.arange(LANES)[None, :]).astype(BF16)
    return e, e.T


def _rope_tables(l):
    rows = l // GRID_W
    row = jnp.repeat(jnp.arange(rows, dtype=F32), GRID_W)
    col = jnp.tile(jnp.arange(GRID_W, dtype=F32), rows)
    inv = ROPE_THETA ** (-jnp.arange(0, AXIS_ROPE_DIM, 2, dtype=F32) / AXIS_ROPE_DIM)
    lane = jnp.arange(LANES)
    in_head = lane % HEAD_DIM
    use_col = (in_head // AXIS_ROPE_DIM) == 1
    hi_half = ((in_head % AXIS_ROPE_DIM) // ROPE_HALF) == 1
    freq = inv[in_head % ROPE_HALF]
    ang = jnp.where(use_col[None, :], col[:, None], row[:, None]) * freq[None, :]
    cos, sin = jnp.cos(ang), jnp.sin(ang)
    return cos, jnp.where(hi_half[None, :], 0.0, -sin), jnp.where(hi_half[None, :], sin, 0.0)


def _qkv_project(x, g, shift, scale, w, q_g, k_g, qw, kw, rope, tm=512):
    b, l, d = x.shape
    tm = _tile(l, tm)
    e, et = _head_tables(max(qw, kw))
    n_q = max(qw, kw) // HEAD_DIM
    qg = jnp.tile(q_g, n_q).reshape(1, -1)
    kg = jnp.tile(k_g, kw // HEAD_DIM).reshape(1, kw)
    if rope:
        cos, s_lo, s_hi = _rope_tables(l)
    else:
        cos = s_lo = s_hi = jnp.zeros((l, LANES), F32)
    vec = pl.BlockSpec((1, 1, d), lambda bi, i: (bi, 0, 0))
    full = lambda a: pl.BlockSpec(a.shape, lambda bi, i: (0,) * a.ndim)
    tab = pl.BlockSpec((tm, LANES), lambda bi, i: (i, 0))
    widths = ([qw] if qw else []) + [kw, kw]
    return pl.pallas_call(
        functools.partial(_qkv_kernel, qw=qw, kw=kw, rope=rope),
        grid=(b, l // tm),
        in_specs=[pl.BlockSpec((1, tm, d), lambda bi, i: (bi, i, 0)),
                  pl.BlockSpec((1, d), lambda bi, i: (0, 0)), vec, vec,
                  full(w), full(e), full(et), full(qg), full(kg), tab, tab, tab],
        out_specs=[pl.BlockSpec((1, tm, n), lambda bi, i: (bi, i, 0)) for n in widths],
        out_shape=[jax.ShapeDtypeStruct((b, l, n), BF16) for n in widths],
        compiler_params=_cparams("parallel", "parallel"),
    )(x, g.reshape(1, d), shift, scale, w, e, et, qg, kg, cos, s_lo, s_hi)


def _attn_kernel(q_ref, kp_ref, kc_ref, kn_ref, vp_ref, vc_ref, vn_ref, kx_ref, vx_ref, sink_ref, o_ref, *,
                 seq, group):
    qb = pl.program_id(1)
    bq = q_ref.shape[1]
    lc = kx_ref.shape[1]
    nkeys = 3 * bq + lc
    rows = lax.broadcasted_iota(I32, (group * bq, nkeys), 0)
    cols = lax.broadcasted_iota(I32, (group * bq, nkeys), 1)
    qpos = qb * bq + rows % bq
    kpos = (qb - 1) * bq + cols
    valid = jnp.logical_or(
        cols >= 3 * bq,
        jnp.logical_and(jnp.logical_and(kpos >= 0, kpos < seq), jnp.abs(qpos - kpos) <= WINDOW))
    for h in range(N_KV_HEADS):
        ks = slice(h * HEAD_DIM, (h + 1) * HEAD_DIM)
        kh = jnp.concatenate([kp_ref[0, :, ks], kc_ref[0, :, ks], kn_ref[0, :, ks], kx_ref[0, :, ks]], axis=0)
        vh = jnp.concatenate([vp_ref[0, :, ks], vc_ref[0, :, ks], vn_ref[0, :, ks], vx_ref[0, :, ks]], axis=0)
        qh = jnp.concatenate(
            [q_ref[0, :, (h * group + g) * HEAD_DIM:(h * group + g + 1) * HEAD_DIM] for g in range(group)], axis=0)
        s = lax.dot_general(qh, kh, (((1,), (1,)), ((), ())), preferred_element_type=F32)
        s = jnp.where(valid, s, NEG_INF)
        sk = sink_ref[h]
        m = jnp.maximum(jnp.max(s, axis=1, keepdims=True), sk)
        p = jnp.exp(s - m)
        den = jnp.sum(p, axis=1, keepdims=True) + jnp.exp(sk - m)
        o = _dot(p.astype(BF16), vh) / den
        for g in range(group):
            hq = h * group + g
            o_ref[0, :, hq * HEAD_DIM:(hq + 1) * HEAD_DIM] = o[g * bq:(g + 1) * bq].astype(o_ref.dtype)


def _window_attention(q, k, v, kx, vx, sink):
    b, l, qw = q.shape
    kw = k.shape[2]
    lc = kx.shape[1]
    bq = BLOCK_Q
    nb = l // bq
    group = qw // kw
    sink_tab = jnp.repeat(sink.astype(F32).reshape(N_KV_HEADS, group), bq, axis=1)[..., None]
    kv_prev = pl.BlockSpec((1, bq, kw), lambda bi, i: (bi, jnp.maximum(i - 1, 0), 0))
    kv_cur = pl.BlockSpec((1, bq, kw), lambda bi, i: (bi, i, 0))
    kv_next = pl.BlockSpec((1, bq, kw), lambda bi, i: (bi, jnp.minimum(i + 1, nb - 1), 0))
    kv_ctx = pl.BlockSpec((1, lc, kw), lambda bi, i: (bi, 0, 0))
    return pl.pallas_call(
        functools.partial(_attn_kernel, seq=l, group=group),
        grid=(b, nb),
        in_specs=[pl.BlockSpec((1, bq, qw), lambda bi, i: (bi, i, 0)),
                  kv_prev, kv_cur, kv_next, kv_prev, kv_cur, kv_next, kv_ctx, kv_ctx,
                  pl.BlockSpec(sink_tab.shape, lambda bi, i: (0, 0, 0))],
        out_specs=pl.BlockSpec((1, bq, qw), lambda bi, i: (bi, i, 0)),
        out_shape=jax.ShapeDtypeStruct((b, l, qw), BF16),
        compiler_params=_cparams("parallel", "parallel"),
    )(q, k, k, k, v, v, v, kx, vx, sink_tab)


def _router_kernel(x_ref, g_ref, sh_ref, sc_ref, wh_ref, wl_ref, tri_ref, h_ref, gate_ref, rank_ref, rankt_ref,
                   cnt_ref):
    h = _norm_mod(x_ref[0], g_ref[...], sh_ref[0], sc_ref[0])
    h_ref[0] = h.astype(BF16)
    hi, lo = _split(h)
    logits = _dot(hi, wh_ref[...]) + _dot(hi, wl_ref[...]) + _dot(lo, wh_ref[...])
    lane = lax.broadcasted_iota(I32, logits.shape, 1)
    logits = jnp.where(lane < N_EXPERTS, logits, -jnp.inf)
    m1 = jnp.max(logits, axis=1, keepdims=True)
    i1 = jnp.min(jnp.where(logits == m1, lane, LANES), axis=1, keepdims=True)
    rest = jnp.where(lane == i1, -jnp.inf, logits)
    m2 = jnp.max(rest, axis=1, keepdims=True)
    i2 = jnp.min(jnp.where(rest == m2, lane, LANES), axis=1, keepdims=True)
    e = jnp.exp(m2 - m1)
    g1 = 1.0 / (1.0 + e)
    g2 = e / (1.0 + e)
    pick1, pick2 = lane == i1, lane == i2
    member = jnp.logical_or(pick1, pick2)
    gate_ref[0] = jnp.where(pick1, g1, jnp.where(pick2, g2, 0.0))
    m = jnp.where(member, 1.0, 0.0)
    rank = jnp.where(member, _dot(tri_ref[...], m.astype(BF16)), -1.0)
    rank_ref[0] = rank.astype(I32)
    rankt_ref[0] = jnp.transpose(rank)[:SUBLANES].astype(I32)
    cnt_ref[0] = jnp.sum(m, axis=0, keepdims=True).astype(I32)


def _route(x, g, shift, scale, w_router):
    b, l, d = x.shape
    tm = _tile(l, MOE_TILE)
    nt = l // tm
    wr = jnp.concatenate([w_router, jnp.zeros((d, LANES - w_router.shape[1]), F32)], axis=1)
    wh, wl = _split(wr)
    tri = jnp.tril(jnp.ones((tm, tm), BF16), -1)
    vec = pl.BlockSpec((1, 1, d), lambda bi, i: (bi, 0, 0))
    tok = lambda n: pl.BlockSpec((1, tm, n), lambda bi, i: (bi, i, 0))
    return pl.pallas_call(
        _router_kernel,
        grid=(b, nt),
        in_specs=[tok(d), pl.BlockSpec((1, d), lambda bi, i: (0, 0)), vec, vec,
                  pl.BlockSpec((d, LANES), lambda bi, i: (0, 0)),
                  pl.BlockSpec((d, LANES), lambda bi, i: (0, 0)),
                  pl.BlockSpec((tm, tm), lambda bi, i: (0, 0))],
        out_specs=[tok(d), tok(LANES), tok(LANES),
                   pl.BlockSpec((1, SUBLANES, tm), lambda bi, i: (bi * nt + i, 0, 0)),
                   pl.BlockSpec((1, 1, LANES), lambda bi, i: (bi * nt + i, 0, 0))],
        out_shape=[jax.ShapeDtypeStruct((b, l, d), BF16),
                   jax.ShapeDtypeStruct((b, l, LANES), F32),
                   jax.ShapeDtypeStruct((b, l, LANES), I32),
                   jax.ShapeDtypeStruct((b * nt, SUBLANES, tm), I32),
                   jax.ShapeDtypeStruct((b * nt, 1, LANES), I32)],
        compiler_params=_cparams("parallel", "parallel"),
    )(x, g.reshape(1, d), shift, scale, wh, wl, tri)


def _slot_layout(cnt, rows):
    nt = cnt.shape[0]
    seg = (cnt + SUBLANES - 1) // SUBLANES * SUBLANES
    padded = (jnp.sum(seg, axis=0) + rows - 1) // rows * rows
    stride = padded + rows
    pstart = jnp.cumsum(stride) - stride
    off = pstart[None, :] + jnp.cumsum(seg, axis=0) - seg
    n_slots = (nt * N_EXPERTS * (SUBLANES - 1) + nt * MOE_TILE * TOP_K + rows - 1) // rows * rows \
        + 2 * N_EXPERTS * rows
    nblk = n_slots // rows
    bstart = jnp.arange(nblk, dtype=I32) * rows
    be = jnp.minimum(jnp.sum((bstart[:, None] >= (pstart + stride)[None, :]).astype(I32), axis=1), N_EXPERTS - 1)
    onehot = (be[:, None] == jnp.arange(N_EXPERTS, dtype=I32)[None, :]).astype(I32)
    lo = jnp.sum(onehot * pstart[None, :], axis=1)
    hi = jnp.sum(onehot * (pstart + padded)[None, :], axis=1)
    used = jnp.logical_and(bstart >= lo, bstart < hi)
    zero_blk = jnp.logical_or(jnp.logical_not(used), bstart == hi - rows)
    as_i32 = lambda a: a.reshape(-1).astype(I32)
    return n_slots, as_i32(off), as_i32(cnt), as_i32(zero_blk), as_i32(be), as_i32(used)


def _seg_copy(src_ref, dst_ref, row, sem):
    n = src_ref.shape[0]
    return pltpu.make_async_copy(src_ref, dst_ref.at[pl.ds(pl.multiple_of(row, SUBLANES), n)], sem)


def _chunk_copy(src_ref, dst_ref, row, sem):
    n = dst_ref.shape[0]
    return pltpu.make_async_copy(src_ref.at[pl.ds(pl.multiple_of(row, SUBLANES), n)], dst_ref, sem)


def _dispatch_kernel(off_ref, cnt_ref, zero_ref, h_ref, rt_ref, xs_ref, xbuf, xbuf2, sem, sem2, *, rows):
    i = pl.program_id(0)
    cap = xbuf.shape[1]
    tm = h_ref.shape[1]

    @pl.when(i == 0)
    def _():
        xbuf[0] = jnp.zeros(xbuf.shape[1:], F32)

        def zero_block(j, carry):
            @pl.when(zero_ref[j] > 0)
            def _():
                for part in range(rows // cap):
                    c = _seg_copy(xbuf.at[0], xs_ref, j * rows + part * cap, sem2)
                    c.start()
                    c.wait()
            return carry

        lax.fori_loop(0, zero_ref.shape[0], zero_block, 0)

    h = h_ref[0]
    rt = rt_ref[0]
    riota = lax.broadcasted_iota(I32, (cap, tm), 0)

    def copy(e):
        return _seg_copy(xbuf.at[e], xs_ref, off_ref[i * N_EXPERTS + e], sem.at[e])

    for e in range(N_EXPERTS):
        sel = rt[e:e + 1, :]

        def segment(base):
            return _dot(jnp.where(sel == riota + base, 1.0, 0.0).astype(BF16), h)

        xbuf[e] = segment(0)
        copy(e).start()

        @pl.when(cnt_ref[i * N_EXPERTS + e] > cap)
        def _():
            xbuf2[...] = segment(cap)
            c = _seg_copy(xbuf2, xs_ref, off_ref[i * N_EXPERTS + e] + cap, sem2)
            c.start()
            c.wait()

    for e in range(N_EXPERTS):
        copy(e).wait()


def _dispatch(h, rankt, n_slots, off, cnt, zero_blk, rows):
    b, l, d = h.shape
    tm = _tile(l, MOE_TILE)
    nt = l // tm
    cap = MOE_CAP
    assert tm <= 2 * cap and rows % cap == 0
    grid_spec = pltpu.PrefetchScalarGridSpec(
        num_scalar_prefetch=3,
        grid=(b * nt,),
        in_specs=[pl.BlockSpec((1, tm, d), lambda i, *_: (i // nt, i % nt, 0)),
                  pl.BlockSpec((1, SUBLANES, tm), lambda i, *_: (i, 0, 0))],
        out_specs=pl.BlockSpec(memory_space=pl.ANY),
        scratch_shapes=[pltpu.VMEM((N_EXPERTS, cap, d), F32), pltpu.VMEM((cap, d), F32),
                        pltpu.SemaphoreType.DMA((N_EXPERTS,)), pltpu.SemaphoreType.DMA(())],
    )
    return pl.pallas_call(
        functools.partial(_dispatch_kernel, rows=rows),
        grid_spec=grid_spec,
        out_shape=jax.ShapeDtypeStruct((n_slots, d), F32),
        compiler_params=_cparams("arbitrary"),
    )(off, cnt, zero_blk, h, rankt)


def _moe_kernel(be_ref, used_ref, x_ref, wg_ref, wu_ref, wd_ref, o_ref):
    i = pl.program_id(0)

    @pl.when(used_ref[i] > 0)
    def _():
        x = x_ref[...].astype(BF16)
        mid = _silu(_dot(x, wg_ref[0])) * _dot(x, wu_ref[0])
        o_ref[...] = _dot(mid.astype(BF16), wd_ref[0])

    @pl.when(used_ref[i] == 0)
    def _():
        o_ref[...] = jnp.zeros_like(o_ref)


def _expert_ffn(xs, block_e, used, wg, wu, wd, rows):
    s, d = xs.shape
    nblk = s // rows
    expert = lambda w: pl.BlockSpec((1,) + w.shape[1:], lambda i, be, us: (be[i], 0, 0),
                                    pipeline_mode=pl.Buffered(1))
    grid_spec = pltpu.PrefetchScalarGridSpec(
        num_scalar_prefetch=2,
        grid=(nblk,),
        in_specs=[pl.BlockSpec((rows, d), lambda i, be, us: (i, 0)), expert(wg), expert(wu), expert(wd)],
        out_specs=pl.BlockSpec((rows, d), lambda i, be, us: (i, 0)),
    )
    return pl.pallas_call(
        _moe_kernel,
        grid_spec=grid_spec,
        out_shape=jax.ShapeDtypeStruct((s, d), F32),
        compiler_params=_cparams("arbitrary"),
    )(block_e, used, xs, wg, wu, wd)


def _combine_kernel(off_ref, cnt_ref, x_ref, gate_ref, rg_ref, rank_ref, ys_ref, o_ref, buf, buf2, acc_ref, sem,
                    sem2, *, n_tiles):
    i = pl.program_id(0)
    slot = i % 2
    cap = buf.shape[2]
    tm = x_ref.shape[1]

    def chunk(t, s, e):
        return _chunk_copy(ys_ref, buf.at[s, e], off_ref[t * N_EXPERTS + e], sem.at[s, e])

    @pl.when(i == 0)
    def _():
        for e in range(N_EXPERTS):
            chunk(0, 0, e).start()

    @pl.when(i + 1 < n_tiles)
    def _():
        for e in range(N_EXPERTS):
            chunk(i + 1, 1 - slot, e).start()

    rank = rank_ref[0]
    rg = rg_ref[0]
    liota = lax.broadcasted_iota(I32, (tm, cap), 1)

    def picked(e, base, rows_ref):
        q = jnp.where(rank[:, e:e + 1] == liota + base, 1.0, 0.0).astype(BF16)
        return rg[:, e:e + 1] * _dot(q, rows_ref[...].astype(BF16))

    y = jnp.zeros(acc_ref.shape, F32)
    for e in range(N_EXPERTS):
        chunk(i, slot, e).wait()
        y = y + picked(e, 0, buf.at[slot, e])
    acc_ref[...] = y

    for e in range(N_EXPERTS):
        @pl.when(cnt_ref[i * N_EXPERTS + e] > cap)
        def _():
            c = _chunk_copy(ys_ref, buf2, off_ref[i * N_EXPERTS + e] + cap, sem2)
            c.start()
            c.wait()
            acc_ref[...] += picked(e, cap, buf2)

    o_ref[0] = x_ref[0] + gate_ref[0] * acc_ref[...]


def _moe_combine(x, gate, route_gates, rank, ys, off, cnt):
    b, l, d = x.shape
    tm = _tile(l, MOE_TILE)
    nt = l // tm
    cap = MOE_CAP
    tok = lambda n: pl.BlockSpec((1, tm, n), lambda i, *_: (i // nt, i % nt, 0))
    grid_spec = pltpu.PrefetchScalarGridSpec(
        num_scalar_prefetch=2,
        grid=(b * nt,),
        in_specs=[tok(d), pl.BlockSpec((1, 1, d), lambda i, *_: (i // nt, 0, 0)), tok(LANES), tok(LANES),
                  pl.BlockSpec(memory_space=pl.ANY)],
        out_specs=tok(d),
        scratch_shapes=[pltpu.VMEM((2, N_EXPERTS, cap, d), F32), pltpu.VMEM((cap, d), F32),
                        pltpu.VMEM((tm, d), F32),
                        pltpu.SemaphoreType.DMA((2, N_EXPERTS)), pltpu.SemaphoreType.DMA(())],
    )
    return pl.pallas_call(
        functools.partial(_combine_kernel, n_tiles=b * nt),
        grid_spec=grid_spec,
        out_shape=jax.ShapeDtypeStruct((b, l, d), F32),
        compiler_params=_cparams("arbitrary"),
    )(off, cnt, x, gate, route_gates, rank, ys)


def _angles(rows, cols, n):
    m = (rows[:, None] * cols[None, :]) % n
    return (2.0 * math.pi / n) * m.astype(F32)


def _fnet_table(l):
    idx = jnp.arange(l, dtype=I32)
    ang = _angles(idx, idx, l)
    s = 1.0 / math.sqrt(l)
    return jnp.concatenate([jnp.cos(ang) * s, jnp.sin(ang) * (-s)], axis=1).astype(BF16)


def _rfft_table(n):
    half = n // 2
    f = jnp.arange(half, dtype=I32)
    t = jnp.arange(n, dtype=I32)
    ang = _angles(f, t, n)
    top = jnp.cos(ang)
    bot = -jnp.sin(ang)
    nyq = jnp.cos(_angles(jnp.full((1,), half, I32), t, n))
    bot = jnp.concatenate([nyq, bot[1:]], axis=0)
    return jnp.concatenate([top, bot], axis=0).astype(BF16)


def _group_dft_table(width):
    gd = FNET_GROUP_DIM
    idx = jnp.arange(gd, dtype=I32)
    ang = _angles(idx, idx, gd)
    s = 1.0 / math.sqrt(gd)
    eye = jnp.eye(width // gd, dtype=F32)
    return jnp.concatenate([jnp.kron(eye, jnp.cos(ang) * s), jnp.kron(eye, jnp.sin(ang) * s)], axis=1)


def _even_mixer(x, g1, shift, scale, w_in_f, conv_w, conv_b, hy_bias, filt, fw, hw):
    b, l, d = x.shape
    n = 2 * l
    long_seq = l >= LONG_SEQ
    seq_dtype = F32 if long_seq else BF16
    u, y_cos, y_sin = _norm_mod_matmul(x, g1, shift, scale, w_in_f, [3 * hw, fw, fw], [BF16, seq_dtype, seq_dtype])
    uu, x0 = _hyena_pre(u, conv_w, conv_b, hw, seq_dtype)
    k_raw, k_ss = filt(l)
    if long_seq:
        a = _fnet_long(y_cos, y_sin)
        hy = _hyena_long(uu, x0, hy_bias, k_raw, k_ss)
    else:
        y_both = jnp.concatenate([y_cos, y_sin], axis=2)
        a = _left_dft(_fnet_table(l), y_both, lambda bi, k, tk: (bi, k % (l // tk), k // (l // tk)), n, fw, BF16,
                      tk=min(l, 1024))
        wf = _rfft_table(n)
        wts = jnp.concatenate([jnp.ones((1,), F32), jnp.full((l - 1,), 2.0, F32)]) / n
        row_scale = jnp.concatenate([wts, wts]).reshape(n, 1)
        kspec = _left_dft(wf, k_raw[None], lambda bi, k, tk: (0, k, 0), n, hw, F32,
                          row_scale=row_scale, col_sumsq=k_ss)
        uhat = _left_dft(wf, uu, lambda bi, k, tk: (bi, k, 0), l, hw, F32)
        hy = _hyena_inverse(wf[:, :l].T, uhat, kspec, uu, x0, hy_bias)
    return a, hy


def kernel(x, c, ctx, c_ctx, ada_w, ada_b, norm1_g, norm2_g, ev_w_in, ev_w_out, hy_conv_w, hy_conv_b, hy_bias,
           hf_w0, hf_b0, hf_w1, hf_b1, hf_w2, hf_b2, hf_w3, hf_freq, ffn_w_gate, ffn_w_up, ffn_w_down, od_w_qkv,
           od_w_out, q_norm_g, k_norm_g, attn_sink, moe_router, moe_w_gate, moe_w_up, moe_w_down):
    b, l, d = x.shape
    lc = ctx.shape[1]
    assert ada_w.shape[0] == 2, "this implementation covers the two-layer (even, odd) stack"
    hw = hy_bias.shape[1]
    fw = ev_w_in.shape[2] - 3 * hw
    qw = od_w_out.shape[1]
    kw = (od_w_qkv.shape[2] - qw) // 2

    rows = (b + 1 + 7) // 8 * 8
    cc = jnp.concatenate([c, c_ctx[None, :], jnp.zeros((rows - b - 1, d), F32)], axis=0)
    mod = _ada_vectors(cc, ada_w, ada_b)
    ml = [[mod[i, :b, None, m * d:(m + 1) * d] for m in range(N_MOD)] for i in range(2)]
    mc = [[mod[i, b:b + 1, None, m * d:(m + 1) * d] for m in range(N_MOD)] for i in range(2)]

    w_in = ev_w_in[0]
    w_fnet = _matmul3(w_in[:, :fw], _group_dft_table(fw))
    w_in_f = jnp.concatenate([w_in[:, fw:], w_fnet], axis=1).astype(BF16)
    w_out0 = ev_w_out[0].astype(BF16)
    filt = lambda seq: _hyena_filter(seq, hw, hf_w0[0], hf_b0[0], hf_w1[0], hf_b1[0], hf_w2[0], hf_b2[0],
                                     hf_w3[0], hf_freq[0])
    wg, wu, wd = ffn_w_gate[0].astype(BF16), ffn_w_up[0].astype(BF16), ffn_w_down[0].astype(BF16)

    mix = _even_mixer(x, norm1_g[0], ml[0][0], ml[0][1], w_in_f, hy_conv_w[0], hy_conv_b[0], hy_bias[0], filt, fw,
                      hw)
    x = _mixer_out_ffn(x, mix, w_out0, ml[0][2], norm2_g[0], ml[0][3], ml[0][4], ml[0][5], wg, wu, wd)

    bc = lambda v: jnp.broadcast_to(v, (b, 1, d))
    mix = _even_mixer(ctx, norm1_g[0], bc(mc[0][0]), bc(mc[0][1]), w_in_f, hy_conv_w[0], hy_conv_b[0], hy_bias[0],
                      filt, fw, hw)
    flat = lambda t: t.reshape(1, b * lc, t.shape[2])
    ctx = _mixer_out_ffn(flat(ctx), [flat(t) for t in mix], w_out0, mc[0][2], norm2_g[0], mc[0][3], mc[0][4],
                         mc[0][5], wg, wu, wd)

    w_qkv = od_w_qkv[0].astype(BF16)
    q, k, v = _qkv_project(x, norm1_g[1], ml[1][0], ml[1][1], w_qkv, q_norm_g[0], k_norm_g[0], qw, kw, True)
    kx, vx = _qkv_project(ctx, norm1_g[1], mc[1][0], mc[1][1], w_qkv[:, qw:], q_norm_g[0], k_norm_g[0], 0, kw,
                          False)
    o = _window_attention(q, k, v, kx.reshape(b, lc, kw), vx.reshape(b, lc, kw), attn_sink[0])
    x = _proj_residual(x, o, od_w_out[0].astype(BF16), ml[1][2])

    h2, gates, rank, rankt, cnt = _route(x, norm2_g[1], ml[1][3], ml[1][4], moe_router[0])
    n_slots, off, cnt, zero_blk, block_e, used = _slot_layout(cnt[:, 0, :N_EXPERTS], MOE_ROWS)
    xs = _dispatch(h2, rankt, n_slots, off, cnt, zero_blk, MOE_ROWS)
    ys = _expert_ffn(xs, block_e, used, moe_w_gate[0].astype(BF16), moe_w_up[0].astype(BF16),
                     moe_w_down[0].astype(BF16), MOE_ROWS)
    return _moe_combine(x, ml[1][5], gates, rank, ys, off, cnt)
```

```python
import functools
import math

import jax
import jax.numpy as jnp
import numpy as np
from jax import lax
from jax.experimental import pallas as pl
from jax.experimental.pallas import tpu as pltpu

F32 = jnp.float32
BF16 = jnp.bfloat16
I32 = jnp.int32

NORM_EPS = 1e-6
NEG_INF = -1e30
N_MOD = 6

FNET_GROUP_DIM = 128
HY_EMB_BANDS = 16
HY_FILTER_HIDDEN = 64
HY_DECAY_TARGET = 1e-2
HY_FAST_DECAY = 0.3
HY_SLOW_DECAY = 1.5

HEAD_DIM = 64
N_KV_HEADS = 4
GRID_W = 64
WINDOW = 128
BLOCK_Q = 128
ROPE_THETA = 10000.0
AXIS_ROPE_DIM = HEAD_DIM // 2
ROPE_HALF = AXIS_ROPE_DIM // 2
N_EXPERTS = 8
TOP_K = 2

LANES = 128
SUBLANES = 8
VMEM_LIMIT = 56 * 1024 * 1024
MOE_TILE = 512
MOE_ROWS = 512
MOE_CAP = 256


def _cparams(*sem):
    return pltpu.CompilerParams(dimension_semantics=sem, vmem_limit_bytes=VMEM_LIMIT)


def _split(a):
    hi = a.astype(BF16)
    lo = (a - hi.astype(F32)).astype(BF16)
    return hi, lo


def _dot(a, b):
    return jnp.dot(a, b, preferred_element_type=F32)


def _dot3(a, b):
    ah, al = _split(a)
    bh, bl = _split(b)
    return _dot(ah, bh) + _dot(ah, bl) + _dot(al, bh)


def _silu(t):
    return t / (1.0 + jnp.exp(-t))


def _norm_mod(x, g, shift, scale):
    ms = jnp.mean(x * x, axis=-1, keepdims=True)
    y = x * lax.rsqrt(ms + NORM_EPS) * g
    return y * (1.0 + scale) + shift


def _tile(n, pref):
    t = min(n, pref)
    assert n % t == 0, (n, pref)
    return t


def _ada_kernel(c_ref, w_ref, b_ref, o_ref):
    o_ref[0] = _dot3(_silu(c_ref[...]), w_ref[0]) + b_ref[0]


def _ada_vectors(cc, ada_w, ada_b):
    depth, d, n = ada_w.shape
    rows = cc.shape[0]
    tn = _tile(n, 1536)
    return pl.pallas_call(
        _ada_kernel,
        grid=(depth, n // tn),
        in_specs=[pl.BlockSpec((rows, d), lambda l, j: (0, 0)),
                  pl.BlockSpec((1, d, tn), lambda l, j: (l, 0, j)),
                  pl.BlockSpec((1, 1, tn), lambda l, j: (l, 0, j))],
        out_specs=pl.BlockSpec((1, rows, tn), lambda l, j: (l, 0, j)),
        out_shape=jax.ShapeDtypeStruct((depth, rows, n), F32),
        compiler_params=_cparams("arbitrary", "arbitrary"),
    )(cc, ada_w, ada_b.reshape(depth, 1, n))


def _mm3_kernel(a_ref, b_ref, o_ref):
    o_ref[...] = _dot3(a_ref[...], b_ref[...])


def _matmul3(a, b):
    m, _ = a.shape
    n = b.shape[1]
    return pl.pallas_call(_mm3_kernel, out_shape=jax.ShapeDtypeStruct((m, n), F32),
                          compiler_params=_cparams())(a, b)


def _nmm_kernel(x_ref, g_ref, sh_ref, sc_ref, w_ref, *o_refs):
    h = _norm_mod(x_ref[0], g_ref[...], sh_ref[0], sc_ref[0])
    y = _dot(h.astype(BF16), w_ref[...])
    col = 0
    for o_ref in o_refs:
        n = o_ref.shape[2]
        o_ref[0] = y[:, col:col + n].astype(o_ref.dtype)
        col += n


def _norm_mod_matmul(x, g, shift, scale, w, widths, out_dtypes, tm=512):
    b, l, d = x.shape
    n = w.shape[1]
    assert sum(widths) == n
    tm = _tile(l, tm)
    return pl.pallas_call(
        _nmm_kernel,
        grid=(b, l // tm),
        in_specs=[pl.BlockSpec((1, tm, d), lambda bi, i: (bi, i, 0)),
                  pl.BlockSpec((1, d), lambda bi, i: (0, 0)),
                  pl.BlockSpec((1, 1, d), lambda bi, i: (bi, 0, 0)),
                  pl.BlockSpec((1, 1, d), lambda bi, i: (bi, 0, 0)),
                  pl.BlockSpec((d, n), lambda bi, i: (0, 0))],
        out_specs=[pl.BlockSpec((1, tm, wd), lambda bi, i: (bi, i, 0)) for wd in widths],
        out_shape=[jax.ShapeDtypeStruct((b, l, wd), dt) for wd, dt in zip(widths, out_dtypes)],
        compiler_params=_cparams("parallel", "parallel"),
    )(x, g.reshape(1, d), shift, scale, w)


def _ldft_kernel(w_ref, r_ref, o_ref, acc_ref, *, nk):
    k = pl.program_id(2)

    @pl.when(k == 0)
    def _():
        acc_ref[...] = jnp.zeros_like(acc_ref)

    acc_ref[...] += _dot(w_ref[...], r_ref[0].astype(BF16))

    @pl.when(k == nk - 1)
    def _():
        o_ref[0] = acc_ref[...].astype(o_ref.dtype)


def _ldft_scaled_kernel(w_ref, r_ref, rs_ref, ss_ref, o_ref, acc_ref, *, nk):
    k = pl.program_id(2)

    @pl.when(k == 0)
    def _():
        acc_ref[...] = jnp.zeros_like(acc_ref)

    acc_ref[...] += _dot(w_ref[...], r_ref[0].astype(BF16))

    @pl.when(k == nk - 1)
    def _():
        o_ref[0] = acc_ref[...] * rs_ref[...] * lax.rsqrt(ss_ref[...] + NORM_EPS)


def _left_dft(w, rhs, rhs_map, kdim, c, out_dtype, tm=2048, tk=1024, row_scale=None, col_sumsq=None):
    m = w.shape[0]
    nb = rhs.shape[0]
    tm = _tile(m, tm)
    tk = _tile(kdim, tk)
    nk = kdim // tk
    in_specs = [pl.BlockSpec((tm, tk), lambda b, i, k: (i, k)),
                pl.BlockSpec((1, tk, c), lambda b, i, k: rhs_map(b, k, tk))]
    args = [w, rhs]
    if row_scale is None:
        body = functools.partial(_ldft_kernel, nk=nk)
    else:
        body = functools.partial(_ldft_scaled_kernel, nk=nk)
        in_specs += [pl.BlockSpec((tm, 1), lambda b, i, k: (i, 0)),
                     pl.BlockSpec((1, c), lambda b, i, k: (0, 0))]
        args += [row_scale, col_sumsq]
    return pl.pallas_call(
        body,
        grid=(nb, m // tm, nk),
        in_specs=in_specs,
        out_specs=pl.BlockSpec((1, tm, c), lambda b, i, k: (b, i, 0)),
        out_shape=jax.ShapeDtypeStruct((nb, m, c), out_dtype),
        scratch_shapes=[pltpu.VMEM((tm, c), F32)],
        compiler_params=_cparams("parallel", "parallel", "arbitrary"),
    )(*args)


def _hy_pre_kernel(u_ref, up_ref, un_ref, cw_ref, cb_ref, uo_ref, x0_ref, *, nl, tl, hw):
    i = pl.program_id(1)
    u = u_ref[0].astype(F32)
    halo = up_ref.shape[1]
    prev = up_ref[0].astype(F32)[halo - 1:halo]
    nxt = un_ref[0].astype(F32)[0:1]
    prev = jnp.where(i == 0, 0.0, prev)
    nxt = jnp.where(i == nl - 1, 0.0, nxt)
    rows = lax.broadcasted_iota(I32, u.shape, 0)
    um = jnp.where(rows == 0, prev, pltpu.roll(u, 1, 0))
    up = jnp.where(rows == tl - 1, nxt, pltpu.roll(u, tl - 1, 0))
    cw = cw_ref[...]
    z = um * cw[0:1] + u * cw[1:2] + up * cw[2:3] + cb_ref[...]
    uo_ref[0] = (z[:, :hw] * z[:, hw:2 * hw]).astype(uo_ref.dtype)
    x0_ref[0] = z[:, 2 * hw:].astype(x0_ref.dtype)


def _hyena_pre(u, conv_w, conv_b, hw, out_dtype, tl=512):
    b, l, _ = u.shape
    tl = _tile(l, tl)
    nl = l // tl
    halo = 16
    hb = tl // halo
    nh = l // halo
    w3 = 3 * hw
    return pl.pallas_call(
        functools.partial(_hy_pre_kernel, nl=nl, tl=tl, hw=hw),
        grid=(b, nl),
        in_specs=[pl.BlockSpec((1, tl, w3), lambda bi, i: (bi, i, 0)),
                  pl.BlockSpec((1, halo, w3), lambda bi, i: (bi, jnp.maximum(i * hb - 1, 0), 0)),
                  pl.BlockSpec((1, halo, w3), lambda bi, i: (bi, jnp.minimum((i + 1) * hb, nh - 1), 0)),
                  pl.BlockSpec((3, w3), lambda bi, i: (0, 0)),
                  pl.BlockSpec((1, w3), lambda bi, i: (0, 0))],
        out_specs=[pl.BlockSpec((1, tl, hw), lambda bi, i: (bi, i, 0)),
                   pl.BlockSpec((1, tl, hw), lambda bi, i: (bi, i, 0))],
        out_shape=[jax.ShapeDtypeStruct((b, l, hw), out_dtype), jax.ShapeDtypeStruct((b, l, hw), out_dtype)],
        compiler_params=_cparams("parallel", "parallel"),
    )(u, u, u, conv_w, conv_b.reshape(1, w3))


def _hy_filter_kernel(z_ref, aux_ref, w0_ref, b0_ref, w1_ref, b1_ref, w2_ref, b2_ref, w3_ref, fr_ref, dl_ref,
                      k_ref, ss_ref, *, hw):
    i = pl.program_id(0)
    fr = fr_ref[...]
    h = jnp.sin(fr * (_dot3(z_ref[...], w0_ref[...]) + b0_ref[...]))
    h = jnp.sin(fr * (_dot3(h, w1_ref[...]) + b1_ref[...]))
    h = jnp.sin(fr * (_dot3(h, w2_ref[...]) + b2_ref[...]))
    h = _dot3(h, w3_ref[...])
    aux = aux_ref[...]
    t, m_fwd, m_bwd = aux[:, 0:1], aux[:, 1:2], aux[:, 2:3]
    k = (h[:, :hw] * m_fwd + h[:, hw:] * m_bwd) * jnp.exp(-t * dl_ref[...])
    k_ref[...] = k

    @pl.when(i == 0)
    def _():
        ss_ref[...] = jnp.zeros_like(ss_ref)

    ss_ref[...] += jnp.sum(k * k, axis=0, keepdims=True)


def _hyena_filter(l, hw, fw0, fb0, fw1, fb1, fw2, fb2, fw3, freq):
    n = 2 * l
    hid = HY_FILTER_HIDDEN
    r = jnp.arange(n)
    pos = jnp.where(r < l, r, n - r).astype(F32)
    t = pos / max(l - 1, 1)
    wv = (2.0 * math.pi / l) * pos
    bands = jnp.linspace(1e-4, HY_EMB_BANDS - 1, HY_EMB_BANDS, dtype=F32)
    ang = wv[:, None] * bands[None, :]
    ztab = jnp.concatenate([t[:, None], jnp.cos(ang), -jnp.sin(ang),
                            jnp.zeros((n, hid - 1 - 2 * HY_EMB_BANDS), F32)], axis=-1)
    aux = jnp.zeros((n, LANES), F32)
    aux = aux.at[:, 0].set(t).at[:, 1].set((r < l).astype(F32)).at[:, 2].set((r > l).astype(F32))
    w0 = jnp.concatenate([fw0, jnp.zeros((hid - fw0.shape[0], hid), F32)], axis=0)
    deltas = jnp.abs(jnp.linspace(math.log(HY_DECAY_TARGET) / HY_SLOW_DECAY,
                                  math.log(HY_DECAY_TARGET) / HY_FAST_DECAY, hw, dtype=F32)).reshape(1, hw)
    tr = _tile(n, 1024)
    full = lambda shape: pl.BlockSpec(shape, lambda i: (0, 0))
    return pl.pallas_call(
        functools.partial(_hy_filter_kernel, hw=hw),
        grid=(n // tr,),
        in_specs=[pl.BlockSpec((tr, hid), lambda i: (i, 0)),
                  pl.BlockSpec((tr, LANES), lambda i: (i, 0)),
                  full((hid, hid)), full((1, hid)), full((hid, hid)), full((1, hid)),
                  full((hid, hid)), full((1, hid)), full((hid, 2 * hw)), full((1, hid)), full((1, hw))],
        out_specs=[pl.BlockSpec((tr, hw), lambda i: (i, 0)), full((1, hw))],
        out_shape=[jax.ShapeDtypeStruct((n, hw), F32), jax.ShapeDtypeStruct((1, hw), F32)],
        compiler_params=_cparams("arbitrary"),
    )(ztab, aux, w0, fb0.reshape(1, hid), fw1, fb1.reshape(1, hid), fw2, fb2.reshape(1, hid), fw3,
      freq.reshape(1, hid), deltas)


def _hy_inv_kernel(wr_ref, wi_ref, ur_ref, ui_ref, kr_ref, ki_ref, u_ref, x0_ref, bias_ref, o_ref, acc_ref, *, nf):
    f = pl.program_id(2)

    @pl.when(f == 0)
    def _():
        acc_ref[...] = jnp.zeros_like(acc_ref)

    ur, ui, kr, ki = ur_ref[0], ui_ref[0], kr_ref[0], ki_ref[0]
    packed = jnp.logical_and(lax.broadcasted_iota(I32, ur.shape, 0) == 0, f == 0)
    yr = jnp.where(packed, ur * kr, ur * kr - ui * ki)
    yi = jnp.where(packed, ui * ki, ur * ki + ui * kr)
    acc_ref[...] += _dot(wr_ref[...], yr.astype(BF16)) + _dot(wi_ref[...], yi.astype(BF16))

    @pl.when(f == nf - 1)
    def _():
        u = u_ref[0].astype(F32)
        o_ref[0] = (x0_ref[0].astype(F32) * (acc_ref[...] + u * bias_ref[...])).astype(o_ref.dtype)


def _hyena_inverse(winv, uhat, kspec, u, x0, bias, tm=2048, tf=512):
    b, l, c = u.shape
    nfreq = uhat.shape[1] // 2
    tm = _tile(l, tm)
    tf = _tile(nfreq, tf)
    nf = nfreq // tf
    return pl.pallas_call(
        functools.partial(_hy_inv_kernel, nf=nf),
        grid=(b, l // tm, nf),
        in_specs=[pl.BlockSpec((tm, tf), lambda bi, i, f: (i, f)),
                  pl.BlockSpec((tm, tf), lambda bi, i, f: (i, nf + f)),
                  pl.BlockSpec((1, tf, c), lambda bi, i, f: (bi, f, 0)),
                  pl.BlockSpec((1, tf, c), lambda bi, i, f: (bi, nf + f, 0)),
                  pl.BlockSpec((1, tf, c), lambda bi, i, f: (0, f, 0)),
                  pl.BlockSpec((1, tf, c), lambda bi, i, f: (0, nf + f, 0)),
                  pl.BlockSpec((1, tm, c), lambda bi, i, f: (bi, i, 0)),
                  pl.BlockSpec((1, tm, c), lambda bi, i, f: (bi, i, 0)),
                  pl.BlockSpec((1, c), lambda bi, i, f: (0, 0))],
        out_specs=pl.BlockSpec((1, tm, c), lambda bi, i, f: (bi, i, 0)),
        out_shape=jax.ShapeDtypeStruct((b, l, c), BF16),
        scratch_shapes=[pltpu.VMEM((tm, c), F32)],
        compiler_params=_cparams("parallel", "parallel", "arbitrary"),
    )(winv, winv, uhat, uhat, kspec, kspec, u, x0, bias.reshape(1, c))


LONG_SEQ = 2048
FFT_N2 = 128
FNET_N2 = 64
K1_GROUP = 8
HY_K1_STEP = 4


def _cos_sin(m, n):
    ang = (2.0 * math.pi / n) * (m % n).astype(np.float64)
    return np.cos(ang).astype(np.float32), np.sin(ang).astype(np.float32)


def _rows(x):
    return x.reshape(-1, x.shape[-1]).astype(BF16)


def _per_sublane(t):
    return np.kron(t, np.eye(SUBLANES, dtype=np.float32)).astype(BF16)


def _stage1_kernel(*refs, n_in):
    tabs, ins, o_ref = refs[:n_in], refs[n_in:2 * n_in], refs[2 * n_in]
    acc = _dot(tabs[0][...], _rows(ins[0][0]))
    for t_ref, x_ref in zip(tabs[1:], ins[1:]):
        acc = acc + _dot(t_ref[...], _rows(x_ref[0]))
    o_ref[0] = acc.reshape(o_ref.shape[1:])


def _dft_stage1(tables, inputs, batch_maps, nb):
    k1 = tables[0].shape[0] // 2
    n2w, c = inputs[0].shape[2:]
    tables = [_per_sublane(t) for t in tables]
    in_specs = [pl.BlockSpec(t.shape, lambda p, j: (0, 0)) for t in tables]
    in_specs += [pl.BlockSpec((1, a.shape[1], SUBLANES, c), functools.partial(lambda p, j, m: (m(p), 0, j, 0), m=m))
                 for a, m in zip(inputs, batch_maps)]
    return pl.pallas_call(
        functools.partial(_stage1_kernel, n_in=len(inputs)),
        grid=(nb, n2w // SUBLANES),
        in_specs=in_specs,
        out_specs=pl.BlockSpec((1, 2, k1, SUBLANES, c), lambda p, j: (p, 0, 0, j, 0)),
        out_shape=jax.ShapeDtypeStruct((nb, 2, k1, n2w, c), F32),
        compiler_params=_cparams("parallel", "parallel"),
    )(*tables, *inputs)


def _fnet2_kernel(a_ref, g_ref, o_ref):
    o_ref[0] = _dot(g_ref[0], _rows(a_ref[0])).reshape(o_ref.shape[1:])


def _fnet_long(y_cos, y_sin):
    b, l, c = y_cos.shape
    n2w, n1w, kg = FNET_N2, l // FNET_N2, K1_GROUP
    idx = np.arange(n1w)
    c1, s1 = _cos_sin(idx[:, None] * idx[None, :], n1w)
    t_cos = np.concatenate([c1, -s1], axis=0)
    t_sin = np.concatenate([-s1, -c1], axis=0)
    view = lambda a: a.reshape(b, n1w, n2w, c)
    a = _dft_stage1([t_cos, t_sin], [view(y_cos), view(y_sin)], [lambda p: p, lambda p: p], b)
    grp = np.arange(n1w // kg)[:, None, None, None]
    k2 = np.arange(n2w)[None, :, None, None]
    j = np.arange(kg)[None, None, :, None]
    n2 = np.arange(n2w)[None, None, None, :]
    c2, s2 = _cos_sin((n1w * k2 + kg * grp + j) * n2, l)
    eye = np.eye(kg, dtype=np.float32)
    expand = lambda t: np.einsum('gkjn,ji->gkjin', t, eye).reshape(n1w // kg, n2w * kg, kg * n2w)
    gbig = (np.concatenate([expand(c2), expand(s2)], axis=2) * np.float32(1.0 / math.sqrt(l))).astype(BF16)
    out = pl.pallas_call(
        _fnet2_kernel,
        grid=(n1w // kg, b),
        in_specs=[pl.BlockSpec((1, 2, kg, n2w, c), lambda i, bi: (bi, 0, i, 0, 0)),
                  pl.BlockSpec((1,) + gbig.shape[1:], lambda i, bi: (i, 0, 0))],
        out_specs=pl.BlockSpec((1, n2w, kg, c), lambda i, bi: (bi, 0, i, 0)),
        out_shape=jax.ShapeDtypeStruct((b, n2w, n1w, c), F32),
        compiler_params=_cparams("parallel", "parallel"),
    )(a, gbig)
    return out.reshape(b, l, c)


def _stage2(g, a):
    return _dot(g, _rows(a))


def _spec_kernel(a_ref, g_ref, ss_ref, o_ref, *, scale):
    n2w = a_ref.shape[3]
    col = lax.rsqrt(ss_ref[...] + NORM_EPS) * scale
    for j in range(a_ref.shape[2]):
        o = _stage2(g_ref[j], a_ref[0, :, j]) * col
        o_ref[0, j] = o[:n2w]
        o_ref[1, j] = o[n2w:]


def _hy_mid_kernel(a_ref, g_ref, gh_ref, k_ref, o_ref):
    n2w = a_ref.shape[3]
    for j in range(a_ref.shape[2]):
        o = _stage2(g_ref[j], a_ref[0, :, j])
        o_r, o_i = o[:n2w], o[n2w:]
        k_r, k_i = k_ref[0, j], k_ref[1, j]
        y = jnp.concatenate([o_r * k_r - o_i * k_i, o_r * k_i + o_i * k_r], axis=0).astype(BF16)
        z = _dot(gh_ref[j], y)
        o_ref[0, 0, j] = z[:n2w]
        o_ref[0, 1, j] = z[n2w:]


def _hy_out_kernel(c_ref, t_ref, u_ref, x0_ref, bias_ref, o_ref):
    y = _dot(t_ref[...], _rows(c_ref[0])).reshape(o_ref.shape)
    o_ref[...] = x0_ref[...] * (y + u_ref[...] * bias_ref[...])


def _hyena_long(uu, x0, bias, k_raw, k_ss):
    b, l, c = uu.shape
    assert b % 2 == 0
    n = 2 * l
    n2w, n1w = FFT_N2, n // FFT_N2
    half, kb = n1w // 2, HY_K1_STEP
    k1 = np.arange(n1w)
    c1, s1 = _cos_sin(k1[:, None] * k1[None, :], n1w)
    bf = lambda t: t.astype(BF16)
    k2 = np.arange(n2w)
    c2, s2 = _cos_sin((n1w * k2[None, :, None] + k1[:, None, None]) * k2[None, None, :], n)
    g = bf(np.concatenate([np.concatenate([c2, s2], axis=2), np.concatenate([-s2, c2], axis=2)], axis=1))
    c2t, s2t = np.swapaxes(c2, 1, 2), np.swapaxes(s2, 1, 2)
    gh = bf(np.concatenate([np.concatenate([c2t, -s2t], axis=2), np.concatenate([s2t, c2t], axis=2)], axis=1))

    ak = _dft_stage1([np.concatenate([c1, -s1], axis=0)], [k_raw.reshape(1, n1w, n2w, c)], [lambda p: 0], 1)
    kspec = pl.pallas_call(
        functools.partial(_spec_kernel, scale=1.0 / n),
        grid=(n1w // kb,),
        in_specs=[pl.BlockSpec((1, 2, kb, n2w, c), lambda i: (0, 0, i, 0, 0)),
                  pl.BlockSpec((kb, 2 * n2w, 2 * n2w), lambda i: (i, 0, 0)),
                  pl.BlockSpec((1, c), lambda i: (0, 0))],
        out_specs=pl.BlockSpec((2, kb, n2w, c), lambda i: (0, i, 0, 0)),
        out_shape=jax.ShapeDtypeStruct((2, n1w, n2w, c), F32),
        compiler_params=_cparams("parallel"),
    )(ak, g, k_ss)

    ch, sh = c1[:, :half], s1[:, :half]
    view = lambda a: a.reshape(b, half, n2w, c)
    a = _dft_stage1([np.concatenate([ch, -sh], axis=0), np.concatenate([sh, ch], axis=0)],
                    [view(uu), view(uu)], [lambda p: 2 * p, lambda p: 2 * p + 1], b // 2)
    z = pl.pallas_call(
        _hy_mid_kernel,
        grid=(n1w // kb, b // 2),
        in_specs=[pl.BlockSpec((1, 2, kb, n2w, c), lambda i, p: (p, 0, i, 0, 0)),
                  pl.BlockSpec((kb, 2 * n2w, 2 * n2w), lambda i, p: (i, 0, 0)),
                  pl.BlockSpec((kb, 2 * n2w, 2 * n2w), lambda i, p: (i, 0, 0)),
                  pl.BlockSpec((2, kb, n2w, c), lambda i, p: (0, i, 0, 0))],
        out_specs=pl.BlockSpec((1, 2, kb, n2w, c), lambda i, p: (p, 0, i, 0, 0)),
        out_shape=jax.ShapeDtypeStruct((b // 2, 2, n1w, n2w, c), F32),
        compiler_params=_cparams("parallel", "parallel"),
    )(a, g, gh, kspec)
    cht, sht = ch.T, sh.T
    t_inv = _per_sublane(
        np.concatenate([np.concatenate([cht, -sht], axis=1), np.concatenate([sht, cht], axis=1)], axis=0))
    pair = pl.BlockSpec((2, half, SUBLANES, c), lambda p, j: (p, 0, j, 0))
    out = pl.pallas_call(
        _hy_out_kernel,
        grid=(b // 2, n2w // SUBLANES),
        in_specs=[pl.BlockSpec((1, 2, n1w, SUBLANES, c), lambda p, j: (p, 0, 0, j, 0)),
                  pl.BlockSpec(t_inv.shape, lambda p, j: (0, 0)),
                  pair, pair, pl.BlockSpec((1, c), lambda p, j: (0, 0))],
        out_specs=pair,
        out_shape=jax.ShapeDtypeStruct((b, half, n2w, c), F32),
        compiler_params=_cparams("parallel", "parallel"),
    )(z, t_inv, view(uu), view(x0), bias.reshape(1, c))
    return out.reshape(b, l, c)


def _mixer_residual(x_ref, part_refs, wo_ref, gate_ref):
    y, row = None, 0
    for p_ref in part_refs:
        n = p_ref.shape[2]
        t = _dot(p_ref[0].astype(BF16), wo_ref[row:row + n])
        y = t if y is None else y + t
        row += n
    return x_ref[0] + gate_ref[0] * y


def _proj_res_kernel(x_ref, a_ref, wo_ref, gate_ref, o_ref):
    o_ref[0] = _mixer_residual(x_ref, [a_ref], wo_ref, gate_ref)


def _proj_residual(x, a, w_out, gate, tm=512):
    b, l, d = x.shape
    tm = _tile(l, tm)
    tok = lambda n: pl.BlockSpec((1, tm, n), lambda bi, i: (bi, i, 0))
    return pl.pallas_call(
        _proj_res_kernel,
        grid=(b, l // tm),
        in_specs=[tok(d), tok(a.shape[2]), pl.BlockSpec(w_out.shape, lambda bi, i: (0, 0)),
                  pl.BlockSpec((1, 1, d), lambda bi, i: (bi, 0, 0))],
        out_specs=tok(d),
        out_shape=jax.ShapeDtypeStruct((b, l, d), F32),
        compiler_params=_cparams("parallel", "parallel"),
    )(x, a, w_out, gate)


def _ffn_kernel(*refs, n_parts):
    x_ref, part_refs = refs[0], refs[1:1 + n_parts]
    wo_ref, gate1_ref, g_ref, sh_ref, sc_ref, gate_ref, wg_ref, wu_ref, wd_ref, o_ref = refs[1 + n_parts:]
    x = _mixer_residual(x_ref, part_refs, wo_ref, gate1_ref)
    h = _norm_mod(x, g_ref[...], sh_ref[0], sc_ref[0]).astype(BF16)
    mid = _silu(_dot(h, wg_ref[...])) * _dot(h, wu_ref[...])
    o_ref[0] = x + gate_ref[0] * _dot(mid.astype(BF16), wd_ref[...])


def _mixer_out_ffn(x, parts, w_out, gate1, g, shift, scale, gate, wg, wu, wd, tm=512):
    b, l, d = x.shape
    tm = _tile(l, tm)
    vec = pl.BlockSpec((1, 1, d), lambda bi, i: (bi, 0, 0))
    tok = lambda n: pl.BlockSpec((1, tm, n), lambda bi, i: (bi, i, 0))
    resident = lambda w: pl.BlockSpec(w.shape, lambda bi, i: (0, 0), pipeline_mode=pl.Buffered(1))
    return pl.pallas_call(
        functools.partial(_ffn_kernel, n_parts=len(parts)),
        grid=(b, l // tm),
        in_specs=[tok(d)] + [tok(p.shape[2]) for p in parts]
        + [resident(w_out), vec, pl.BlockSpec((1, d), lambda bi, i: (0, 0)), vec, vec, vec,
           resident(wg), resident(wu), resident(wd)],
        out_specs=tok(d),
        out_shape=jax.ShapeDtypeStruct((b, l, d), F32),
        compiler_params=_cparams("parallel", "parallel"),
    )(x, *parts, w_out, gate1, g.reshape(1, d), shift, scale, gate, wg, wu, wd)


def _head_norm(t, e, et, g_full):
    ss = _dot((t * t).astype(BF16), e)
    rinv = lax.rsqrt(ss * (1.0 / HEAD_DIM) + NORM_EPS)
    hi, lo = _split(rinv)
    return t * (_dot(hi, et) + _dot(lo, et)) * g_full


def _rope(t, cos, sin_lo, sin_hi):
    w = t.shape[1]
    rep = w // LANES
    tile = lambda a: jnp.concatenate([a] * rep, axis=1)
    return (t * tile(cos) + pltpu.roll(t, w - ROPE_HALF, 1) * tile(sin_lo)
            + pltpu.roll(t, ROPE_HALF, 1) * tile(sin_hi))


def _qkv_kernel(x_ref, g_ref, sh_ref, sc_ref, w_ref, e_ref, et_ref, qg_ref, kg_ref, cos_ref, sl_ref, sh2_ref,
                *out_refs, qw, kw, rope):
    k_ref, v_ref = out_refs[-2:]
    h = _norm_mod(x_ref[0], g_ref[...], sh_ref[0], sc_ref[0])
    t = _dot(h.astype(BF16), w_ref[...])
    e, et = e_ref[...], et_ref[...]
    k = _head_norm(t[:, qw:qw + kw], e[:kw], et[:, :kw], kg_ref[...])
    if rope:
        k = _rope(k, cos_ref[...], sl_ref[...], sh2_ref[...])
    k_ref[0] = k.astype(BF16)
    v_ref[0] = t[:, qw + kw:].astype(BF16)
    if qw:
        q = _head_norm(t[:, :qw], e, et, qg_ref[...])
        q = _rope(q, cos_ref[...], sl_ref[...], sh2_ref[...]) * (HEAD_DIM ** -0.5)
        out_refs[0][0] = q.astype(BF16)


def _head_tables(qw):
    lane = np.arange(qw)
    e = (lane[:, None] // HEAD_DIM == np.arange(LANES)[None, :]).astype(BF16)
    return e, np.ascontiguousarray(e.T)


def _rope_tables(l):
    rows = l // GRID_W
    row = np.repeat(np.arange(rows, dtype=np.float32), GRID_W)
    col = np.tile(np.arange(GRID_W, dtype=np.float32), rows)
    inv = (ROPE_THETA ** (-np.arange(0, AXIS_ROPE_DIM, 2, dtype=np.float32) / AXIS_ROPE_DIM)).astype(np.float32)
    lane = np.arange(LANES)
    in_head = lane % HEAD_DIM
    use_col = (in_head // AXIS_ROPE_DIM) == 1
    hi_half = ((in_head % AXIS_ROPE_DIM) // ROPE_HALF) == 1
    freq = inv[in_head % ROPE_HALF]
    ang = (np.where(use_col[None, :], col[:, None], row[:, None]) * freq[None, :]).astype(np.float32)
    cos, sin = np.cos(ang), np.sin(ang)
    zero = np.float32(0.0)
    return cos, np.where(hi_half[None, :], zero, -sin), np.where(hi_half[None, :], sin, zero)


def _qkv_project(x, g, shift, scale, w, q_g, k_g, qw, kw, rope, tm=512):
    b, l, d = x.shape
    tm = _tile(l, tm)
    e, et = _head_tables(max(qw, kw))
    n_q = max(qw, kw) // HEAD_DIM
    qg = jnp.tile(q_g, n_q).reshape(1, -1)
    kg = jnp.tile(k_g, kw // HEAD_DIM).reshape(1, kw)
    if rope:
        cos, s_lo, s_hi = _rope_tables(l)
    else:
        cos = s_lo = s_hi = jnp.zeros((l, LANES), F32)
    vec = pl.BlockSpec((1, 1, d), lambda bi, i: (bi, 0, 0))
    full = lambda a: pl.BlockSpec(a.shape, lambda bi, i: (0,) * a.ndim)
    tab = pl.BlockSpec((tm, LANES), lambda bi, i: (i, 0))
    widths = ([qw] if qw else []) + [kw, kw]
    return pl.pallas_call(
        functools.partial(_qkv_kernel, qw=qw, kw=kw, rope=rope),
        grid=(b, l // tm),
        in_specs=[pl.BlockSpec((1, tm, d), lambda bi, i: (bi, i, 0)),
                  pl.BlockSpec((1, d), lambda bi, i: (0, 0)), vec, vec,
                  full(w), full(e), full(et), full(qg), full(kg), tab, tab, tab],
        out_specs=[pl.BlockSpec((1, tm, n), lambda bi, i: (bi, i, 0)) for n in widths],
        out_shape=[jax.ShapeDtypeStruct((b, l, n), BF16) for n in widths],
        compiler_params=_cparams("parallel", "parallel"),
    )(x, g.reshape(1, d), shift, scale, w, e, et, qg, kg, cos, s_lo, s_hi)


def _attn_kernel(q_ref, kp_ref, kc_ref, kn_ref, vp_ref, vc_ref, vn_ref, kx_ref, vx_ref, sink_ref, o_ref, *,
                 seq, group):
    qb = pl.program_id(1)
    bq = q_ref.shape[1]
    lc = kx_ref.shape[1]
    nkeys = 3 * bq + lc
    rows = lax.broadcasted_iota(I32, (group * bq, nkeys), 0)
    cols = lax.broadcasted_iota(I32, (group * bq, nkeys), 1)
    qpos = qb * bq + rows % bq
    kpos = (qb - 1) * bq + cols
    valid = jnp.logical_or(
        cols >= 3 * bq,
        jnp.logical_and(jnp.logical_and(kpos >= 0, kpos < seq), jnp.abs(qpos - kpos) <= WINDOW))
    for h in range(N_KV_HEADS):
        ks = slice(h * HEAD_DIM, (h + 1) * HEAD_DIM)
        kh = jnp.concatenate([kp_ref[0, :, ks], kc_ref[0, :, ks], kn_ref[0, :, ks], kx_ref[0, :, ks]], axis=0)
        vh = jnp.concatenate([vp_ref[0, :, ks], vc_ref[0, :, ks], vn_ref[0, :, ks], vx_ref[0, :, ks]], axis=0)
        qh = jnp.concatenate(
            [q_ref[0, :, (h * group + g) * HEAD_DIM:(h * group + g + 1) * HEAD_DIM] for g in range(group)], axis=0)
        s = lax.dot_general(qh, kh, (((1,), (1,)), ((), ())), preferred_element_type=F32)
        s = jnp.where(valid, s, NEG_INF)
        sk = sink_ref[h]
        m = jnp.maximum(jnp.max(s, axis=1, keepdims=True), sk)
        p = jnp.exp(s - m)
        den = jnp.sum(p, axis=1, keepdims=True) + jnp.exp(sk - m)
        o = _dot(p.astype(BF16), vh) / den
        for g in range(group):
            hq = h * group + g
            o_ref[0, :, hq * HEAD_DIM:(hq + 1) * HEAD_DIM] = o[g * bq:(g + 1) * bq].astype(o_ref.dtype)


def _window_attention(q, k, v, kx, vx, sink):
    b, l, qw = q.shape
    kw = k.shape[2]
    lc = kx.shape[1]
    bq = BLOCK_Q
    nb = l // bq
    group = qw // kw
    sink_tab = jnp.repeat(sink.astype(F32).reshape(N_KV_HEADS, group), bq, axis=1)[..., None]
    kv_prev = pl.BlockSpec((1, bq, kw), lambda bi, i: (bi, jnp.maximum(i - 1, 0), 0))
    kv_cur = pl.BlockSpec((1, bq, kw), lambda bi, i: (bi, i, 0))
    kv_next = pl.BlockSpec((1, bq, kw), lambda bi, i: (bi, jnp.minimum(i + 1, nb - 1), 0))
    kv_ctx = pl.BlockSpec((1, lc, kw), lambda bi, i: (bi, 0, 0))
    return pl.pallas_call(
        functools.partial(_attn_kernel, seq=l, group=group),
        grid=(b, nb),
        in_specs=[pl.BlockSpec((1, bq, qw), lambda bi, i: (bi, i, 0)),
                  kv_prev, kv_cur, kv_next, kv_prev, kv_cur, kv_next, kv_ctx, kv_ctx,
                  pl.BlockSpec(sink_tab.shape, lambda bi, i: (0, 0, 0))],
        out_specs=pl.BlockSpec((1, bq, qw), lambda bi, i: (bi, i, 0)),
        out_shape=jax.ShapeDtypeStruct((b, l, qw), BF16),
        compiler_params=_cparams("parallel", "parallel"),
    )(q, k, k, k, v, v, v, kx, vx, sink_tab)


def _router_kernel(x_ref, g_ref, sh_ref, sc_ref, wh_ref, wl_ref, tri_ref, h_ref, gate_ref, rank_ref, rankt_ref,
                   cnt_ref):
    h = _norm_mod(x_ref[0], g_ref[...], sh_ref[0], sc_ref[0])
    h_ref[0] = h.astype(BF16)
    hi, lo = _split(h)
    logits = _dot(hi, wh_ref[...]) + _dot(hi, wl_ref[...]) + _dot(lo, wh_ref[...])
    lane = lax.broadcasted_iota(I32, logits.shape, 1)
    logits = jnp.where(lane < N_EXPERTS, logits, -jnp.inf)
    m1 = jnp.max(logits, axis=1, keepdims=True)
    i1 = jnp.min(jnp.where(logits == m1, lane, LANES), axis=1, keepdims=True)
    rest = jnp.where(lane == i1, -jnp.inf, logits)
    m2 = jnp.max(rest, axis=1, keepdims=True)
    i2 = jnp.min(jnp.where(rest == m2, lane, LANES), axis=1, keepdims=True)
    e = jnp.exp(m2 - m1)
    g1 = 1.0 / (1.0 + e)
    g2 = e / (1.0 + e)
    pick1, pick2 = lane == i1, lane == i2
    member = jnp.logical_or(pick1, pick2)
    gate_ref[0] = jnp.where(pick1, g1, jnp.where(pick2, g2, 0.0))
    m = jnp.where(member, 1.0, 0.0)
    rank = jnp.where(member, _dot(tri_ref[...], m.astype(BF16)), -1.0)
    rank_ref[0] = rank.astype(I32)
    rankt_ref[0] = jnp.transpose(rank)[:SUBLANES].astype(I32)
    cnt_ref[0] = jnp.sum(m, axis=0, keepdims=True).astype(I32)


def _route(x, g, shift, scale, w_router):
    b, l, d = x.shape
    tm = _tile(l, MOE_TILE)
    nt = l // tm
    wr = jnp.concatenate([w_router, jnp.zeros((d, LANES - w_router.shape[1]), F32)], axis=1)
    wh, wl = _split(wr)
    tri = np.tril(np.ones((tm, tm), np.float32), -1).astype(BF16)
    vec = pl.BlockSpec((1, 1, d), lambda bi, i: (bi, 0, 0))
    tok = lambda n: pl.BlockSpec((1, tm, n), lambda bi, i: (bi, i, 0))
    return pl.pallas_call(
        _router_kernel,
        grid=(b, nt),
        in_specs=[tok(d), pl.BlockSpec((1, d), lambda bi, i: (0, 0)), vec, vec,
                  pl.BlockSpec((d, LANES), lambda bi, i: (0, 0)),
                  pl.BlockSpec((d, LANES), lambda bi, i: (0, 0)),
                  pl.BlockSpec((tm, tm), lambda bi, i: (0, 0))],
        out_specs=[tok(d), tok(LANES), tok(LANES),
                   pl.BlockSpec((1, SUBLANES, tm), lambda bi, i: (bi * nt + i, 0, 0)),
                   pl.BlockSpec((1, 1, LANES), lambda bi, i: (bi * nt + i, 0, 0))],
        out_shape=[jax.ShapeDtypeStruct((b, l, d), BF16),
                   jax.ShapeDtypeStruct((b, l, LANES), F32),
                   jax.ShapeDtypeStruct((b, l, LANES), I32),
                   jax.ShapeDtypeStruct((b * nt, SUBLANES, tm), I32),
                   jax.ShapeDtypeStruct((b * nt, 1, LANES), I32)],
        compiler_params=_cparams("parallel", "parallel"),
    )(x, g.reshape(1, d), shift, scale, wh, wl, tri)


def _slot_layout(cnt, rows):
    nt = cnt.shape[0]
    seg = (cnt + SUBLANES - 1) // SUBLANES * SUBLANES
    padded = (jnp.sum(seg, axis=0) + rows - 1) // rows * rows
    stride = padded + rows
    pstart = jnp.cumsum(stride) - stride
    off = pstart[None, :] + jnp.cumsum(seg, axis=0) - seg
    n_slots = (nt * N_EXPERTS * (SUBLANES - 1) + nt * MOE_TILE * TOP_K + rows - 1) // rows * rows \
        + 2 * N_EXPERTS * rows
    nblk = n_slots // rows
    bstart = jnp.arange(nblk, dtype=I32) * rows
    be = jnp.minimum(jnp.sum((bstart[:, None] >= (pstart + stride)[None, :]).astype(I32), axis=1), N_EXPERTS - 1)
    onehot = (be[:, None] == jnp.arange(N_EXPERTS, dtype=I32)[None, :]).astype(I32)
    lo = jnp.sum(onehot * pstart[None, :], axis=1)
    hi = jnp.sum(onehot * (pstart + padded)[None, :], axis=1)
    used = jnp.logical_and(bstart >= lo, bstart < hi)
    zero_blk = jnp.logical_or(jnp.logical_not(used), bstart == hi - rows)
    as_i32 = lambda a: a.reshape(-1).astype(I32)
    return n_slots, as_i32(off), as_i32(cnt), as_i32(zero_blk), as_i32(be), as_i32(used)


def _seg_copy(src_ref, dst_ref, row, sem):
    n = src_ref.shape[0]
    return pltpu.make_async_copy(src_ref, dst_ref.at[pl.ds(pl.multiple_of(row, SUBLANES), n)], sem)


def _chunk_copy(src_ref, dst_ref, row, sem):
    n = dst_ref.shape[0]
    return pltpu.make_async_copy(src_ref.at[pl.ds(pl.multiple_of(row, SUBLANES), n)], dst_ref, sem)


def _dispatch_kernel(off_ref, cnt_ref, zero_ref, h_ref, rt_ref, xs_ref, xbuf, xbuf2, sem, sem2, *, rows, n_tiles):
    i = pl.program_id(0)
    cap = xbuf.shape[2]
    tm = h_ref.shape[1]
    slot = i % 2

    @pl.when(i == 0)
    def _():
        xbuf[0, 0] = jnp.zeros(xbuf.shape[2:], F32)

        def zero_block(j, carry):
            @pl.when(zero_ref[j] > 0)
            def _():
                for part in range(rows // cap):
                    c = _seg_copy(xbuf.at[0, 0], xs_ref, j * rows + part * cap, sem2)
                    c.start()
                    c.wait()
            return carry

        lax.fori_loop(0, zero_ref.shape[0], zero_block, 0)

    h = h_ref[0]
    rt = rt_ref[0]
    riota = lax.broadcasted_iota(I32, (cap, tm), 0)

    def copy(t, s, e):
        return _seg_copy(xbuf.at[s, e], xs_ref, off_ref[t * N_EXPERTS + e], sem.at[s, e])

    for e in range(N_EXPERTS):
        sel = rt[e:e + 1, :]

        def segment(base):
            return _dot(jnp.where(sel == riota + base, 1.0, 0.0).astype(BF16), h)

        @pl.when(i > 0)
        def _():
            copy(i - 1, 1 - slot, e).wait()

        xbuf[slot, e] = segment(0)
        copy(i, slot, e).start()

        @pl.when(cnt_ref[i * N_EXPERTS + e] > cap)
        def _():
            xbuf2[...] = segment(cap)
            c = _seg_copy(xbuf2, xs_ref, off_ref[i * N_EXPERTS + e] + cap, sem2)
            c.start()
            c.wait()

    @pl.when(i == n_tiles - 1)
    def _():
        for e in range(N_EXPERTS):
            copy(i, slot, e).wait()


def _dispatch(h, rankt, n_slots, off, cnt, zero_blk, rows):
    b, l, d = h.shape
    tm = _tile(l, MOE_TILE)
    nt = l // tm
    cap = MOE_CAP
    assert tm <= 2 * cap and rows % cap == 0
    grid_spec = pltpu.PrefetchScalarGridSpec(
        num_scalar_prefetch=3,
        grid=(b * nt,),
        in_specs=[pl.BlockSpec((1, tm, d), lambda i, *_: (i // nt, i % nt, 0)),
                  pl.BlockSpec((1, SUBLANES, tm), lambda i, *_: (i, 0, 0))],
        out_specs=pl.BlockSpec(memory_space=pl.ANY),
        scratch_shapes=[pltpu.VMEM((2, N_EXPERTS, cap, d), F32), pltpu.VMEM((cap, d), F32),
                        pltpu.SemaphoreType.DMA((2, N_EXPERTS)), pltpu.SemaphoreType.DMA(())],
    )
    return pl.pallas_call(
        functools.partial(_dispatch_kernel, rows=rows, n_tiles=b * nt),
        grid_spec=grid_spec,
        out_shape=jax.ShapeDtypeStruct((n_slots, d), F32),
        compiler_params=_cparams("arbitrary"),
    )(off, cnt, zero_blk, h, rankt)


def _moe_kernel(be_ref, used_ref, x_ref, wg_ref, wu_ref, wd_ref, o_ref):
    i = pl.program_id(0)

    @pl.when(used_ref[i] > 0)
    def _():
        x = x_ref[...].astype(BF16)
        mid = _silu(_dot(x, wg_ref[0])) * _dot(x, wu_ref[0])
        o_ref[...] = _dot(mid.astype(BF16), wd_ref[0])

    @pl.when(used_ref[i] == 0)
    def _():
        o_ref[...] = jnp.zeros_like(o_ref)


def _expert_ffn(xs, block_e, used, wg, wu, wd, rows):
    s, d = xs.shape
    nblk = s // rows
    expert = lambda w: pl.BlockSpec((1,) + w.shape[1:], lambda i, be, us: (be[i], 0, 0),
                                    pipeline_mode=pl.Buffered(1))
    grid_spec = pltpu.PrefetchScalarGridSpec(
        num_scalar_prefetch=2,
        grid=(nblk,),
        in_specs=[pl.BlockSpec((rows, d), lambda i, be, us: (i, 0)), expert(wg), expert(wu), expert(wd)],
        out_specs=pl.BlockSpec((rows, d), lambda i, be, us: (i, 0)),
    )
    return pl.pallas_call(
        _moe_kernel,
        grid_spec=grid_spec,
        out_shape=jax.ShapeDtypeStruct((s, d), F32),
        compiler_params=_cparams("arbitrary"),
    )(block_e, used, xs, wg, wu, wd)


def _combine_kernel(off_ref, cnt_ref, x_ref, gate_ref, rg_ref, rank_ref, ys_ref, o_ref, buf, buf2, acc_ref, sem,
                    sem2, *, n_tiles):
    i = pl.program_id(0)
    slot = i % 2
    cap = buf.shape[2]
    tm = x_ref.shape[1]

    def chunk(t, s, e):
        return _chunk_copy(ys_ref, buf.at[s, e], off_ref[t * N_EXPERTS + e], sem.at[s, e])

    @pl.when(i == 0)
    def _():
        for e in range(N_EXPERTS):
            chunk(0, 0, e).start()

    @pl.when(i + 1 < n_tiles)
    def _():
        for e in range(N_EXPERTS):
            chunk(i + 1, 1 - slot, e).start()

    rank = rank_ref[0]
    rg = rg_ref[0]
    liota = lax.broadcasted_iota(I32, (tm, cap), 1)

    def picked(e, base, rows_ref):
        q = jnp.where(rank[:, e:e + 1] == liota + base, 1.0, 0.0).astype(BF16)
        return rg[:, e:e + 1] * _dot(q, rows_ref[...].astype(BF16))

    y = jnp.zeros(acc_ref.shape, F32)
    for e in range(N_EXPERTS):
        chunk(i, slot, e).wait()
        y = y + picked(e, 0, buf.at[slot, e])
    acc_ref[...] = y

    for e in range(N_EXPERTS):
        @pl.when(cnt_ref[i * N_EXPERTS + e] > cap)
        def _():
            c = _chunk_copy(ys_ref, buf2, off_ref[i * N_EXPERTS + e] + cap, sem2)
            c.start()
            c.wait()
            acc_ref[...] += picked(e, cap, buf2)

    o_ref[0] = x_ref[0] + gate_ref[0] * acc_ref[...]


def _moe_combine(x, gate, route_gates, rank, ys, off, cnt):
    b, l, d = x.shape
    tm = _tile(l, MOE_TILE)
    nt = l // tm
    cap = MOE_CAP
    tok = lambda n: pl.BlockSpec((1, tm, n), lambda i, *_: (i // nt, i % nt, 0))
    grid_spec = pltpu.PrefetchScalarGridSpec(
        num_scalar_prefetch=2,
        grid=(b * nt,),
        in_specs=[tok(d), pl.BlockSpec((1, 1, d), lambda i, *_: (i // nt, 0, 0)), tok(LANES), tok(LANES),
                  pl.BlockSpec(memory_space=pl.ANY)],
        out_specs=tok(d),
        scratch_shapes=[pltpu.VMEM((2, N_EXPERTS, cap, d), F32), pltpu.VMEM((cap, d), F32),
                        pltpu.VMEM((tm, d), F32),
                        pltpu.SemaphoreType.DMA((2, N_EXPERTS)), pltpu.SemaphoreType.DMA(())],
    )
    return pl.pallas_call(
        functools.partial(_combine_kernel, n_tiles=b * nt),
        grid_spec=grid_spec,
        out_shape=jax.ShapeDtypeStruct((b, l, d), F32),
        compiler_params=_cparams("arbitrary"),
    )(off, cnt, x, gate, route_gates, rank, ys)


def _angles(rows, cols, n):
    m = (rows[:, None] * cols[None, :]) % n
    return (2.0 * math.pi / n) * m.astype(F32)


def _fnet_table(l):
    idx = jnp.arange(l, dtype=I32)
    ang = _angles(idx, idx, l)
    s = 1.0 / math.sqrt(l)
    return jnp.concatenate([jnp.cos(ang) * s, jnp.sin(ang) * (-s)], axis=1).astype(BF16)


def _rfft_table(n):
    half = n // 2
    f = jnp.arange(half, dtype=I32)
    t = jnp.arange(n, dtype=I32)
    ang = _angles(f, t, n)
    top = jnp.cos(ang)
    bot = -jnp.sin(ang)
    nyq = jnp.cos(_angles(jnp.full((1,), half, I32), t, n))
    bot = jnp.concatenate([nyq, bot[1:]], axis=0)
    return jnp.concatenate([top, bot], axis=0).astype(BF16)


def _group_dft_table(width):
    gd = FNET_GROUP_DIM
    idx = jnp.arange(gd, dtype=I32)
    ang = _angles(idx, idx, gd)
    s = 1.0 / math.sqrt(gd)
    eye = jnp.eye(width // gd, dtype=F32)
    return jnp.concatenate([jnp.kron(eye, jnp.cos(ang) * s), jnp.kron(eye, jnp.sin(ang) * s)], axis=1)


def _even_mixer(x, g1, shift, scale, w_in_f, conv_w, conv_b, hy_bias, filt, fw, hw):
    b, l, d = x.shape
    n = 2 * l
    long_seq = l >= LONG_SEQ
    seq_dtype = F32 if long_seq else BF16
    u, y_cos, y_sin = _norm_mod_matmul(x, g1, shift, scale, w_in_f, [3 * hw, fw, fw], [BF16, seq_dtype, seq_dtype])
    uu, x0 = _hyena_pre(u, conv_w, conv_b, hw, seq_dtype)
    k_raw, k_ss = filt(l)
    if long_seq:
        a = _fnet_long(y_cos, y_sin)
        hy = _hyena_long(uu, x0, hy_bias, k_raw, k_ss)
    else:
        y_both = jnp.concatenate([y_cos, y_sin], axis=2)
        a = _left_dft(_fnet_table(l), y_both, lambda bi, k, tk: (bi, k % (l // tk), k // (l // tk)), n, fw, BF16,
                      tk=min(l, 1024))
        wf = _rfft_table(n)
        wts = jnp.concatenate([jnp.ones((1,), F32), jnp.full((l - 1,), 2.0, F32)]) / n
        row_scale = jnp.concatenate([wts, wts]).reshape(n, 1)
        kspec = _left_dft(wf, k_raw[None], lambda bi, k, tk: (0, k, 0), n, hw, F32,
                          row_scale=row_scale, col_sumsq=k_ss)
        uhat = _left_dft(wf, uu, lambda bi, k, tk: (bi, k, 0), l, hw, F32)
        hy = _hyena_inverse(wf[:, :l].T, uhat, kspec, uu, x0, hy_bias)
    return a, hy


def kernel(x, c, ctx, c_ctx, ada_w, ada_b, norm1_g, norm2_g, ev_w_in, ev_w_out, hy_conv_w, hy_conv_b, hy_bias,
           hf_w0, hf_b0, hf_w1, hf_b1, hf_w2, hf_b2, hf_w3, hf_freq, ffn_w_gate, ffn_w_up, ffn_w_down, od_w_qkv,
           od_w_out, q_norm_g, k_norm_g, attn_sink, moe_router, moe_w_gate, moe_w_up, moe_w_down):
    b, l, d = x.shape
    lc = ctx.shape[1]
    assert ada_w.shape[0] == 2, "this implementation covers the two-layer (even, odd) stack"
    hw = hy_bias.shape[1]
    fw = ev_w_in.shape[2] - 3 * hw
    qw = od_w_out.shape[1]
    kw = (od_w_qkv.shape[2] - qw) // 2

    rows = (b + 1 + 7) // 8 * 8
    cc = jnp.concatenate([c, c_ctx[None, :], jnp.zeros((rows - b - 1, d), F32)], axis=0)
    mod = _ada_vectors(cc, ada_w, ada_b)
    ml = [[mod[i, :b, None, m * d:(m + 1) * d] for m in range(N_MOD)] for i in range(2)]
    mc = [[mod[i, b:b + 1, None, m * d:(m + 1) * d] for m in range(N_MOD)] for i in range(2)]

    w_in = ev_w_in[0]
    w_fnet = _matmul3(w_in[:, :fw], _group_dft_table(fw))
    w_in_f = jnp.concatenate([w_in[:, fw:], w_fnet], axis=1).astype(BF16)
    w_out0 = ev_w_out[0].astype(BF16)
    filt = lambda seq: _hyena_filter(seq, hw, hf_w0[0], hf_b0[0], hf_w1[0], hf_b1[0], hf_w2[0], hf_b2[0],
                                     hf_w3[0], hf_freq[0])
    wg, wu, wd = ffn_w_gate[0].astype(BF16), ffn_w_up[0].astype(BF16), ffn_w_down[0].astype(BF16)

    mix = _even_mixer(x, norm1_g[0], ml[0][0], ml[0][1], w_in_f, hy_conv_w[0], hy_conv_b[0], hy_bias[0], filt, fw,
                      hw)
    x = _mixer_out_ffn(x, mix, w_out0, ml[0][2], norm2_g[0], ml[0][3], ml[0][4], ml[0][5], wg, wu, wd)

    bc = lambda v: jnp.broadcast_to(v, (b, 1, d))
    mix = _even_mixer(ctx, norm1_g[0], bc(mc[0][0]), bc(mc[0][1]), w_in_f, hy_conv_w[0], hy_conv_b[0], hy_bias[0],
                      filt, fw, hw)
    flat = lambda t: t.reshape(1, b * lc, t.shape[2])
    ctx = _mixer_out_ffn(flat(ctx), [flat(t) for t in mix], w_out0, mc[0][2], norm2_g[0], mc[0][3], mc[0][4],
                         mc[0][5], wg, wu, wd)

    w_qkv = od_w_qkv[0].astype(BF16)
    q, k, v = _qkv_project(x, norm1_g[1], ml[1][0], ml[1][1], w_qkv, q_norm_g[0], k_norm_g[0], qw, kw, True)
    kx, vx = _qkv_project(ctx, norm1_g[1], mc[1][0], mc[1][1], w_qkv[:, qw:], q_norm_g[0], k_norm_g[0], 0, kw,
                          False)
    o = _window_attention(q, k, v, kx.reshape(b, lc, kw), vx.reshape(b, lc, kw), attn_sink[0])
    x = _proj_residual(x, o, od_w_out[0].astype(BF16), ml[1][2])

    h2, gates, rank, rankt, cnt = _route(x, norm2_g[1], ml[1][3], ml[1][4], moe_router[0])
    n_slots, off, cnt, zero_blk, block_e, used = _slot_layout(cnt[:, 0, :N_EXPERTS], MOE_ROWS)
    xs = _dispatch(h2, rankt, n_slots, off, cnt, zero_blk, MOE_ROWS)
    ys = _expert_ffn(xs, block_e, used, moe_w_gate[0].astype(BF16), moe_w_up[0].astype(BF16),
                     moe_w_down[0].astype(BF16), MOE_ROWS)
    return _moe_combine(x, ml[1][5], gates, rank, ys, off, cnt)
```

```python
import functools
import math

import jax
import jax.numpy as jnp
import numpy as np
from jax import lax
from jax.experimental import pallas as pl
from jax.experimental.pallas import tpu as pltpu

F32 = jnp.float32
BF16 = jnp.bfloat16
I32 = jnp.int32

NORM_EPS = 1e-6
NEG_INF = -1e30
N_MOD = 6

FNET_GROUP_DIM = 128
HY_EMB_BANDS = 16
HY_FILTER_HIDDEN = 64
HY_DECAY_TARGET = 1e-2
HY_FAST_DECAY = 0.3
HY_SLOW_DECAY = 1.5

HEAD_DIM = 64
N_KV_HEADS = 4
GRID_W = 64
WINDOW = 128
BLOCK_Q = 128
ROPE_THETA = 10000.0
AXIS_ROPE_DIM = HEAD_DIM // 2
ROPE_HALF = AXIS_ROPE_DIM // 2
N_EXPERTS = 8
TOP_K = 2

LANES = 128
SUBLANES = 8
VMEM_LIMIT = 56 * 1024 * 1024
MOE_TILE = 512
MOE_ROWS = 512
MOE_CAP = 256


def _cparams(*sem):
    return pltpu.CompilerParams(dimension_semantics=sem, vmem_limit_bytes=VMEM_LIMIT)


def _split(a):
    hi = a.astype(BF16)
    lo = (a - hi.astype(F32)).astype(BF16)
    return hi, lo


def _dot(a, b):
    return jnp.dot(a, b, preferred_element_type=F32)


def _dot3(a, b):
    ah, al = _split(a)
    bh, bl = _split(b)
    return _dot(ah, bh) + _dot(ah, bl) + _dot(al, bh)


def _silu(t):
    return t / (1.0 + jnp.exp(-t))


def _norm_mod(x, g, shift, scale):
    ms = jnp.mean(x * x, axis=-1, keepdims=True)
    y = x * lax.rsqrt(ms + NORM_EPS) * g
    return y * (1.0 + scale) + shift


def _tile(n, pref):
    t = min(n, pref)
    assert n % t == 0, (n, pref)
    return t


def _ada_kernel(c_ref, w_ref, b_ref, o_ref):
    o_ref[0] = _dot3(_silu(c_ref[...]), w_ref[0]) + b_ref[0]


def _ada_vectors(cc, ada_w, ada_b):
    depth, d, n = ada_w.shape
    rows = cc.shape[0]
    tn = _tile(n, 1536)
    return pl.pallas_call(
        _ada_kernel,
        grid=(depth, n // tn),
        in_specs=[pl.BlockSpec((rows, d), lambda l, j: (0, 0)),
                  pl.BlockSpec((1, d, tn), lambda l, j: (l, 0, j)),
                  pl.BlockSpec((1, 1, tn), lambda l, j: (l, 0, j))],
        out_specs=pl.BlockSpec((1, rows, tn), lambda l, j: (l, 0, j)),
        out_shape=jax.ShapeDtypeStruct((depth, rows, n), F32),
        compiler_params=_cparams("arbitrary", "arbitrary"),
    )(cc, ada_w, ada_b.reshape(depth, 1, n))


def _mm3_kernel(a_ref, b_ref, o_ref):
    o_ref[...] = _dot3(a_ref[...], b_ref[...])


def _matmul3(a, b):
    m, _ = a.shape
    n = b.shape[1]
    return pl.pallas_call(_mm3_kernel, out_shape=jax.ShapeDtypeStruct((m, n), F32),
                          compiler_params=_cparams())(a, b)


def _nmm_kernel(x_ref, g_ref, sh_ref, sc_ref, w_ref, *o_refs):
    h = _norm_mod(x_ref[0], g_ref[...], sh_ref[0], sc_ref[0])
    y = _dot(h.astype(BF16), w_ref[...])
    col = 0
    for o_ref in o_refs:
        n = o_ref.shape[2]
        o_ref[0] = y[:, col:col + n].astype(o_ref.dtype)
        col += n


def _norm_mod_matmul(x, g, shift, scale, w, widths, out_dtypes, tm=512):
    b, l, d = x.shape
    n = w.shape[1]
    assert sum(widths) == n
    tm = _tile(l, tm)
    return pl.pallas_call(
        _nmm_kernel,
        grid=(b, l // tm),
        in_specs=[pl.BlockSpec((1, tm, d), lambda bi, i: (bi, i, 0)),
                  pl.BlockSpec((1, d), lambda bi, i: (0, 0)),
                  pl.BlockSpec((1, 1, d), lambda bi, i: (bi, 0, 0)),
                  pl.BlockSpec((1, 1, d), lambda bi, i: (bi, 0, 0)),
                  pl.BlockSpec((d, n), lambda bi, i: (0, 0))],
        out_specs=[pl.BlockSpec((1, tm, wd), lambda bi, i: (bi, i, 0)) for wd in widths],
        out_shape=[jax.ShapeDtypeStruct((b, l, wd), dt) for wd, dt in zip(widths, out_dtypes)],
        compiler_params=_cparams("parallel", "parallel"),
    )(x, g.reshape(1, d), shift, scale, w)


def _ldft_kernel(w_ref, r_ref, o_ref, acc_ref, *, nk):
    k = pl.program_id(2)

    @pl.when(k == 0)
    def _():
        acc_ref[...] = jnp.zeros_like(acc_ref)

    acc_ref[...] += _dot(w_ref[...], r_ref[0].astype(BF16))

    @pl.when(k == nk - 1)
    def _():
        o_ref[0] = acc_ref[...].astype(o_ref.dtype)


def _ldft_scaled_kernel(w_ref, r_ref, rs_ref, ss_ref, o_ref, acc_ref, *, nk):
    k = pl.program_id(2)

    @pl.when(k == 0)
    def _():
        acc_ref[...] = jnp.zeros_like(acc_ref)

    acc_ref[...] += _dot(w_ref[...], r_ref[0].astype(BF16))

    @pl.when(k == nk - 1)
    def _():
        o_ref[0] = acc_ref[...] * rs_ref[...] * lax.rsqrt(ss_ref[...] + NORM_EPS)


def _left_dft(w, rhs, rhs_map, kdim, c, out_dtype, tm=2048, tk=1024, row_scale=None, col_sumsq=None):
    m = w.shape[0]
    nb = rhs.shape[0]
    tm = _tile(m, tm)
    tk = _tile(kdim, tk)
    nk = kdim // tk
    in_specs = [pl.BlockSpec((tm, tk), lambda b, i, k: (i, k)),
                pl.BlockSpec((1, tk, c), lambda b, i, k: rhs_map(b, k, tk))]
    args = [w, rhs]
    if row_scale is None:
        body = functools.partial(_ldft_kernel, nk=nk)
    else:
        body = functools.partial(_ldft_scaled_kernel, nk=nk)
        in_specs += [pl.BlockSpec((tm, 1), lambda b, i, k: (i, 0)),
                     pl.BlockSpec((1, c), lambda b, i, k: (0, 0))]
        args += [row_scale, col_sumsq]
    return pl.pallas_call(
        body,
        grid=(nb, m // tm, nk),
        in_specs=in_specs,
        out_specs=pl.BlockSpec((1, tm, c), lambda b, i, k: (b, i, 0)),
        out_shape=jax.ShapeDtypeStruct((nb, m, c), out_dtype),
        scratch_shapes=[pltpu.VMEM((tm, c), F32)],
        compiler_params=_cparams("parallel", "parallel", "arbitrary"),
    )(*args)


def _hy_pre_kernel(u_ref, up_ref, un_ref, cw_ref, cb_ref, uo_ref, x0_ref, *, nl, tl, hw):
    i = pl.program_id(1)
    u = u_ref[0].astype(F32)
    halo = up_ref.shape[1]
    prev = up_ref[0].astype(F32)[halo - 1:halo]
    nxt = un_ref[0].astype(F32)[0:1]
    prev = jnp.where(i == 0, 0.0, prev)
    nxt = jnp.where(i == nl - 1, 0.0, nxt)
    rows = lax.broadcasted_iota(I32, u.shape, 0)
    um = jnp.where(rows == 0, prev, pltpu.roll(u, 1, 0))
    up = jnp.where(rows == tl - 1, nxt, pltpu.roll(u, tl - 1, 0))
    cw = cw_ref[...]
    z = um * cw[0:1] + u * cw[1:2] + up * cw[2:3] + cb_ref[...]
    uo_ref[0] = (z[:, :hw] * z[:, hw:2 * hw]).astype(uo_ref.dtype)
    x0_ref[0] = z[:, 2 * hw:].astype(x0_ref.dtype)


def _hyena_pre(u, conv_w, conv_b, hw, out_dtype, tl=512):
    b, l, _ = u.shape
    tl = _tile(l, tl)
    nl = l // tl
    halo = 16
    hb = tl // halo
    nh = l // halo
    w3 = 3 * hw
    return pl.pallas_call(
        functools.partial(_hy_pre_kernel, nl=nl, tl=tl, hw=hw),
        grid=(b, nl),
        in_specs=[pl.BlockSpec((1, tl, w3), lambda bi, i: (bi, i, 0)),
                  pl.BlockSpec((1, halo, w3), lambda bi, i: (bi, jnp.maximum(i * hb - 1, 0), 0)),
                  pl.BlockSpec((1, halo, w3), lambda bi, i: (bi, jnp.minimum((i + 1) * hb, nh - 1), 0)),
                  pl.BlockSpec((3, w3), lambda bi, i: (0, 0)),
                  pl.BlockSpec((1, w3), lambda bi, i: (0, 0))],
        out_specs=[pl.BlockSpec((1, tl, hw), lambda bi, i: (bi, i, 0)),
                   pl.BlockSpec((1, tl, hw), lambda bi, i: (bi, i, 0))],
        out_shape=[jax.ShapeDtypeStruct((b, l, hw), out_dtype), jax.ShapeDtypeStruct((b, l, hw), out_dtype)],
        compiler_params=_cparams("parallel", "parallel"),
    )(u, u, u, conv_w, conv_b.reshape(1, w3))


def _hy_filter_kernel(z_ref, aux_ref, w0_ref, b0_ref, w1_ref, b1_ref, w2_ref, b2_ref, w3_ref, fr_ref, dl_ref,
                      k_ref, ss_ref, *, hw):
    i = pl.program_id(0)
    fr = fr_ref[...]
    h = jnp.sin(fr * (_dot3(z_ref[...], w0_ref[...]) + b0_ref[...]))
    h = jnp.sin(fr * (_dot3(h, w1_ref[...]) + b1_ref[...]))
    h = jnp.sin(fr * (_dot3(h, w2_ref[...]) + b2_ref[...]))
    h = _dot3(h, w3_ref[...])
    aux = aux_ref[...]
    t, m_fwd, m_bwd = aux[:, 0:1], aux[:, 1:2], aux[:, 2:3]
    k = (h[:, :hw] * m_fwd + h[:, hw:] * m_bwd) * jnp.exp(-t * dl_ref[...])
    k_ref[...] = k

    @pl.when(i == 0)
    def _():
        ss_ref[...] = jnp.zeros_like(ss_ref)

    ss_ref[...] += jnp.sum(k * k, axis=0, keepdims=True)


def _hyena_filter(l, hw, fw0, fb0, fw1, fb1, fw2, fb2, fw3, freq):
    n = 2 * l
    hid = HY_FILTER_HIDDEN
    r = jnp.arange(n)
    pos = jnp.where(r < l, r, n - r).astype(F32)
    t = pos / max(l - 1, 1)
    wv = (2.0 * math.pi / l) * pos
    bands = jnp.linspace(1e-4, HY_EMB_BANDS - 1, HY_EMB_BANDS, dtype=F32)
    ang = wv[:, None] * bands[None, :]
    ztab = jnp.concatenate([t[:, None], jnp.cos(ang), -jnp.sin(ang),
                            jnp.zeros((n, hid - 1 - 2 * HY_EMB_BANDS), F32)], axis=-1)
    aux = jnp.zeros((n, LANES), F32)
    aux = aux.at[:, 0].set(t).at[:, 1].set((r < l).astype(F32)).at[:, 2].set((r > l).astype(F32))
    w0 = jnp.concatenate([fw0, jnp.zeros((hid - fw0.shape[0], hid), F32)], axis=0)
    deltas = jnp.abs(jnp.linspace(math.log(HY_DECAY_TARGET) / HY_SLOW_DECAY,
                                  math.log(HY_DECAY_TARGET) / HY_FAST_DECAY, hw, dtype=F32)).reshape(1, hw)
    tr = _tile(n, 1024)
    full = lambda shape: pl.BlockSpec(shape, lambda i: (0, 0))
    return pl.pallas_call(
        functools.partial(_hy_filter_kernel, hw=hw),
        grid=(n // tr,),
        in_specs=[pl.BlockSpec((tr, hid), lambda i: (i, 0)),
                  pl.BlockSpec((tr, LANES), lambda i: (i, 0)),
                  full((hid, hid)), full((1, hid)), full((hid, hid)), full((1, hid)),
                  full((hid, hid)), full((1, hid)), full((hid, 2 * hw)), full((1, hid)), full((1, hw))],
        out_specs=[pl.BlockSpec((tr, hw), lambda i: (i, 0)), full((1, hw))],
        out_shape=[jax.ShapeDtypeStruct((n, hw), F32), jax.ShapeDtypeStruct((1, hw), F32)],
        compiler_params=_cparams("arbitrary"),
    )(ztab, aux, w0, fb0.reshape(1, hid), fw1, fb1.reshape(1, hid), fw2, fb2.reshape(1, hid), fw3,
      freq.reshape(1, hid), deltas)


def _hy_inv_kernel(wr_ref, wi_ref, ur_ref, ui_ref, kr_ref, ki_ref, u_ref, x0_ref, bias_ref, o_ref, acc_ref, *, nf):
    f = pl.program_id(2)

    @pl.when(f == 0)
    def _():
        acc_ref[...] = jnp.zeros_like(acc_ref)

    ur, ui, kr, ki = ur_ref[0], ui_ref[0], kr_ref[0], ki_ref[0]
    packed = jnp.logical_and(lax.broadcasted_iota(I32, ur.shape, 0) == 0, f == 0)
    yr = jnp.where(packed, ur * kr, ur * kr - ui * ki)
    yi = jnp.where(packed, ui * ki, ur * ki + ui * kr)
    acc_ref[...] += _dot(wr_ref[...], yr.astype(BF16)) + _dot(wi_ref[...], yi.astype(BF16))

    @pl.when(f == nf - 1)
    def _():
        u = u_ref[0].astype(F32)
        o_ref[0] = (x0_ref[0].astype(F32) * (acc_ref[...] + u * bias_ref[...])).astype(o_ref.dtype)


def _hyena_inverse(winv, uhat, kspec, u, x0, bias, tm=2048, tf=512):
    b, l, c = u.shape
    nfreq = uhat.shape[1] // 2
    tm = _tile(l, tm)
    tf = _tile(nfreq, tf)
    nf = nfreq // tf
    return pl.pallas_call(
        functools.partial(_hy_inv_kernel, nf=nf),
        grid=(b, l // tm, nf),
        in_specs=[pl.BlockSpec((tm, tf), lambda bi, i, f: (i, f)),
                  pl.BlockSpec((tm, tf), lambda bi, i, f: (i, nf + f)),
                  pl.BlockSpec((1, tf, c), lambda bi, i, f: (bi, f, 0)),
                  pl.BlockSpec((1, tf, c), lambda bi, i, f: (bi, nf + f, 0)),
                  pl.BlockSpec((1, tf, c), lambda bi, i, f: (0, f, 0)),
                  pl.BlockSpec((1, tf, c), lambda bi, i, f: (0, nf + f, 0)),
                  pl.BlockSpec((1, tm, c), lambda bi, i, f: (bi, i, 0)),
                  pl.BlockSpec((1, tm, c), lambda bi, i, f: (bi, i, 0)),
                  pl.BlockSpec((1, c), lambda bi, i, f: (0, 0))],
        out_specs=pl.BlockSpec((1, tm, c), lambda bi, i, f: (bi, i, 0)),
        out_shape=jax.ShapeDtypeStruct((b, l, c), BF16),
        scratch_shapes=[pltpu.VMEM((tm, c), F32)],
        compiler_params=_cparams("parallel", "parallel", "arbitrary"),
    )(winv, winv, uhat, uhat, kspec, kspec, u, x0, bias.reshape(1, c))


LONG_SEQ = 2048
FFT_N2 = 128
FNET_N2 = 64
K1_GROUP = 8
HY_K1_STEP = 4


def _cos_sin(m, n):
    ang = (2.0 * math.pi / n) * (m % n).astype(np.float64)
    return np.cos(ang).astype(np.float32), np.sin(ang).astype(np.float32)


def _rows(x):
    return x.reshape(-1, x.shape[-1]).astype(BF16)


def _per_sublane(t):
    return np.kron(t, np.eye(SUBLANES, dtype=np.float32)).astype(BF16)


def _stage1_kernel(*refs, n_in):
    tabs, ins, o_ref = refs[:n_in], refs[n_in:2 * n_in], refs[2 * n_in]
    acc = _dot(tabs[0][...], _rows(ins[0][0]))
    for t_ref, x_ref in zip(tabs[1:], ins[1:]):
        acc = acc + _dot(t_ref[...], _rows(x_ref[0]))
    o_ref[0] = acc.reshape(o_ref.shape[1:])


def _dft_stage1(tables, inputs, batch_maps, nb):
    k1 = tables[0].shape[0] // 2
    n2w, c = inputs[0].shape[2:]
    tables = [_per_sublane(t) for t in tables]
    in_specs = [pl.BlockSpec(t.shape, lambda p, j: (0, 0)) for t in tables]
    in_specs += [pl.BlockSpec((1, a.shape[1], SUBLANES, c), functools.partial(lambda p, j, m: (m(p), 0, j, 0), m=m))
                 for a, m in zip(inputs, batch_maps)]
    return pl.pallas_call(
        functools.partial(_stage1_kernel, n_in=len(inputs)),
        grid=(nb, n2w // SUBLANES),
        in_specs=in_specs,
        out_specs=pl.BlockSpec((1, 2, k1, SUBLANES, c), lambda p, j: (p, 0, 0, j, 0)),
        out_shape=jax.ShapeDtypeStruct((nb, 2, k1, n2w, c), F32),
        compiler_params=_cparams("parallel", "parallel"),
    )(*tables, *inputs)


def _fnet2_kernel(a_ref, g_ref, o_ref):
    o_ref[0] = _dot(g_ref[0], _rows(a_ref[0])).reshape(o_ref.shape[1:])


def _fnet_long(y_cos, y_sin):
    b, l, c = y_cos.shape
    n2w, n1w, kg = FNET_N2, l // FNET_N2, K1_GROUP
    idx = np.arange(n1w)
    c1, s1 = _cos_sin(idx[:, None] * idx[None, :], n1w)
    t_cos = np.concatenate([c1, -s1], axis=0)
    t_sin = np.concatenate([-s1, -c1], axis=0)
    view = lambda a: a.reshape(b, n1w, n2w, c)
    a = _dft_stage1([t_cos, t_sin], [view(y_cos), view(y_sin)], [lambda p: p, lambda p: p], b)
    grp = np.arange(n1w // kg)[:, None, None, None]
    k2 = np.arange(n2w)[None, :, None, None]
    j = np.arange(kg)[None, None, :, None]
    n2 = np.arange(n2w)[None, None, None, :]
    c2, s2 = _cos_sin((n1w * k2 + kg * grp + j) * n2, l)
    eye = np.eye(kg, dtype=np.float32)
    expand = lambda t: np.einsum('gkjn,ji->gkjin', t, eye).reshape(n1w // kg, n2w * kg, kg * n2w)
    gbig = (np.concatenate([expand(c2), expand(s2)], axis=2) * np.float32(1.0 / math.sqrt(l))).astype(BF16)
    out = pl.pallas_call(
        _fnet2_kernel,
        grid=(n1w // kg, b),
        in_specs=[pl.BlockSpec((1, 2, kg, n2w, c), lambda i, bi: (bi, 0, i, 0, 0)),
                  pl.BlockSpec((1,) + gbig.shape[1:], lambda i, bi: (i, 0, 0))],
        out_specs=pl.BlockSpec((1, n2w, kg, c), lambda i, bi: (bi, 0, i, 0)),
        out_shape=jax.ShapeDtypeStruct((b, n2w, n1w, c), F32),
        compiler_params=_cparams("parallel", "parallel"),
    )(a, gbig)
    return out.reshape(b, l, c)


def _stage2(g, a):
    return _dot(g, _rows(a))


def _spec_kernel(a_ref, g_ref, ss_ref, o_ref, *, scale):
    n2w = a_ref.shape[3]
    col = lax.rsqrt(ss_ref[...] + NORM_EPS) * scale
    for j in range(a_ref.shape[2]):
        o = _stage2(g_ref[j], a_ref[0, :, j]) * col
        o_ref[0, j] = o[:n2w]
        o_ref[1, j] = o[n2w:]


def _hy_mid_kernel(a_ref, g_ref, gh_ref, k_ref, o_ref):
    n2w = a_ref.shape[3]
    for j in range(a_ref.shape[2]):
        o = _stage2(g_ref[j], a_ref[0, :, j])
        o_r, o_i = o[:n2w], o[n2w:]
        k_r, k_i = k_ref[0, j], k_ref[1, j]
        y = jnp.concatenate([o_r * k_r - o_i * k_i, o_r * k_i + o_i * k_r], axis=0).astype(BF16)
        z = _dot(gh_ref[j], y)
        o_ref[0, 0, j] = z[:n2w]
        o_ref[0, 1, j] = z[n2w:]


def _hy_out_kernel(c_ref, t_ref, u_ref, x0_ref, bias_ref, o_ref):
    y = _dot(t_ref[...], _rows(c_ref[0])).reshape(o_ref.shape)
    o_ref[...] = x0_ref[...] * (y + u_ref[...] * bias_ref[...])


def _hyena_long(uu, x0, bias, k_raw, k_ss):
    b, l, c = uu.shape
    assert b % 2 == 0
    n = 2 * l
    n2w, n1w = FFT_N2, n // FFT_N2
    half, kb = n1w // 2, HY_K1_STEP
    k1 = np.arange(n1w)
    c1, s1 = _cos_sin(k1[:, None] * k1[None, :], n1w)
    bf = lambda t: t.astype(BF16)
    k2 = np.arange(n2w)
    c2, s2 = _cos_sin((n1w * k2[None, :, None] + k1[:, None, None]) * k2[None, None, :], n)
    g = bf(np.concatenate([np.concatenate([c2, s2], axis=2), np.concatenate([-s2, c2], axis=2)], axis=1))
    c2t, s2t = np.swapaxes(c2, 1, 2), np.swapaxes(s2, 1, 2)
    gh = bf(np.concatenate([np.concatenate([c2t, -s2t], axis=2), np.concatenate([s2t, c2t], axis=2)], axis=1))

    ak = _dft_stage1([np.concatenate([c1, -s1], axis=0)], [k_raw.reshape(1, n1w, n2w, c)], [lambda p: 0], 1)
    kspec = pl.pallas_call(
        functools.partial(_spec_kernel, scale=1.0 / n),
        grid=(n1w // kb,),
        in_specs=[pl.BlockSpec((1, 2, kb, n2w, c), lambda i: (0, 0, i, 0, 0)),
                  pl.BlockSpec((kb, 2 * n2w, 2 * n2w), lambda i: (i, 0, 0)),
                  pl.BlockSpec((1, c), lambda i: (0, 0))],
        out_specs=pl.BlockSpec((2, kb, n2w, c), lambda i: (0, i, 0, 0)),
        out_shape=jax.ShapeDtypeStruct((2, n1w, n2w, c), F32),
        compiler_params=_cparams("parallel"),
    )(ak, g, k_ss)

    ch, sh = c1[:, :half], s1[:, :half]
    view = lambda a: a.reshape(b, half, n2w, c)
    a = _dft_stage1([np.concatenate([ch, -sh], axis=0), np.concatenate([sh, ch], axis=0)],
                    [view(uu), view(uu)], [lambda p: 2 * p, lambda p: 2 * p + 1], b // 2)
    z = pl.pallas_call(
        _hy_mid_kernel,
        grid=(n1w // kb, b // 2),
        in_specs=[pl.BlockSpec((1, 2, kb, n2w, c), lambda i, p: (p, 0, i, 0, 0)),
                  pl.BlockSpec((kb, 2 * n2w, 2 * n2w), lambda i, p: (i, 0, 0)),
                  pl.BlockSpec((kb, 2 * n2w, 2 * n2w), lambda i, p: (i, 0, 0)),
                  pl.BlockSpec((2, kb, n2w, c), lambda i, p: (0, i, 0, 0))],
        out_specs=pl.BlockSpec((1, 2, kb, n2w, c), lambda i, p: (p, 0, i, 0, 0)),
        out_shape=jax.ShapeDtypeStruct((b // 2, 2, n1w, n2w, c), F32),
        compiler_params=_cparams("parallel", "parallel"),
    )(a, g, gh, kspec)
    cht, sht = ch.T, sh.T
    t_inv = _per_sublane(
        np.concatenate([np.concatenate([cht, -sht], axis=1), np.concatenate([sht, cht], axis=1)], axis=0))
    pair = pl.BlockSpec((2, half, SUBLANES, c), lambda p, j: (p, 0, j, 0))
    out = pl.pallas_call(
        _hy_out_kernel,
        grid=(b // 2, n2w // SUBLANES),
        in_specs=[pl.BlockSpec((1, 2, n1w, SUBLANES, c), lambda p, j: (p, 0, 0, j, 0)),
                  pl.BlockSpec(t_inv.shape, lambda p, j: (0, 0)),
                  pair, pair, pl.BlockSpec((1, c), lambda p, j: (0, 0))],
        out_specs=pair,
        out_shape=jax.ShapeDtypeStruct((b, half, n2w, c), F32),
        compiler_params=_cparams("parallel", "parallel"),
    )(z, t_inv, view(uu), view(x0), bias.reshape(1, c))
    return out.reshape(b, l, c)


def _mixer_residual(x_ref, part_refs, wo_ref, gate_ref):
    y, row = None, 0
    for p_ref in part_refs:
        n = p_ref.shape[2]
        t = _dot(p_ref[0].astype(BF16), wo_ref[row:row + n])
        y = t if y is None else y + t
        row += n
    return x_ref[0] + gate_ref[0] * y


def _proj_res_kernel(x_ref, a_ref, wo_ref, gate_ref, o_ref):
    o_ref[0] = _mixer_residual(x_ref, [a_ref], wo_ref, gate_ref)


def _proj_residual(x, a, w_out, gate, tm=512):
    b, l, d = x.shape
    tm = _tile(l, tm)
    tok = lambda n: pl.BlockSpec((1, tm, n), lambda bi, i: (bi, i, 0))
    return pl.pallas_call(
        _proj_res_kernel,
        grid=(b, l // tm),
        in_specs=[tok(d), tok(a.shape[2]), pl.BlockSpec(w_out.shape, lambda bi, i: (0, 0)),
                  pl.BlockSpec((1, 1, d), lambda bi, i: (bi, 0, 0))],
        out_specs=tok(d),
        out_shape=jax.ShapeDtypeStruct((b, l, d), F32),
        compiler_params=_cparams("parallel", "parallel"),
    )(x, a, w_out, gate)


def _ffn_kernel(*refs, n_parts):
    x_ref, part_refs = refs[0], refs[1:1 + n_parts]
    wo_ref, gate1_ref, g_ref, sh_ref, sc_ref, gate_ref, wg_ref, wu_ref, wd_ref, o_ref = refs[1 + n_parts:]
    x = _mixer_residual(x_ref, part_refs, wo_ref, gate1_ref)
    h = _norm_mod(x, g_ref[...], sh_ref[0], sc_ref[0]).astype(BF16)
    mid = _silu(_dot(h, wg_ref[...])) * _dot(h, wu_ref[...])
    o_ref[0] = x + gate_ref[0] * _dot(mid.astype(BF16), wd_ref[...])


def _mixer_out_ffn(x, parts, w_out, gate1, g, shift, scale, gate, wg, wu, wd, tm=512):
    b, l, d = x.shape
    tm = _tile(l, tm)
    vec = pl.BlockSpec((1, 1, d), lambda bi, i: (bi, 0, 0))
    tok = lambda n: pl.BlockSpec((1, tm, n), lambda bi, i: (bi, i, 0))
    resident = lambda w: pl.BlockSpec(w.shape, lambda bi, i: (0, 0), pipeline_mode=pl.Buffered(1))
    return pl.pallas_call(
        functools.partial(_ffn_kernel, n_parts=len(parts)),
        grid=(b, l // tm),
        in_specs=[tok(d)] + [tok(p.shape[2]) for p in parts]
        + [resident(w_out), vec, pl.BlockSpec((1, d), lambda bi, i: (0, 0)), vec, vec, vec,
           resident(wg), resident(wu), resident(wd)],
        out_specs=tok(d),
        out_shape=jax.ShapeDtypeStruct((b, l, d), F32),
        compiler_params=_cparams("parallel", "parallel"),
    )(x, *parts, w_out, gate1, g.reshape(1, d), shift, scale, gate, wg, wu, wd)


def _head_norm(t, e, et, g_full):
    ss = _dot((t * t).astype(BF16), e)
    rinv = lax.rsqrt(ss * (1.0 / HEAD_DIM) + NORM_EPS)
    hi, lo = _split(rinv)
    return t * (_dot(hi, et) + _dot(lo, et)) * g_full


def _rope(t, cos, sin_lo, sin_hi):
    w = t.shape[1]
    rep = w // LANES
    tile = lambda a: jnp.concatenate([a] * rep, axis=1)
    return (t * tile(cos) + pltpu.roll(t, w - ROPE_HALF, 1) * tile(sin_lo)
            + pltpu.roll(t, ROPE_HALF, 1) * tile(sin_hi))


def _qkv_kernel(x_ref, g_ref, sh_ref, sc_ref, w_ref, e_ref, et_ref, qg_ref, kg_ref, cos_ref, sl_ref, sh2_ref,
                *out_refs, qw, kw, rope):
    k_ref, v_ref = out_refs[-2:]
    h = _norm_mod(x_ref[0], g_ref[...], sh_ref[0], sc_ref[0])
    t = _dot(h.astype(BF16), w_ref[...])
    e, et = e_ref[...], et_ref[...]
    k = _head_norm(t[:, qw:qw + kw], e[:kw], et[:, :kw], kg_ref[...])
    if rope:
        k = _rope(k, cos_ref[...], sl_ref[...], sh2_ref[...])
    k_ref[0] = k.astype(BF16)
    v_ref[0] = t[:, qw + kw:].astype(BF16)
    if qw:
        q = _head_norm(t[:, :qw], e, et, qg_ref[...])
        q = _rope(q, cos_ref[...], sl_ref[...], sh2_ref[...]) * (HEAD_DIM ** -0.5)
        out_refs[0][0] = q.astype(BF16)


def _head_tables(qw):
    lane = np.arange(qw)
    e = (lane[:, None] // HEAD_DIM == np.arange(LANES)[None, :]).astype(BF16)
    return e, np.ascontiguousarray(e.T)


def _rope_tables(l):
    rows = l // GRID_W
    row = np.repeat(np.arange(rows, dtype=np.float32), GRID_W)
    col = np.tile(np.arange(GRID_W, dtype=np.float32), rows)
    inv = (ROPE_THETA ** (-np.arange(0, AXIS_ROPE_DIM, 2, dtype=np.float32) / AXIS_ROPE_DIM)).astype(np.float32)
    lane = np.arange(LANES)
    in_head = lane % HEAD_DIM
    use_col = (in_head // AXIS_ROPE_DIM) == 1
    hi_half = ((in_head % AXIS_ROPE_DIM) // ROPE_HALF) == 1
    freq = inv[in_head % ROPE_HALF]
    ang = (np.where(use_col[None, :], col[:, None], row[:, None]) * freq[None, :]).astype(np.float32)
    cos, sin = np.cos(ang), np.sin(ang)
    zero = np.float32(0.0)
    return cos, np.where(hi_half[None, :], zero, -sin), np.where(hi_half[None, :], sin, zero)


def _qkv_project(x, g, shift, scale, w, q_g, k_g, qw, kw, rope, tm=512):
    b, l, d = x.shape
    tm = _tile(l, tm)
    e, et = _head_tables(max(qw, kw))
    n_q = max(qw, kw) // HEAD_DIM
    qg = jnp.tile(q_g, n_q).reshape(1, -1)
    kg = jnp.tile(k_g, kw // HEAD_DIM).reshape(1, kw)
    if rope:
        cos, s_lo, s_hi = _rope_tables(l)
    else:
        cos = s_lo = s_hi = jnp.zeros((l, LANES), F32)
    vec = pl.BlockSpec((1, 1, d), lambda bi, i: (bi, 0, 0))
    full = lambda a: pl.BlockSpec(a.shape, lambda bi, i: (0,) * a.ndim)
    tab = pl.BlockSpec((tm, LANES), lambda bi, i: (i, 0))
    widths = ([qw] if qw else []) + [kw, kw]
    return pl.pallas_call(
        functools.partial(_qkv_kernel, qw=qw, kw=kw, rope=rope),
        grid=(b, l // tm),
        in_specs=[pl.BlockSpec((1, tm, d), lambda bi, i: (bi, i, 0)),
                  pl.BlockSpec((1, d), lambda bi, i: (0, 0)), vec, vec,
                  full(w), full(e), full(et), full(qg), full(kg), tab, tab, tab],
        out_specs=[pl.BlockSpec((1, tm, n), lambda bi, i: (bi, i, 0)) for n in widths],
        out_shape=[jax.ShapeDtypeStruct((b, l, n), BF16) for n in widths],
        compiler_params=_cparams("parallel", "parallel"),
    )(x, g.reshape(1, d), shift, scale, w, e, et, qg, kg, cos, s_lo, s_hi)


def _attn_kernel(q_ref, kp_ref, kc_ref, kn_ref, vp_ref, vc_ref, vn_ref, kx_ref, vx_ref, sink_ref, o_ref, *,
                 seq, group):
    bq = BLOCK_Q
    lc = kx_ref.shape[1]
    nkeys = 3 * bq + lc
    rows = lax.broadcasted_iota(I32, (group * bq, nkeys), 0)
    cols = lax.broadcasted_iota(I32, (group * bq, nkeys), 1)
    for t in range(2):
        qb = 2 * pl.program_id(1) + t
        own = slice(t * bq, (t + 1) * bq)
        qpos = qb * bq + rows % bq
        kpos = (qb - 1) * bq + cols
        valid = jnp.logical_or(
            cols >= 3 * bq,
            jnp.logical_and(jnp.logical_and(kpos >= 0, kpos < seq), jnp.abs(qpos - kpos) <= WINDOW))
        for h in range(N_KV_HEADS):
            ks = slice(h * HEAD_DIM, (h + 1) * HEAD_DIM)

            def band(p_ref, c_ref, n_ref, x_ref):
                before = p_ref[0, :, ks] if t == 0 else c_ref[0, :bq, ks]
                after = c_ref[0, bq:, ks] if t == 0 else n_ref[0, :, ks]
                return jnp.concatenate([before, c_ref[0, own, ks], after, x_ref[0, :, ks]], axis=0)

            kh = band(kp_ref, kc_ref, kn_ref, kx_ref)
            vh = band(vp_ref, vc_ref, vn_ref, vx_ref)
            qh = jnp.concatenate(
                [q_ref[0, own, (h * group + g) * HEAD_DIM:(h * group + g + 1) * HEAD_DIM] for g in range(group)],
                axis=0)
            s = lax.dot_general(qh, kh, (((1,), (1,)), ((), ())), preferred_element_type=F32)
            s = jnp.where(valid, s, NEG_INF)
            sk = sink_ref[h]
            m = jnp.maximum(jnp.max(s, axis=1, keepdims=True), sk)
            p = jnp.exp(s - m)
            den = jnp.sum(p, axis=1, keepdims=True) + jnp.exp(sk - m)
            o = _dot(p.astype(BF16), vh) / den
            for g in range(group):
                hq = h * group + g
                o_ref[0, own, hq * HEAD_DIM:(hq + 1) * HEAD_DIM] = o[g * bq:(g + 1) * bq].astype(o_ref.dtype)


def _window_attention(q, k, v, kx, vx, sink):
    b, l, qw = q.shape
    kw = k.shape[2]
    lc = kx.shape[1]
    bq = BLOCK_Q
    nb = l // bq
    assert nb % 2 == 0
    group = qw // kw
    sink_tab = jnp.repeat(sink.astype(F32).reshape(N_KV_HEADS, group), bq, axis=1)[..., None]
    kv_prev = pl.BlockSpec((1, bq, kw), lambda bi, i: (bi, jnp.maximum(2 * i - 1, 0), 0))
    kv_pair = pl.BlockSpec((1, 2 * bq, kw), lambda bi, i: (bi, i, 0))
    kv_next = pl.BlockSpec((1, bq, kw), lambda bi, i: (bi, jnp.minimum(2 * i + 2, nb - 1), 0))
    kv_ctx = pl.BlockSpec((1, lc, kw), lambda bi, i: (bi, 0, 0))
    return pl.pallas_call(
        functools.partial(_attn_kernel, seq=l, group=group),
        grid=(b, nb // 2),
        in_specs=[pl.BlockSpec((1, 2 * bq, qw), lambda bi, i: (bi, i, 0)),
                  kv_prev, kv_pair, kv_next, kv_prev, kv_pair, kv_next, kv_ctx, kv_ctx,
                  pl.BlockSpec(sink_tab.shape, lambda bi, i: (0, 0, 0))],
        out_specs=pl.BlockSpec((1, 2 * bq, qw), lambda bi, i: (bi, i, 0)),
        out_shape=jax.ShapeDtypeStruct((b, l, qw), BF16),
        compiler_params=_cparams("parallel", "parallel"),
    )(q, k, k, k, v, v, v, kx, vx, sink_tab)


def _router_kernel(x_ref, g_ref, sh_ref, sc_ref, wh_ref, wl_ref, tri_ref, h_ref, gate_ref, rank_ref, rankt_ref,
                   cnt_ref):
    h = _norm_mod(x_ref[0], g_ref[...], sh_ref[0], sc_ref[0])
    h_ref[0] = h.astype(BF16)
    hi, lo = _split(h)
    logits = _dot(hi, wh_ref[...]) + _dot(hi, wl_ref[...]) + _dot(lo, wh_ref[...])
    lane = lax.broadcasted_iota(I32, logits.shape, 1)
    logits = jnp.where(lane < N_EXPERTS, logits, -jnp.inf)
    m1 = jnp.max(logits, axis=1, keepdims=True)
    i1 = jnp.min(jnp.where(logits == m1, lane, LANES), axis=1, keepdims=True)
    rest = jnp.where(lane == i1, -jnp.inf, logits)
    m2 = jnp.max(rest, axis=1, keepdims=True)
    i2 = jnp.min(jnp.where(rest == m2, lane, LANES), axis=1, keepdims=True)
    e = jnp.exp(m2 - m1)
    g1 = 1.0 / (1.0 + e)
    g2 = e / (1.0 + e)
    pick1, pick2 = lane == i1, lane == i2
    member = jnp.logical_or(pick1, pick2)
    gate_ref[0] = jnp.where(pick1, g1, jnp.where(pick2, g2, 0.0))
    m = jnp.where(member, 1.0, 0.0)
    rank = jnp.where(member, _dot(tri_ref[...], m.astype(BF16)), -1.0)
    rank_ref[0] = rank.astype(I32)
    rankt_ref[0] = jnp.transpose(rank)[:SUBLANES].astype(I32)
    cnt_ref[0] = jnp.sum(m, axis=0, keepdims=True).astype(I32)


def _route(x, g, shift, scale, w_router):
    b, l, d = x.shape
    tm = _tile(l, MOE_TILE)
    nt = l // tm
    wr = jnp.concatenate([w_router, jnp.zeros((d, LANES - w_router.shape[1]), F32)], axis=1)
    wh, wl = _split(wr)
    tri = np.tril(np.ones((tm, tm), np.float32), -1).astype(BF16)
    vec = pl.BlockSpec((1, 1, d), lambda bi, i: (bi, 0, 0))
    tok = lambda n: pl.BlockSpec((1, tm, n), lambda bi, i: (bi, i, 0))
    return pl.pallas_call(
        _router_kernel,
        grid=(b, nt),
        in_specs=[tok(d), pl.BlockSpec((1, d), lambda bi, i: (0, 0)), vec, vec,
                  pl.BlockSpec((d, LANES), lambda bi, i: (0, 0)),
                  pl.BlockSpec((d, LANES), lambda bi, i: (0, 0)),
                  pl.BlockSpec((tm, tm), lambda bi, i: (0, 0))],
        out_specs=[tok(d), tok(LANES), tok(LANES),
                   pl.BlockSpec((1, SUBLANES, tm), lambda bi, i: (bi * nt + i, 0, 0)),
                   pl.BlockSpec((1, 1, LANES), lambda bi, i: (bi * nt + i, 0, 0))],
        out_shape=[jax.ShapeDtypeStruct((b, l, d), BF16),
                   jax.ShapeDtypeStruct((b, l, LANES), F32),
                   jax.ShapeDtypeStruct((b, l, LANES), I32),
                   jax.ShapeDtypeStruct((b * nt, SUBLANES, tm), I32),
                   jax.ShapeDtypeStruct((b * nt, 1, LANES), I32)],
        compiler_params=_cparams("parallel", "parallel"),
    )(x, g.reshape(1, d), shift, scale, wh, wl, tri)


def _slot_layout(cnt, rows):
    nt = cnt.shape[0]
    seg = (cnt + SUBLANES - 1) // SUBLANES * SUBLANES
    padded = (jnp.sum(seg, axis=0) + rows - 1) // rows * rows
    stride = padded + rows
    pstart = jnp.cumsum(stride) - stride
    off = pstart[None, :] + jnp.cumsum(seg, axis=0) - seg
    n_slots = (nt * N_EXPERTS * (SUBLANES - 1) + nt * MOE_TILE * TOP_K + rows - 1) // rows * rows \
        + 2 * N_EXPERTS * rows
    nblk = n_slots // rows
    bstart = jnp.arange(nblk, dtype=I32) * rows
    be = jnp.minimum(jnp.sum((bstart[:, None] >= (pstart + stride)[None, :]).astype(I32), axis=1), N_EXPERTS - 1)
    onehot = (be[:, None] == jnp.arange(N_EXPERTS, dtype=I32)[None, :]).astype(I32)
    lo = jnp.sum(onehot * pstart[None, :], axis=1)
    hi = jnp.sum(onehot * (pstart + padded)[None, :], axis=1)
    used = jnp.logical_and(bstart >= lo, bstart < hi)
    zero_blk = jnp.logical_or(jnp.logical_not(used), bstart == hi - rows)
    as_i32 = lambda a: a.reshape(-1).astype(I32)
    return n_slots, as_i32(off), as_i32(cnt), as_i32(zero_blk), as_i32(be), as_i32(used)


def _seg_copy(src_ref, dst_ref, row, sem):
    n = src_ref.shape[0]
    return pltpu.make_async_copy(src_ref, dst_ref.at[pl.ds(pl.multiple_of(row, SUBLANES), n)], sem)


def _chunk_copy(src_ref, dst_ref, row, sem):
    n = dst_ref.shape[0]
    return pltpu.make_async_copy(src_ref.at[pl.ds(pl.multiple_of(row, SUBLANES), n)], dst_ref, sem)


def _dispatch_kernel(off_ref, cnt_ref, zero_ref, h_ref, rt_ref, xs_ref, xbuf, xbuf2, sem, sem2, *, rows, n_tiles):
    i = pl.program_id(0)
    cap = xbuf.shape[2]
    tm = h_ref.shape[1]
    slot = i % 2

    @pl.when(i == 0)
    def _():
        xbuf[0, 0] = jnp.zeros(xbuf.shape[2:], F32)

        def zero_block(j, carry):
            @pl.when(zero_ref[j] > 0)
            def _():
                for part in range(rows // cap):
                    c = _seg_copy(xbuf.at[0, 0], xs_ref, j * rows + part * cap, sem2)
                    c.start()
                    c.wait()
            return carry

        lax.fori_loop(0, zero_ref.shape[0], zero_block, 0)

    h = h_ref[0]
    rt = rt_ref[0]
    riota = lax.broadcasted_iota(I32, (cap, tm), 0)

    def copy(t, s, e):
        return _seg_copy(xbuf.at[s, e], xs_ref, off_ref[t * N_EXPERTS + e], sem.at[s, e])

    for e in range(N_EXPERTS):
        sel = rt[e:e + 1, :]

        def segment(base):
            return _dot(jnp.where(sel == riota + base, 1.0, 0.0).astype(BF16), h)

        @pl.when(i > 0)
        def _():
            copy(i - 1, 1 - slot, e).wait()

        xbuf[slot, e] = segment(0)
        copy(i, slot, e).start()

        @pl.when(cnt_ref[i * N_EXPERTS + e] > cap)
        def _():
            xbuf2[...] = segment(cap)
            c = _seg_copy(xbuf2, xs_ref, off_ref[i * N_EXPERTS + e] + cap, sem2)
            c.start()
            c.wait()

    @pl.when(i == n_tiles - 1)
    def _():
        for e in range(N_EXPERTS):
            copy(i, slot, e).wait()


def _dispatch(h, rankt, n_slots, off, cnt, zero_blk, rows):
    b, l, d = h.shape
    tm = _tile(l, MOE_TILE)
    nt = l // tm
    cap = MOE_CAP
    assert tm <= 2 * cap and rows % cap == 0
    grid_spec = pltpu.PrefetchScalarGridSpec(
        num_scalar_prefetch=3,
        grid=(b * nt,),
        in_specs=[pl.BlockSpec((1, tm, d), lambda i, *_: (i // nt, i % nt, 0)),
                  pl.BlockSpec((1, SUBLANES, tm), lambda i, *_: (i, 0, 0))],
        out_specs=pl.BlockSpec(memory_space=pl.ANY),
        scratch_shapes=[pltpu.VMEM((2, N_EXPERTS, cap, d), F32), pltpu.VMEM((cap, d), F32),
                        pltpu.SemaphoreType.DMA((2, N_EXPERTS)), pltpu.SemaphoreType.DMA(())],
    )
    return pl.pallas_call(
        functools.partial(_dispatch_kernel, rows=rows, n_tiles=b * nt),
        grid_spec=grid_spec,
        out_shape=jax.ShapeDtypeStruct((n_slots, d), F32),
        compiler_params=_cparams("arbitrary"),
    )(off, cnt, zero_blk, h, rankt)


def _moe_kernel(be_ref, used_ref, x_ref, wg_ref, wu_ref, wd_ref, o_ref):
    i = pl.program_id(0)

    @pl.when(used_ref[i] > 0)
    def _():
        x = x_ref[...].astype(BF16)
        mid = _silu(_dot(x, wg_ref[0])) * _dot(x, wu_ref[0])
        o_ref[...] = _dot(mid.astype(BF16), wd_ref[0])

    @pl.when(used_ref[i] == 0)
    def _():
        o_ref[...] = jnp.zeros_like(o_ref)


def _expert_ffn(xs, block_e, used, wg, wu, wd, rows):
    s, d = xs.shape
    nblk = s // rows
    expert = lambda w: pl.BlockSpec((1,) + w.shape[1:], lambda i, be, us: (be[i], 0, 0),
                                    pipeline_mode=pl.Buffered(1))
    grid_spec = pltpu.PrefetchScalarGridSpec(
        num_scalar_prefetch=2,
        grid=(nblk,),
        in_specs=[pl.BlockSpec((rows, d), lambda i, be, us: (i, 0)), expert(wg), expert(wu), expert(wd)],
        out_specs=pl.BlockSpec((rows, d), lambda i, be, us: (i, 0)),
    )
    return pl.pallas_call(
        _moe_kernel,
        grid_spec=grid_spec,
        out_shape=jax.ShapeDtypeStruct((s, d), F32),
        compiler_params=_cparams("arbitrary"),
    )(block_e, used, xs, wg, wu, wd)


def _combine_kernel(off_ref, cnt_ref, x_ref, gate_ref, rg_ref, rank_ref, ys_ref, o_ref, buf, buf2, acc_ref, sem,
                    sem2, *, n_tiles):
    i = pl.program_id(0)
    slot = i % 2
    cap = buf.shape[2]
    tm = x_ref.shape[1]

    def chunk(t, s, e):
        return _chunk_copy(ys_ref, buf.at[s, e], off_ref[t * N_EXPERTS + e], sem.at[s, e])

    @pl.when(i == 0)
    def _():
        for e in range(N_EXPERTS):
            chunk(0, 0, e).start()

    @pl.when(i + 1 < n_tiles)
    def _():
        for e in range(N_EXPERTS):
            chunk(i + 1, 1 - slot, e).start()

    rank = rank_ref[0]
    rg = rg_ref[0]
    liota = lax.broadcasted_iota(I32, (tm, cap), 1)

    def picked(e, base, rows_ref):
        q = jnp.where(rank[:, e:e + 1] == liota + base, 1.0, 0.0).astype(BF16)
        return rg[:, e:e + 1] * _dot(q, rows_ref[...].astype(BF16))

    y = jnp.zeros(acc_ref.shape, F32)
    for e in range(N_EXPERTS):
        chunk(i, slot, e).wait()
        y = y + picked(e, 0, buf.at[slot, e])
    acc_ref[...] = y

    for e in range(N_EXPERTS):
        @pl.when(cnt_ref[i * N_EXPERTS + e] > cap)
        def _():
            c = _chunk_copy(ys_ref, buf2, off_ref[i * N_EXPERTS + e] + cap, sem2)
            c.start()
            c.wait()
            acc_ref[...] += picked(e, cap, buf2)

    o_ref[0] = x_ref[0] + gate_ref[0] * acc_ref[...]


def _moe_combine(x, gate, route_gates, rank, ys, off, cnt):
    b, l, d = x.shape
    tm = _tile(l, MOE_TILE)
    nt = l // tm
    cap = MOE_CAP
    tok = lambda n: pl.BlockSpec((1, tm, n), lambda i, *_: (i // nt, i % nt, 0))
    grid_spec = pltpu.PrefetchScalarGridSpec(
        num_scalar_prefetch=2,
        grid=(b * nt,),
        in_specs=[tok(d), pl.BlockSpec((1, 1, d), lambda i, *_: (i // nt, 0, 0)), tok(LANES), tok(LANES),
                  pl.BlockSpec(memory_space=pl.ANY)],
        out_specs=tok(d),
        scratch_shapes=[pltpu.VMEM((2, N_EXPERTS, cap, d), F32), pltpu.VMEM((cap, d), F32),
                        pltpu.VMEM((tm, d), F32),
                        pltpu.SemaphoreType.DMA((2, N_EXPERTS)), pltpu.SemaphoreType.DMA(())],
    )
    return pl.pallas_call(
        functools.partial(_combine_kernel, n_tiles=b * nt),
        grid_spec=grid_spec,
        out_shape=jax.ShapeDtypeStruct((b, l, d), F32),
        compiler_params=_cparams("arbitrary"),
    )(off, cnt, x, gate, route_gates, rank, ys)


def _angles(rows, cols, n):
    m = (rows[:, None] * cols[None, :]) % n
    return (2.0 * math.pi / n) * m.astype(F32)


def _fnet_table(l):
    idx = jnp.arange(l, dtype=I32)
    ang = _angles(idx, idx, l)
    s = 1.0 / math.sqrt(l)
    return jnp.concatenate([jnp.cos(ang) * s, jnp.sin(ang) * (-s)], axis=1).astype(BF16)


def _rfft_table(n):
    half = n // 2
    f = jnp.arange(half, dtype=I32)
    t = jnp.arange(n, dtype=I32)
    ang = _angles(f, t, n)
    top = jnp.cos(ang)
    bot = -jnp.sin(ang)
    nyq = jnp.cos(_angles(jnp.full((1,), half, I32), t, n))
    bot = jnp.concatenate([nyq, bot[1:]], axis=0)
    return jnp.concatenate([top, bot], axis=0).astype(BF16)


def _group_dft_table(width):
    gd = FNET_GROUP_DIM
    idx = jnp.arange(gd, dtype=I32)
    ang = _angles(idx, idx, gd)
    s = 1.0 / math.sqrt(gd)
    eye = jnp.eye(width // gd, dtype=F32)
    return jnp.concatenate([jnp.kron(eye, jnp.cos(ang) * s), jnp.kron(eye, jnp.sin(ang) * s)], axis=1)


def _even_mixer(x, g1, shift, scale, w_in_f, conv_w, conv_b, hy_bias, filt, fw, hw):
    b, l, d = x.shape
    n = 2 * l
    long_seq = l >= LONG_SEQ
    seq_dtype = F32 if long_seq else BF16
    u, y_cos, y_sin = _norm_mod_matmul(x, g1, shift, scale, w_in_f, [3 * hw, fw, fw], [BF16, seq_dtype, seq_dtype])
    uu, x0 = _hyena_pre(u, conv_w, conv_b, hw, seq_dtype)
    k_raw, k_ss = filt(l)
    if long_seq:
        a = _fnet_long(y_cos, y_sin)
        hy = _hyena_long(uu, x0, hy_bias, k_raw, k_ss)
    else:
        y_both = jnp.concatenate([y_cos, y_sin], axis=2)
        a = _left_dft(_fnet_table(l), y_both, lambda bi, k, tk: (bi, k % (l // tk), k // (l // tk)), n, fw, BF16,
                      tk=min(l, 1024))
        wf = _rfft_table(n)
        wts = jnp.concatenate([jnp.ones((1,), F32), jnp.full((l - 1,), 2.0, F32)]) / n
        row_scale = jnp.concatenate([wts, wts]).reshape(n, 1)
        kspec = _left_dft(wf, k_raw[None], lambda bi, k, tk: (0, k, 0), n, hw, F32,
                          row_scale=row_scale, col_sumsq=k_ss)
        uhat = _left_dft(wf, uu, lambda bi, k, tk: (bi, k, 0), l, hw, F32)
        hy = _hyena_inverse(wf[:, :l].T, uhat, kspec, uu, x0, hy_bias)
    return a, hy


def kernel(x, c, ctx, c_ctx, ada_w, ada_b, norm1_g, norm2_g, ev_w_in, ev_w_out, hy_conv_w, hy_conv_b, hy_bias,
           hf_w0, hf_b0, hf_w1, hf_b1, hf_w2, hf_b2, hf_w3, hf_freq, ffn_w_gate, ffn_w_up, ffn_w_down, od_w_qkv,
           od_w_out, q_norm_g, k_norm_g, attn_sink, moe_router, moe_w_gate, moe_w_up, moe_w_down):
    b, l, d = x.shape
    lc = ctx.shape[1]
    assert ada_w.shape[0] == 2, "this implementation covers the two-layer (even, odd) stack"
    hw = hy_bias.shape[1]
    fw = ev_w_in.shape[2] - 3 * hw
    qw = od_w_out.shape[1]
    kw = (od_w_qkv.shape[2] - qw) // 2

    rows = (b + 1 + 7) // 8 * 8
    cc = jnp.concatenate([c, c_ctx[None, :], jnp.zeros((rows - b - 1, d), F32)], axis=0)
    mod = _ada_vectors(cc, ada_w, ada_b)
    ml = [[mod[i, :b, None, m * d:(m + 1) * d] for m in range(N_MOD)] for i in range(2)]
    mc = [[mod[i, b:b + 1, None, m * d:(m + 1) * d] for m in range(N_MOD)] for i in range(2)]

    w_in = ev_w_in[0]
    w_fnet = _matmul3(w_in[:, :fw], _group_dft_table(fw))
    w_in_f = jnp.concatenate([w_in[:, fw:], w_fnet], axis=1).astype(BF16)
    w_out0 = ev_w_out[0].astype(BF16)
    filt = lambda seq: _hyena_filter(seq, hw, hf_w0[0], hf_b0[0], hf_w1[0], hf_b1[0], hf_w2[0], hf_b2[0],
                                     hf_w3[0], hf_freq[0])
    wg, wu, wd = ffn_w_gate[0].astype(BF16), ffn_w_up[0].astype(BF16), ffn_w_down[0].astype(BF16)

    mix = _even_mixer(x, norm1_g[0], ml[0][0], ml[0][1], w_in_f, hy_conv_w[0], hy_conv_b[0], hy_bias[0], filt, fw,
                      hw)
    x = _mixer_out_ffn(x, mix, w_out0, ml[0][2], norm2_g[0], ml[0][3], ml[0][4], ml[0][5], wg, wu, wd)

    bc = lambda v: jnp.broadcast_to(v, (b, 1, d))
    mix = _even_mixer(ctx, norm1_g[0], bc(mc[0][0]), bc(mc[0][1]), w_in_f, hy_conv_w[0], hy_conv_b[0], hy_bias[0],
                      filt, fw, hw)
    flat = lambda t: t.reshape(1, b * lc, t.shape[2])
    ctx = _mixer_out_ffn(flat(ctx), [flat(t) for t in mix], w_out0, mc[0][2], norm2_g[0], mc[0][3], mc[0][4],
                         mc[0][5], wg, wu, wd)

    w_qkv = od_w_qkv[0].astype(BF16)
    q, k, v = _qkv_project(x, norm1_g[1], ml[1][0], ml[1][1], w_qkv, q_norm_g[0], k_norm_g[0], qw, kw, True)
    kx, vx = _qkv_project(ctx, norm1_g[1], mc[1][0], mc[1][1], w_qkv[:, qw:], q_norm_g[0], k_norm_g[0], 0, kw,
                          False)
    o = _window_attention(q, k, v, kx.reshape(b, lc, kw), vx.reshape(b, lc, kw), attn_sink[0])
    x = _proj_residual(x, o, od_w_out[0].astype(BF16), ml[1][2])

    h2, gates, rank, rankt, cnt = _route(x, norm2_g[1], ml[1][3], ml[1][4], moe_router[0])
    n_slots, off, cnt, zero_blk, block_e, used = _slot_layout(cnt[:, 0, :N_EXPERTS], MOE_ROWS)
    xs = _dispatch(h2, rankt, n_slots, off, cnt, zero_blk, MOE_ROWS)
    ys = _expert_ffn(xs, block_e, used, moe_w_gate[0].astype(BF16), moe_w_up[0].astype(BF16),
                     moe_w_down[0].astype(BF16), MOE_ROWS)
    return _moe_combine(x, ml[1][5], gates, rank, ys, off, cnt)
```

```python
import functools
import math

import jax
import jax.numpy as jnp
import numpy as np
from jax import lax
from jax.experimental import pallas as pl
from jax.experimental.pallas import tpu as pltpu

F32 = jnp.float32
BF16 = jnp.bfloat16
I32 = jnp.int32

NORM_EPS = 1e-6
NEG_INF = -1e30
N_MOD = 6

FNET_GROUP_DIM = 128
HY_EMB_BANDS = 16
HY_FILTER_HIDDEN = 64
HY_DECAY_TARGET = 1e-2
HY_FAST_DECAY = 0.3
HY_SLOW_DECAY = 1.5

HEAD_DIM = 64
N_KV_HEADS = 4
GRID_W = 64
WINDOW = 128
BLOCK_Q = 128
ATT_BLOCKS = 4
ROPE_THETA = 10000.0
AXIS_ROPE_DIM = HEAD_DIM // 2
ROPE_HALF = AXIS_ROPE_DIM // 2
N_EXPERTS = 8
TOP_K = 2

LANES = 128
SUBLANES = 8
VMEM_LIMIT = 56 * 1024 * 1024
MOE_TILE = 512
MOE_ROWS = 512
MOE_CAP = 256


def _cparams(*sem):
    return pltpu.CompilerParams(dimension_semantics=sem, vmem_limit_bytes=VMEM_LIMIT)


def _split(a):
    hi = a.astype(BF16)
    lo = (a - hi.astype(F32)).astype(BF16)
    return hi, lo


def _dot(a, b):
    return jnp.dot(a, b, preferred_element_type=F32)


def _dot3(a, b):
    ah, al = _split(a)
    bh, bl = _split(b)
    return _dot(ah, bh) + _dot(ah, bl) + _dot(al, bh)


def _silu(t):
    return t / (1.0 + jnp.exp(-t))


def _norm_mod(x, g, shift, scale):
    ms = jnp.mean(x * x, axis=-1, keepdims=True)
    y = x * lax.rsqrt(ms + NORM_EPS) * g
    return y * (1.0 + scale) + shift


def _tile(n, pref):
    t = min(n, pref)
    assert n % t == 0, (n, pref)
    return t


def _ada_kernel(c_ref, w_ref, b_ref, o_ref):
    o_ref[0] = _dot3(_silu(c_ref[...]), w_ref[0]) + b_ref[0]


def _ada_vectors(cc, ada_w, ada_b):
    depth, d, n = ada_w.shape
    rows = cc.shape[0]
    tn = _tile(n, 1536)
    return pl.pallas_call(
        _ada_kernel,
        grid=(depth, n // tn),
        in_specs=[pl.BlockSpec((rows, d), lambda l, j: (0, 0)),
                  pl.BlockSpec((1, d, tn), lambda l, j: (l, 0, j)),
                  pl.BlockSpec((1, 1, tn), lambda l, j: (l, 0, j))],
        out_specs=pl.BlockSpec((1, rows, tn), lambda l, j: (l, 0, j)),
        out_shape=jax.ShapeDtypeStruct((depth, rows, n), F32),
        compiler_params=_cparams("arbitrary", "arbitrary"),
    )(cc, ada_w, ada_b.reshape(depth, 1, n))


def _mm3_kernel(a_ref, b_ref, o_ref):
    o_ref[...] = _dot3(a_ref[...], b_ref[...])


def _matmul3(a, b):
    m, _ = a.shape
    n = b.shape[1]
    return pl.pallas_call(_mm3_kernel, out_shape=jax.ShapeDtypeStruct((m, n), F32),
                          compiler_params=_cparams())(a, b)


def _nmm_kernel(x_ref, g_ref, sh_ref, sc_ref, w_ref, *o_refs):
    h = _norm_mod(x_ref[0], g_ref[...], sh_ref[0], sc_ref[0])
    y = _dot(h.astype(BF16), w_ref[...])
    col = 0
    for o_ref in o_refs:
        n = o_ref.shape[2]
        o_ref[0] = y[:, col:col + n].astype(o_ref.dtype)
        col += n


def _norm_mod_matmul(x, g, shift, scale, w, widths, out_dtypes, tm=512):
    b, l, d = x.shape
    n = w.shape[1]
    assert sum(widths) == n
    tm = _tile(l, tm)
    return pl.pallas_call(
        _nmm_kernel,
        grid=(b, l // tm),
        in_specs=[pl.BlockSpec((1, tm, d), lambda bi, i: (bi, i, 0)),
                  pl.BlockSpec((1, d), lambda bi, i: (0, 0)),
                  pl.BlockSpec((1, 1, d), lambda bi, i: (bi, 0, 0)),
                  pl.BlockSpec((1, 1, d), lambda bi, i: (bi, 0, 0)),
                  pl.BlockSpec((d, n), lambda bi, i: (0, 0))],
        out_specs=[pl.BlockSpec((1, tm, wd), lambda bi, i: (bi, i, 0)) for wd in widths],
        out_shape=[jax.ShapeDtypeStruct((b, l, wd), dt) for wd, dt in zip(widths, out_dtypes)],
        compiler_params=_cparams("parallel", "parallel"),
    )(x, g.reshape(1, d), shift, scale, w)


def _ldft_kernel(w_ref, r_ref, o_ref, acc_ref, *, nk):
    k = pl.program_id(2)

    @pl.when(k == 0)
    def _():
        acc_ref[...] = jnp.zeros_like(acc_ref)

    acc_ref[...] += _dot(w_ref[...], r_ref[0].astype(BF16))

    @pl.when(k == nk - 1)
    def _():
        o_ref[0] = acc_ref[...].astype(o_ref.dtype)


def _ldft_scaled_kernel(w_ref, r_ref, rs_ref, ss_ref, o_ref, acc_ref, *, nk):
    k = pl.program_id(2)

    @pl.when(k == 0)
    def _():
        acc_ref[...] = jnp.zeros_like(acc_ref)

    acc_ref[...] += _dot(w_ref[...], r_ref[0].astype(BF16))

    @pl.when(k == nk - 1)
    def _():
        o_ref[0] = acc_ref[...] * rs_ref[...] * lax.rsqrt(ss_ref[...] + NORM_EPS)


def _left_dft(w, rhs, rhs_map, kdim, c, out_dtype, tm=2048, tk=1024, row_scale=None, col_sumsq=None):
    m = w.shape[0]
    nb = rhs.shape[0]
    tm = _tile(m, tm)
    tk = _tile(kdim, tk)
    nk = kdim // tk
    in_specs = [pl.BlockSpec((tm, tk), lambda b, i, k: (i, k)),
                pl.BlockSpec((1, tk, c), lambda b, i, k: rhs_map(b, k, tk))]
    args = [w, rhs]
    if row_scale is None:
        body = functools.partial(_ldft_kernel, nk=nk)
    else:
        body = functools.partial(_ldft_scaled_kernel, nk=nk)
        in_specs += [pl.BlockSpec((tm, 1), lambda b, i, k: (i, 0)),
                     pl.BlockSpec((1, c), lambda b, i, k: (0, 0))]
        args += [row_scale, col_sumsq]
    return pl.pallas_call(
        body,
        grid=(nb, m // tm, nk),
        in_specs=in_specs,
        out_specs=pl.BlockSpec((1, tm, c), lambda b, i, k: (b, i, 0)),
        out_shape=jax.ShapeDtypeStruct((nb, m, c), out_dtype),
        scratch_shapes=[pltpu.VMEM((tm, c), F32)],
        compiler_params=_cparams("parallel", "parallel", "arbitrary"),
    )(*args)


def _hy_pre_kernel(u_ref, up_ref, un_ref, cw_ref, cb_ref, uo_ref, x0_ref, *, nl, tl, hw):
    i = pl.program_id(1)
    u = u_ref[0].astype(F32)
    halo = up_ref.shape[1]
    prev = up_ref[0].astype(F32)[halo - 1:halo]
    nxt = un_ref[0].astype(F32)[0:1]
    prev = jnp.where(i == 0, 0.0, prev)
    nxt = jnp.where(i == nl - 1, 0.0, nxt)
    rows = lax.broadcasted_iota(I32, u.shape, 0)
    um = jnp.where(rows == 0, prev, pltpu.roll(u, 1, 0))
    up = jnp.where(rows == tl - 1, nxt, pltpu.roll(u, tl - 1, 0))
    cw = cw_ref[...]
    z = um * cw[0:1] + u * cw[1:2] + up * cw[2:3] + cb_ref[...]
    uo_ref[0] = (z[:, :hw] * z[:, hw:2 * hw]).astype(uo_ref.dtype)
    x0_ref[0] = z[:, 2 * hw:].astype(x0_ref.dtype)


def _hyena_pre(u, conv_w, conv_b, hw, out_dtype, tl=512):
    b, l, _ = u.shape
    tl = _tile(l, tl)
    nl = l // tl
    halo = 16
    hb = tl // halo
    nh = l // halo
    w3 = 3 * hw
    return pl.pallas_call(
        functools.partial(_hy_pre_kernel, nl=nl, tl=tl, hw=hw),
        grid=(b, nl),
        in_specs=[pl.BlockSpec((1, tl, w3), lambda bi, i: (bi, i, 0)),
                  pl.BlockSpec((1, halo, w3), lambda bi, i: (bi, jnp.maximum(i * hb - 1, 0), 0)),
                  pl.BlockSpec((1, halo, w3), lambda bi, i: (bi, jnp.minimum((i + 1) * hb, nh - 1), 0)),
                  pl.BlockSpec((3, w3), lambda bi, i: (0, 0)),
                  pl.BlockSpec((1, w3), lambda bi, i: (0, 0))],
        out_specs=[pl.BlockSpec((1, tl, hw), lambda bi, i: (bi, i, 0)),
                   pl.BlockSpec((1, tl, hw), lambda bi, i: (bi, i, 0))],
        out_shape=[jax.ShapeDtypeStruct((b, l, hw), out_dtype), jax.ShapeDtypeStruct((b, l, hw), out_dtype)],
        compiler_params=_cparams("parallel", "parallel"),
    )(u, u, u, conv_w, conv_b.reshape(1, w3))


def _hy_filter_kernel(z_ref, aux_ref, w0_ref, b0_ref, w1_ref, b1_ref, w2_ref, b2_ref, w3_ref, fr_ref, dl_ref,
                      k_ref, ss_ref, *, hw):
    i = pl.program_id(0)
    fr = fr_ref[...]
    h = jnp.sin(fr * (_dot3(z_ref[...], w0_ref[...]) + b0_ref[...]))
    h = jnp.sin(fr * (_dot3(h, w1_ref[...]) + b1_ref[...]))
    h = jnp.sin(fr * (_dot3(h, w2_ref[...]) + b2_ref[...]))
    h = _dot3(h, w3_ref[...])
    aux = aux_ref[...]
    t, m_fwd, m_bwd = aux[:, 0:1], aux[:, 1:2], aux[:, 2:3]
    k = (h[:, :hw] * m_fwd + h[:, hw:] * m_bwd) * jnp.exp(-t * dl_ref[...])
    k_ref[...] = k

    @pl.when(i == 0)
    def _():
        ss_ref[...] = jnp.zeros_like(ss_ref)

    ss_ref[...] += jnp.sum(k * k, axis=0, keepdims=True)


def _hyena_filter(l, hw, fw0, fb0, fw1, fb1, fw2, fb2, fw3, freq):
    n = 2 * l
    hid = HY_FILTER_HIDDEN
    r = jnp.arange(n)
    pos = jnp.where(r < l, r, n - r).astype(F32)
    t = pos / max(l - 1, 1)
    wv = (2.0 * math.pi / l) * pos
    bands = jnp.linspace(1e-4, HY_EMB_BANDS - 1, HY_EMB_BANDS, dtype=F32)
    ang = wv[:, None] * bands[None, :]
    ztab = jnp.concatenate([t[:, None], jnp.cos(ang), -jnp.sin(ang),
                            jnp.zeros((n, hid - 1 - 2 * HY_EMB_BANDS), F32)], axis=-1)
    aux = jnp.zeros((n, LANES), F32)
    aux = aux.at[:, 0].set(t).at[:, 1].set((r < l).astype(F32)).at[:, 2].set((r > l).astype(F32))
    w0 = jnp.concatenate([fw0, jnp.zeros((hid - fw0.shape[0], hid), F32)], axis=0)
    deltas = jnp.abs(jnp.linspace(math.log(HY_DECAY_TARGET) / HY_SLOW_DECAY,
                                  math.log(HY_DECAY_TARGET) / HY_FAST_DECAY, hw, dtype=F32)).reshape(1, hw)
    tr = _tile(n, 1024)
    full = lambda shape: pl.BlockSpec(shape, lambda i: (0, 0))
    return pl.pallas_call(
        functools.partial(_hy_filter_kernel, hw=hw),
        grid=(n // tr,),
        in_specs=[pl.BlockSpec((tr, hid), lambda i: (i, 0)),
                  pl.BlockSpec((tr, LANES), lambda i: (i, 0)),
                  full((hid, hid)), full((1, hid)), full((hid, hid)), full((1, hid)),
                  full((hid, hid)), full((1, hid)), full((hid, 2 * hw)), full((1, hid)), full((1, hw))],
        out_specs=[pl.BlockSpec((tr, hw), lambda i: (i, 0)), full((1, hw))],
        out_shape=[jax.ShapeDtypeStruct((n, hw), F32), jax.ShapeDtypeStruct((1, hw), F32)],
        compiler_params=_cparams("arbitrary"),
    )(ztab, aux, w0, fb0.reshape(1, hid), fw1, fb1.reshape(1, hid), fw2, fb2.reshape(1, hid), fw3,
      freq.reshape(1, hid), deltas)


def _hy_inv_kernel(wr_ref, wi_ref, ur_ref, ui_ref, kr_ref, ki_ref, u_ref, x0_ref, bias_ref, o_ref, acc_ref, *, nf):
    f = pl.program_id(2)

    @pl.when(f == 0)
    def _():
        acc_ref[...] = jnp.zeros_like(acc_ref)

    ur, ui, kr, ki = ur_ref[0], ui_ref[0], kr_ref[0], ki_ref[0]
    packed = jnp.logical_and(lax.broadcasted_iota(I32, ur.shape, 0) == 0, f == 0)
    yr = jnp.where(packed, ur * kr, ur * kr - ui * ki)
    yi = jnp.where(packed, ui * ki, ur * ki + ui * kr)
    acc_ref[...] += _dot(wr_ref[...], yr.astype(BF16)) + _dot(wi_ref[...], yi.astype(BF16))

    @pl.when(f == nf - 1)
    def _():
        u = u_ref[0].astype(F32)
        o_ref[0] = (x0_ref[0].astype(F32) * (acc_ref[...] + u * bias_ref[...])).astype(o_ref.dtype)


def _hyena_inverse(winv, uhat, kspec, u, x0, bias, tm=2048, tf=512):
    b, l, c = u.shape
    nfreq = uhat.shape[1] // 2
    tm = _tile(l, tm)
    tf = _tile(nfreq, tf)
    nf = nfreq // tf
    return pl.pallas_call(
        functools.partial(_hy_inv_kernel, nf=nf),
        grid=(b, l // tm, nf),
        in_specs=[pl.BlockSpec((tm, tf), lambda bi, i, f: (i, f)),
                  pl.BlockSpec((tm, tf), lambda bi, i, f: (i, nf + f)),
                  pl.BlockSpec((1, tf, c), lambda bi, i, f: (bi, f, 0)),
                  pl.BlockSpec((1, tf, c), lambda bi, i, f: (bi, nf + f, 0)),
                  pl.BlockSpec((1, tf, c), lambda bi, i, f: (0, f, 0)),
                  pl.BlockSpec((1, tf, c), lambda bi, i, f: (0, nf + f, 0)),
                  pl.BlockSpec((1, tm, c), lambda bi, i, f: (bi, i, 0)),
                  pl.BlockSpec((1, tm, c), lambda bi, i, f: (bi, i, 0)),
                  pl.BlockSpec((1, c), lambda bi, i, f: (0, 0))],
        out_specs=pl.BlockSpec((1, tm, c), lambda bi, i, f: (bi, i, 0)),
        out_shape=jax.ShapeDtypeStruct((b, l, c), BF16),
        scratch_shapes=[pltpu.VMEM((tm, c), F32)],
        compiler_params=_cparams("parallel", "parallel", "arbitrary"),
    )(winv, winv, uhat, uhat, kspec, kspec, u, x0, bias.reshape(1, c))


LONG_SEQ = 2048
FFT_N2 = 128
FNET_N2 = 64
K1_GROUP = 8
HY_K1_STEP = 4


def _cos_sin(m, n):
    ang = (2.0 * math.pi / n) * (m % n).astype(np.float64)
    return np.cos(ang).astype(np.float32), np.sin(ang).astype(np.float32)


def _rows(x):
    return x.reshape(-1, x.shape[-1]).astype(BF16)


def _per_sublane(t):
    return np.kron(t, np.eye(SUBLANES, dtype=np.float32)).astype(BF16)


def _stage1_kernel(*refs, n_in):
    tabs, ins, o_ref = refs[:n_in], refs[n_in:2 * n_in], refs[2 * n_in]
    acc = _dot(tabs[0][...], _rows(ins[0][0]))
    for t_ref, x_ref in zip(tabs[1:], ins[1:]):
        acc = acc + _dot(t_ref[...], _rows(x_ref[0]))
    o_ref[0] = acc.reshape(o_ref.shape[1:])


def _dft_stage1(tables, inputs, batch_maps, nb):
    k1 = tables[0].shape[0] // 2
    n2w, c = inputs[0].shape[2:]
    tables = [_per_sublane(t) for t in tables]
    in_specs = [pl.BlockSpec(t.shape, lambda p, j: (0, 0)) for t in tables]
    in_specs += [pl.BlockSpec((1, a.shape[1], SUBLANES, c), functools.partial(lambda p, j, m: (m(p), 0, j, 0), m=m))
                 for a, m in zip(inputs, batch_maps)]
    return pl.pallas_call(
        functools.partial(_stage1_kernel, n_in=len(inputs)),
        grid=(nb, n2w // SUBLANES),
        in_specs=in_specs,
        out_specs=pl.BlockSpec((1, 2, k1, SUBLANES, c), lambda p, j: (p, 0, 0, j, 0)),
        out_shape=jax.ShapeDtypeStruct((nb, 2, k1, n2w, c), F32),
        compiler_params=_cparams("parallel", "parallel"),
    )(*tables, *inputs)


def _fnet2_kernel(a_ref, g_ref, o_ref):
    o_ref[0] = _dot(g_ref[0], _rows(a_ref[0])).reshape(o_ref.shape[1:])


def _fnet_long(y_cos, y_sin):
    b, l, c = y_cos.shape
    n2w, n1w, kg = FNET_N2, l // FNET_N2, K1_GROUP
    idx = np.arange(n1w)
    c1, s1 = _cos_sin(idx[:, None] * idx[None, :], n1w)
    t_cos = np.concatenate([c1, -s1], axis=0)
    t_sin = np.concatenate([-s1, -c1], axis=0)
    view = lambda a: a.reshape(b, n1w, n2w, c)
    a = _dft_stage1([t_cos, t_sin], [view(y_cos), view(y_sin)], [lambda p: p, lambda p: p], b)
    grp = np.arange(n1w // kg)[:, None, None, None]
    k2 = np.arange(n2w)[None, :, None, None]
    j = np.arange(kg)[None, None, :, None]
    n2 = np.arange(n2w)[None, None, None, :]
    c2, s2 = _cos_sin((n1w * k2 + kg * grp + j) * n2, l)
    eye = np.eye(kg, dtype=np.float32)
    expand = lambda t: np.einsum('gkjn,ji->gkjin', t, eye).reshape(n1w // kg, n2w * kg, kg * n2w)
    gbig = (np.concatenate([expand(c2), expand(s2)], axis=2) * np.float32(1.0 / math.sqrt(l))).astype(BF16)
    out = pl.pallas_call(
        _fnet2_kernel,
        grid=(n1w // kg, b),
        in_specs=[pl.BlockSpec((1, 2, kg, n2w, c), lambda i, bi: (bi, 0, i, 0, 0)),
                  pl.BlockSpec((1,) + gbig.shape[1:], lambda i, bi: (i, 0, 0))],
        out_specs=pl.BlockSpec((1, n2w, kg, c), lambda i, bi: (bi, 0, i, 0)),
        out_shape=jax.ShapeDtypeStruct((b, n2w, n1w, c), F32),
        compiler_params=_cparams("parallel", "parallel"),
    )(a, gbig)
    return out.reshape(b, l, c)


def _stage2(g, a):
    return _dot(g, _rows(a))


def _spec_kernel(a_ref, g_ref, ss_ref, o_ref, *, scale):
    n2w = a_ref.shape[3]
    col = lax.rsqrt(ss_ref[...] + NORM_EPS) * scale
    for j in range(a_ref.shape[2]):
        o = _stage2(g_ref[j], a_ref[0, :, j]) * col
        o_ref[0, j] = o[:n2w]
        o_ref[1, j] = o[n2w:]


def _hy_mid_kernel(a_ref, g_ref, gh_ref, k_ref, o_ref):
    n2w = a_ref.shape[3]
    for j in range(a_ref.shape[2]):
        o = _stage2(g_ref[j], a_ref[0, :, j])
        o_r, o_i = o[:n2w], o[n2w:]
        k_r, k_i = k_ref[0, j], k_ref[1, j]
        y = jnp.concatenate([o_r * k_r - o_i * k_i, o_r * k_i + o_i * k_r], axis=0).astype(BF16)
        z = _dot(gh_ref[j], y)
        o_ref[0, 0, j] = z[:n2w]
        o_ref[0, 1, j] = z[n2w:]


def _hy_out_kernel(c_ref, t_ref, u_ref, x0_ref, bias_ref, o_ref):
    y = _dot(t_ref[...], _rows(c_ref[0])).reshape(o_ref.shape)
    o_ref[...] = x0_ref[...] * (y + u_ref[...] * bias_ref[...])


def _hyena_long(uu, x0, bias, k_raw, k_ss):
    b, l, c = uu.shape
    assert b % 2 == 0
    n = 2 * l
    n2w, n1w = FFT_N2, n // FFT_N2
    half, kb = n1w // 2, HY_K1_STEP
    k1 = np.arange(n1w)
    c1, s1 = _cos_sin(k1[:, None] * k1[None, :], n1w)
    bf = lambda t: t.astype(BF16)
    k2 = np.arange(n2w)
    c2, s2 = _cos_sin((n1w * k2[None, :, None] + k1[:, None, None]) * k2[None, None, :], n)
    g = bf(np.concatenate([np.concatenate([c2, s2], axis=2), np.concatenate([-s2, c2], axis=2)], axis=1))
    c2t, s2t = np.swapaxes(c2, 1, 2), np.swapaxes(s2, 1, 2)
    gh = bf(np.concatenate([np.concatenate([c2t, -s2t], axis=2), np.concatenate([s2t, c2t], axis=2)], axis=1))

    ak = _dft_stage1([np.concatenate([c1, -s1], axis=0)], [k_raw.reshape(1, n1w, n2w, c)], [lambda p: 0], 1)
    kspec = pl.pallas_call(
        functools.partial(_spec_kernel, scale=1.0 / n),
        grid=(n1w // kb,),
        in_specs=[pl.BlockSpec((1, 2, kb, n2w, c), lambda i: (0, 0, i, 0, 0)),
                  pl.BlockSpec((kb, 2 * n2w, 2 * n2w), lambda i: (i, 0, 0)),
                  pl.BlockSpec((1, c), lambda i: (0, 0))],
        out_specs=pl.BlockSpec((2, kb, n2w, c), lambda i: (0, i, 0, 0)),
        out_shape=jax.ShapeDtypeStruct((2, n1w, n2w, c), F32),
        compiler_params=_cparams("parallel"),
    )(ak, g, k_ss)

    ch, sh = c1[:, :half], s1[:, :half]
    view = lambda a: a.reshape(b, half, n2w, c)
    a = _dft_stage1([np.concatenate([ch, -sh], axis=0), np.concatenate([sh, ch], axis=0)],
                    [view(uu), view(uu)], [lambda p: 2 * p, lambda p: 2 * p + 1], b // 2)
    z = pl.pallas_call(
        _hy_mid_kernel,
        grid=(n1w // kb, b // 2),
        in_specs=[pl.BlockSpec((1, 2, kb, n2w, c), lambda i, p: (p, 0, i, 0, 0)),
                  pl.BlockSpec((kb, 2 * n2w, 2 * n2w), lambda i, p: (i, 0, 0)),
                  pl.BlockSpec((kb, 2 * n2w, 2 * n2w), lambda i, p: (i, 0, 0)),
                  pl.BlockSpec((2, kb, n2w, c), lambda i, p: (0, i, 0, 0))],
        out_specs=pl.BlockSpec((1, 2, kb, n2w, c), lambda i, p: (p, 0, i, 0, 0)),
        out_shape=jax.ShapeDtypeStruct((b // 2, 2, n1w, n2w, c), F32),
        compiler_params=_cparams("parallel", "parallel"),
    )(a, g, gh, kspec)
    cht, sht = ch.T, sh.T
    t_inv = _per_sublane(
        np.concatenate([np.concatenate([cht, -sht], axis=1), np.concatenate([sht, cht], axis=1)], axis=0))
    pair = pl.BlockSpec((2, half, SUBLANES, c), lambda p, j: (p, 0, j, 0))
    out = pl.pallas_call(
        _hy_out_kernel,
        grid=(b // 2, n2w // SUBLANES),
        in_specs=[pl.BlockSpec((1, 2, n1w, SUBLANES, c), lambda p, j: (p, 0, 0, j, 0)),
                  pl.BlockSpec(t_inv.shape, lambda p, j: (0, 0)),
                  pair, pair, pl.BlockSpec((1, c), lambda p, j: (0, 0))],
        out_specs=pair,
        out_shape=jax.ShapeDtypeStruct((b, half, n2w, c), F32),
        compiler_params=_cparams("parallel", "parallel"),
    )(z, t_inv, view(uu), view(x0), bias.reshape(1, c))
    return out.reshape(b, l, c)


def _mixer_residual(x_ref, part_refs, wo_ref, gate_ref):
    y, row = None, 0
    for p_ref in part_refs:
        n = p_ref.shape[2]
        t = _dot(p_ref[0].astype(BF16), wo_ref[row:row + n])
        y = t if y is None else y + t
        row += n
    return x_ref[0] + gate_ref[0] * y


def _proj_res_kernel(x_ref, a_ref, wo_ref, gate_ref, o_ref):
    o_ref[0] = _mixer_residual(x_ref, [a_ref], wo_ref, gate_ref)


def _proj_residual(x, a, w_out, gate, tm=512):
    b, l, d = x.shape
    tm = _tile(l, tm)
    tok = lambda n: pl.BlockSpec((1, tm, n), lambda bi, i: (bi, i, 0))
    return pl.pallas_call(
        _proj_res_kernel,
        grid=(b, l // tm),
        in_specs=[tok(d), tok(a.shape[2]), pl.BlockSpec(w_out.shape, lambda bi, i: (0, 0)),
                  pl.BlockSpec((1, 1, d), lambda bi, i: (bi, 0, 0))],
        out_specs=tok(d),
        out_shape=jax.ShapeDtypeStruct((b, l, d), F32),
        compiler_params=_cparams("parallel", "parallel"),
    )(x, a, w_out, gate)


def _ffn_kernel(*refs, n_parts):
    x_ref, part_refs = refs[0], refs[1:1 + n_parts]
    wo_ref, gate1_ref, g_ref, sh_ref, sc_ref, gate_ref, wg_ref, wu_ref, wd_ref, o_ref = refs[1 + n_parts:]
    x = _mixer_residual(x_ref, part_refs, wo_ref, gate1_ref)
    h = _norm_mod(x, g_ref[...], sh_ref[0], sc_ref[0]).astype(BF16)
    mid = _silu(_dot(h, wg_ref[...])) * _dot(h, wu_ref[...])
    o_ref[0] = x + gate_ref[0] * _dot(mid.astype(BF16), wd_ref[...])


def _mixer_out_ffn(x, parts, w_out, gate1, g, shift, scale, gate, wg, wu, wd, tm=512):
    b, l, d = x.shape
    tm = _tile(l, tm)
    vec = pl.BlockSpec((1, 1, d), lambda bi, i: (bi, 0, 0))
    tok = lambda n: pl.BlockSpec((1, tm, n), lambda bi, i: (bi, i, 0))
    resident = lambda w: pl.BlockSpec(w.shape, lambda bi, i: (0, 0), pipeline_mode=pl.Buffered(1))
    return pl.pallas_call(
        functools.partial(_ffn_kernel, n_parts=len(parts)),
        grid=(b, l // tm),
        in_specs=[tok(d)] + [tok(p.shape[2]) for p in parts]
        + [resident(w_out), vec, pl.BlockSpec((1, d), lambda bi, i: (0, 0)), vec, vec, vec,
           resident(wg), resident(wu), resident(wd)],
        out_specs=tok(d),
        out_shape=jax.ShapeDtypeStruct((b, l, d), F32),
        compiler_params=_cparams("parallel", "parallel"),
    )(x, *parts, w_out, gate1, g.reshape(1, d), shift, scale, gate, wg, wu, wd)


def _head_norm(t, e, et, g_full):
    ss = _dot((t * t).astype(BF16), e)
    rinv = lax.rsqrt(ss * (1.0 / HEAD_DIM) + NORM_EPS)
    hi, lo = _split(rinv)
    return t * (_dot(hi, et) + _dot(lo, et)) * g_full


def _rope(t, cos, sin_lo, sin_hi):
    w = t.shape[1]
    rep = w // LANES
    tile = lambda a: jnp.concatenate([a] * rep, axis=1)
    return (t * tile(cos) + pltpu.roll(t, w - ROPE_HALF, 1) * tile(sin_lo)
            + pltpu.roll(t, ROPE_HALF, 1) * tile(sin_hi))


def _qkv_kernel(x_ref, g_ref, sh_ref, sc_ref, w_ref, e_ref, et_ref, qg_ref, kg_ref, cos_ref, sl_ref, sh2_ref,
                *out_refs, qw, kw, rope):
    k_ref, v_ref = out_refs[-2:]
    h = _norm_mod(x_ref[0], g_ref[...], sh_ref[0], sc_ref[0])
    t = _dot(h.astype(BF16), w_ref[...])
    e, et = e_ref[...], et_ref[...]
    k = _head_norm(t[:, qw:qw + kw], e[:kw], et[:, :kw], kg_ref[...])
    if rope:
        k = _rope(k, cos_ref[...], sl_ref[...], sh2_ref[...])
    k_ref[0] = k.astype(BF16)
    v_ref[0] = t[:, qw + kw:].astype(BF16)
    if qw:
        q = _head_norm(t[:, :qw], e, et, qg_ref[...])
        q = _rope(q, cos_ref[...], sl_ref[...], sh2_ref[...]) * (HEAD_DIM ** -0.5)
        out_refs[0][0] = q.astype(BF16)


def _head_tables(qw):
    lane = np.arange(qw)
    e = (lane[:, None] // HEAD_DIM == np.arange(LANES)[None, :]).astype(BF16)
    return e, np.ascontiguousarray(e.T)


def _rope_tables(l):
    rows = l // GRID_W
    row = np.repeat(np.arange(rows, dtype=np.float32), GRID_W)
    col = np.tile(np.arange(GRID_W, dtype=np.float32), rows)
    inv = (ROPE_THETA ** (-np.arange(0, AXIS_ROPE_DIM, 2, dtype=np.float32) / AXIS_ROPE_DIM)).astype(np.float32)
    lane = np.arange(LANES)
    in_head = lane % HEAD_DIM
    use_col = (in_head // AXIS_ROPE_DIM) == 1
    hi_half = ((in_head % AXIS_ROPE_DIM) // ROPE_HALF) == 1
    freq = inv[in_head % ROPE_HALF]
    ang = (np.where(use_col[None, :], col[:, None], row[:, None]) * freq[None, :]).astype(np.float32)
    cos, sin = np.cos(ang), np.sin(ang)
    zero = np.float32(0.0)
    return cos, np.where(hi_half[None, :], zero, -sin), np.where(hi_half[None, :], sin, zero)


def _qkv_project(x, g, shift, scale, w, q_g, k_g, qw, kw, rope, tm=512):
    b, l, d = x.shape
    tm = _tile(l, tm)
    e, et = _head_tables(max(qw, kw))
    n_q = max(qw, kw) // HEAD_DIM
    qg = jnp.tile(q_g, n_q).reshape(1, -1)
    kg = jnp.tile(k_g, kw // HEAD_DIM).reshape(1, kw)
    if rope:
        cos, s_lo, s_hi = _rope_tables(l)
    else:
        cos = s_lo = s_hi = jnp.zeros((l, LANES), F32)
    vec = pl.BlockSpec((1, 1, d), lambda bi, i: (bi, 0, 0))
    full = lambda a: pl.BlockSpec(a.shape, lambda bi, i: (0,) * a.ndim)
    tab = pl.BlockSpec((tm, LANES), lambda bi, i: (i, 0))
    widths = ([qw] if qw else []) + [kw, kw]
    return pl.pallas_call(
        functools.partial(_qkv_kernel, qw=qw, kw=kw, rope=rope),
        grid=(b, l // tm),
        in_specs=[pl.BlockSpec((1, tm, d), lambda bi, i: (bi, i, 0)),
                  pl.BlockSpec((1, d), lambda bi, i: (0, 0)), vec, vec,
                  full(w), full(e), full(et), full(qg), full(kg), tab, tab, tab],
        out_specs=[pl.BlockSpec((1, tm, n), lambda bi, i: (bi, i, 0)) for n in widths],
        out_shape=[jax.ShapeDtypeStruct((b, l, n), BF16) for n in widths],
        compiler_params=_cparams("parallel", "parallel"),
    )(x, g.reshape(1, d), shift, scale, w, e, et, qg, kg, cos, s_lo, s_hi)


def _attn_kernel(q_ref, kp_ref, kc_ref, kn_ref, vp_ref, vc_ref, vn_ref, kx_ref, vx_ref, sink_ref, o_ref, *,
                 seq, group):
    bq = BLOCK_Q
    nt = ATT_BLOCKS
    lc = kx_ref.shape[1]
    nkeys = 3 * bq + lc
    rows = lax.broadcasted_iota(I32, (group * bq, nkeys), 0)
    cols = lax.broadcasted_iota(I32, (group * bq, nkeys), 1)
    for t in range(nt):
        qb = nt * pl.program_id(1) + t
        own = slice(t * bq, (t + 1) * bq)
        qpos = qb * bq + rows % bq
        kpos = (qb - 1) * bq + cols
        valid = jnp.logical_or(
            cols >= 3 * bq,
            jnp.logical_and(jnp.logical_and(kpos >= 0, kpos < seq), jnp.abs(qpos - kpos) <= WINDOW))
        for h in range(N_KV_HEADS):
            ks = slice(h * HEAD_DIM, (h + 1) * HEAD_DIM)

            def band(p_ref, c_ref, n_ref, x_ref):
                before = p_ref[0, :, ks] if t == 0 else c_ref[0, (t - 1) * bq:t * bq, ks]
                after = c_ref[0, (t + 1) * bq:(t + 2) * bq, ks] if t < nt - 1 else n_ref[0, :, ks]
                return jnp.concatenate([before, c_ref[0, own, ks], after, x_ref[0, :, ks]], axis=0)

            kh = band(kp_ref, kc_ref, kn_ref, kx_ref)
            vh = band(vp_ref, vc_ref, vn_ref, vx_ref)
            qh = jnp.concatenate(
                [q_ref[0, own, (h * group + g) * HEAD_DIM:(h * group + g + 1) * HEAD_DIM] for g in range(group)],
                axis=0)
            s = lax.dot_general(qh, kh, (((1,), (1,)), ((), ())), preferred_element_type=F32)
            s = jnp.where(valid, s, NEG_INF)
            sk = sink_ref[h]
            m = jnp.maximum(jnp.max(s, axis=1, keepdims=True), sk)
            p = jnp.exp(s - m)
            den = jnp.sum(p, axis=1, keepdims=True) + jnp.exp(sk - m)
            o = _dot(p.astype(BF16), vh) / den
            for g in range(group):
                hq = h * group + g
                o_ref[0, own, hq * HEAD_DIM:(hq + 1) * HEAD_DIM] = o[g * bq:(g + 1) * bq].astype(o_ref.dtype)


def _window_attention(q, k, v, kx, vx, sink):
    b, l, qw = q.shape
    kw = k.shape[2]
    lc = kx.shape[1]
    bq = BLOCK_Q
    nb = l // bq
    nt = ATT_BLOCKS
    assert nb % nt == 0
    group = qw // kw
    sink_tab = jnp.repeat(sink.astype(F32).reshape(N_KV_HEADS, group), bq, axis=1)[..., None]
    kv_prev = pl.BlockSpec((1, bq, kw), lambda bi, i: (bi, jnp.maximum(nt * i - 1, 0), 0))
    kv_pair = pl.BlockSpec((1, nt * bq, kw), lambda bi, i: (bi, i, 0))
    kv_next = pl.BlockSpec((1, bq, kw), lambda bi, i: (bi, jnp.minimum(nt * i + nt, nb - 1), 0))
    kv_ctx = pl.BlockSpec((1, lc, kw), lambda bi, i: (bi, 0, 0))
    return pl.pallas_call(
        functools.partial(_attn_kernel, seq=l, group=group),
        grid=(b, nb // nt),
        in_specs=[pl.BlockSpec((1, nt * bq, qw), lambda bi, i: (bi, i, 0)),
                  kv_prev, kv_pair, kv_next, kv_prev, kv_pair, kv_next, kv_ctx, kv_ctx,
                  pl.BlockSpec(sink_tab.shape, lambda bi, i: (0, 0, 0))],
        out_specs=pl.BlockSpec((1, nt * bq, qw), lambda bi, i: (bi, i, 0)),
        out_shape=jax.ShapeDtypeStruct((b, l, qw), BF16),
        compiler_params=_cparams("parallel", "parallel"),
    )(q, k, k, k, v, v, v, kx, vx, sink_tab)


def _router_kernel(x_ref, g_ref, sh_ref, sc_ref, wh_ref, wl_ref, tri_ref, h_ref, gate_ref, rank_ref, rankt_ref,
                   cnt_ref):
    h = _norm_mod(x_ref[0], g_ref[...], sh_ref[0], sc_ref[0])
    h_ref[0] = h.astype(BF16)
    hi, lo = _split(h)
    logits = _dot(hi, wh_ref[...]) + _dot(hi, wl_ref[...]) + _dot(lo, wh_ref[...])
    lane = lax.broadcasted_iota(I32, logits.shape, 1)
    logits = jnp.where(lane < N_EXPERTS, logits, -jnp.inf)
    m1 = jnp.max(logits, axis=1, keepdims=True)
    i1 = jnp.min(jnp.where(logits == m1, lane, LANES), axis=1, keepdims=True)
    rest = jnp.where(lane == i1, -jnp.inf, logits)
    m2 = jnp.max(rest, axis=1, keepdims=True)
    i2 = jnp.min(jnp.where(rest == m2, lane, LANES), axis=1, keepdims=True)
    e = jnp.exp(m2 - m1)
    g1 = 1.0 / (1.0 + e)
    g2 = e / (1.0 + e)
    pick1, pick2 = lane == i1, lane == i2
    member = jnp.logical_or(pick1, pick2)
    gate_ref[0] = jnp.where(pick1, g1, jnp.where(pick2, g2, 0.0))
    m = jnp.where(member, 1.0, 0.0)
    rank = jnp.where(member, _dot(tri_ref[...], m.astype(BF16)), -1.0)
    rank_ref[0] = rank.astype(I32)
    rankt_ref[0] = jnp.transpose(rank)[:SUBLANES].astype(I32)
    cnt_ref[0] = jnp.sum(m, axis=0, keepdims=True).astype(I32)


def _route(x, g, shift, scale, w_router):
    b, l, d = x.shape
    tm = _tile(l, MOE_TILE)
    nt = l // tm
    wr = jnp.concatenate([w_router, jnp.zeros((d, LANES - w_router.shape[1]), F32)], axis=1)
    wh, wl = _split(wr)
    tri = np.tril(np.ones((tm, tm), np.float32), -1).astype(BF16)
    vec = pl.BlockSpec((1, 1, d), lambda bi, i: (bi, 0, 0))
    tok = lambda n: pl.BlockSpec((1, tm, n), lambda bi, i: (bi, i, 0))
    return pl.pallas_call(
        _router_kernel,
        grid=(b, nt),
        in_specs=[tok(d), pl.BlockSpec((1, d), lambda bi, i: (0, 0)), vec, vec,
                  pl.BlockSpec((d, LANES), lambda bi, i: (0, 0)),
                  pl.BlockSpec((d, LANES), lambda bi, i: (0, 0)),
                  pl.BlockSpec((tm, tm), lambda bi, i: (0, 0))],
        out_specs=[tok(d), tok(LANES), tok(LANES),
                   pl.BlockSpec((1, SUBLANES, tm), lambda bi, i: (bi * nt + i, 0, 0)),
                   pl.BlockSpec((1, 1, LANES), lambda bi, i: (bi * nt + i, 0, 0))],
        out_shape=[jax.ShapeDtypeStruct((b, l, d), BF16),
                   jax.ShapeDtypeStruct((b, l, LANES), F32),
                   jax.ShapeDtypeStruct((b, l, LANES), I32),
                   jax.ShapeDtypeStruct((b * nt, SUBLANES, tm), I32),
                   jax.ShapeDtypeStruct((b * nt, 1, LANES), I32)],
        compiler_params=_cparams("parallel", "parallel"),
    )(x, g.reshape(1, d), shift, scale, wh, wl, tri)


def _slot_layout(cnt, rows):
    nt = cnt.shape[0]
    seg = (cnt + SUBLANES - 1) // SUBLANES * SUBLANES
    padded = (jnp.sum(seg, axis=0) + rows - 1) // rows * rows
    stride = padded + rows
    pstart = jnp.cumsum(stride) - stride
    off = pstart[None, :] + jnp.cumsum(seg, axis=0) - seg
    n_slots = (nt * N_EXPERTS * (SUBLANES - 1) + nt * MOE_TILE * TOP_K + rows - 1) // rows * rows \
        + 2 * N_EXPERTS * rows
    nblk = n_slots // rows
    bstart = jnp.arange(nblk, dtype=I32) * rows
    be = jnp.minimum(jnp.sum((bstart[:, None] >= (pstart + stride)[None, :]).astype(I32), axis=1), N_EXPERTS - 1)
    onehot = (be[:, None] == jnp.arange(N_EXPERTS, dtype=I32)[None, :]).astype(I32)
    lo = jnp.sum(onehot * pstart[None, :], axis=1)
    hi = jnp.sum(onehot * (pstart + padded)[None, :], axis=1)
    used = jnp.logical_and(bstart >= lo, bstart < hi)
    zero_blk = jnp.logical_or(jnp.logical_not(used), bstart == hi - rows)
    as_i32 = lambda a: a.reshape(-1).astype(I32)
    return n_slots, as_i32(off), as_i32(cnt), as_i32(zero_blk), as_i32(be), as_i32(used)


def _seg_copy(src_ref, dst_ref, row, sem):
    n = src_ref.shape[0]
    return pltpu.make_async_copy(src_ref, dst_ref.at[pl.ds(pl.multiple_of(row, SUBLANES), n)], sem)


def _chunk_copy(src_ref, dst_ref, row, sem):
    n = dst_ref.shape[0]
    return pltpu.make_async_copy(src_ref.at[pl.ds(pl.multiple_of(row, SUBLANES), n)], dst_ref, sem)


def _dispatch_kernel(off_ref, cnt_ref, zero_ref, h_ref, rt_ref, xs_ref, xbuf, xbuf2, sem, sem2, *, rows, n_tiles):
    i = pl.program_id(0)
    cap = xbuf.shape[2]
    tm = h_ref.shape[1]
    slot = i % 2

    @pl.when(i == 0)
    def _():
        xbuf[0, 0] = jnp.zeros(xbuf.shape[2:], F32)

        def zero_block(j, carry):
            @pl.when(zero_ref[j] > 0)
            def _():
                for part in range(rows // cap):
                    c = _seg_copy(xbuf.at[0, 0], xs_ref, j * rows + part * cap, sem2)
                    c.start()
                    c.wait()
            return carry

        lax.fori_loop(0, zero_ref.shape[0], zero_block, 0)

    h = h_ref[0]
    rt = rt_ref[0]
    riota = lax.broadcasted_iota(I32, (cap, tm), 0)

    def copy(t, s, e):
        return _seg_copy(xbuf.at[s, e], xs_ref, off_ref[t * N_EXPERTS + e], sem.at[s, e])

    for e in range(N_EXPERTS):
        sel = rt[e:e + 1, :]

        def segment(base):
            return _dot(jnp.where(sel == riota + base, 1.0, 0.0).astype(BF16), h)

        @pl.when(i > 0)
        def _():
            copy(i - 1, 1 - slot, e).wait()

        xbuf[slot, e] = segment(0)
        copy(i, slot, e).start()

        @pl.when(cnt_ref[i * N_EXPERTS + e] > cap)
        def _():
            xbuf2[...] = segment(cap)
            c = _seg_copy(xbuf2, xs_ref, off_ref[i * N_EXPERTS + e] + cap, sem2)
            c.start()
            c.wait()

    @pl.when(i == n_tiles - 1)
    def _():
        for e in range(N_EXPERTS):
            copy(i, slot, e).wait()


def _dispatch(h, rankt, n_slots, off, cnt, zero_blk, rows):
    b, l, d = h.shape
    tm = _tile(l, MOE_TILE)
    nt = l // tm
    cap = MOE_CAP
    assert tm <= 2 * cap and rows % cap == 0
    grid_spec = pltpu.PrefetchScalarGridSpec(
        num_scalar_prefetch=3,
        grid=(b * nt,),
        in_specs=[pl.BlockSpec((1, tm, d), lambda i, *_: (i // nt, i % nt, 0)),
                  pl.BlockSpec((1, SUBLANES, tm), lambda i, *_: (i, 0, 0))],
        out_specs=pl.BlockSpec(memory_space=pl.ANY),
        scratch_shapes=[pltpu.VMEM((2, N_EXPERTS, cap, d), F32), pltpu.VMEM((cap, d), F32),
                        pltpu.SemaphoreType.DMA((2, N_EXPERTS)), pltpu.SemaphoreType.DMA(())],
    )
    return pl.pallas_call(
        functools.partial(_dispatch_kernel, rows=rows, n_tiles=b * nt),
        grid_spec=grid_spec,
        out_shape=jax.ShapeDtypeStruct((n_slots, d), F32),
        compiler_params=_cparams("arbitrary"),
    )(off, cnt, zero_blk, h, rankt)


def _moe_kernel(be_ref, used_ref, x_ref, wg_ref, wu_ref, wd_ref, o_ref):
    i = pl.program_id(0)

    @pl.when(used_ref[i] > 0)
    def _():
        x = x_ref[...].astype(BF16)
        mid = _silu(_dot(x, wg_ref[0])) * _dot(x, wu_ref[0])
        o_ref[...] = _dot(mid.astype(BF16), wd_ref[0])

    @pl.when(used_ref[i] == 0)
    def _():
        o_ref[...] = jnp.zeros_like(o_ref)


def _expert_ffn(xs, block_e, used, wg, wu, wd, rows):
    s, d = xs.shape
    nblk = s // rows
    expert = lambda w: pl.BlockSpec((1,) + w.shape[1:], lambda i, be, us: (be[i], 0, 0),
                                    pipeline_mode=pl.Buffered(1))
    grid_spec = pltpu.PrefetchScalarGridSpec(
        num_scalar_prefetch=2,
        grid=(nblk,),
        in_specs=[pl.BlockSpec((rows, d), lambda i, be, us: (i, 0)), expert(wg), expert(wu), expert(wd)],
        out_specs=pl.BlockSpec((rows, d), lambda i, be, us: (i, 0)),
    )
    return pl.pallas_call(
        _moe_kernel,
        grid_spec=grid_spec,
        out_shape=jax.ShapeDtypeStruct((s, d), F32),
        compiler_params=_cparams("arbitrary"),
    )(block_e, used, xs, wg, wu, wd)


def _combine_kernel(off_ref, cnt_ref, x_ref, gate_ref, rg_ref, rank_ref, ys_ref, o_ref, buf, buf2, acc_ref, sem,
                    sem2, *, n_tiles):
    i = pl.program_id(0)
    slot = i % 2
    cap = buf.shape[2]
    tm = x_ref.shape[1]

    def chunk(t, s, e):
        return _chunk_copy(ys_ref, buf.at[s, e], off_ref[t * N_EXPERTS + e], sem.at[s, e])

    @pl.when(i == 0)
    def _():
        for e in range(N_EXPERTS):
            chunk(0, 0, e).start()

    @pl.when(i + 1 < n_tiles)
    def _():
        for e in range(N_EXPERTS):
            chunk(i + 1, 1 - slot, e).start()

    rank = rank_ref[0]
    rg = rg_ref[0]
    liota = lax.broadcasted_iota(I32, (tm, cap), 1)

    def picked(e, base, rows_ref):
        q = jnp.where(rank[:, e:e + 1] == liota + base, 1.0, 0.0).astype(BF16)
        return rg[:, e:e + 1] * _dot(q, rows_ref[...].astype(BF16))

    y = jnp.zeros(acc_ref.shape, F32)
    for e in range(N_EXPERTS):
        chunk(i, slot, e).wait()
        y = y + picked(e, 0, buf.at[slot, e])
    acc_ref[...] = y

    for e in range(N_EXPERTS):
        @pl.when(cnt_ref[i * N_EXPERTS + e] > cap)
        def _():
            c = _chunk_copy(ys_ref, buf2, off_ref[i * N_EXPERTS + e] + cap, sem2)
            c.start()
            c.wait()
            acc_ref[...] += picked(e, cap, buf2)

    o_ref[0] = x_ref[0] + gate_ref[0] * acc_ref[...]


def _moe_combine(x, gate, route_gates, rank, ys, off, cnt):
    b, l, d = x.shape
    tm = _tile(l, MOE_TILE)
    nt = l // tm
    cap = MOE_CAP
    tok = lambda n: pl.BlockSpec((1, tm, n), lambda i, *_: (i // nt, i % nt, 0))
    grid_spec = pltpu.PrefetchScalarGridSpec(
        num_scalar_prefetch=2,
        grid=(b * nt,),
        in_specs=[tok(d), pl.BlockSpec((1, 1, d), lambda i, *_: (i // nt, 0, 0)), tok(LANES), tok(LANES),
                  pl.BlockSpec(memory_space=pl.ANY)],
        out_specs=tok(d),
        scratch_shapes=[pltpu.VMEM((2, N_EXPERTS, cap, d), F32), pltpu.VMEM((cap, d), F32),
                        pltpu.VMEM((tm, d), F32),
                        pltpu.SemaphoreType.DMA((2, N_EXPERTS)), pltpu.SemaphoreType.DMA(())],
    )
    return pl.pallas_call(
        functools.partial(_combine_kernel, n_tiles=b * nt),
        grid_spec=grid_spec,
        out_shape=jax.ShapeDtypeStruct((b, l, d), F32),
        compiler_params=_cparams("arbitrary"),
    )(off, cnt, x, gate, route_gates, rank, ys)


def _angles(rows, cols, n):
    m = (rows[:, None] * cols[None, :]) % n
    return (2.0 * math.pi / n) * m.astype(F32)


def _fnet_table(l):
    idx = jnp.arange(l, dtype=I32)
    ang = _angles(idx, idx, l)
    s = 1.0 / math.sqrt(l)
    return jnp.concatenate([jnp.cos(ang) * s, jnp.sin(ang) * (-s)], axis=1).astype(BF16)


def _rfft_table(n):
    half = n // 2
    f = jnp.arange(half, dtype=I32)
    t = jnp.arange(n, dtype=I32)
    ang = _angles(f, t, n)
    top = jnp.cos(ang)
    bot = -jnp.sin(ang)
    nyq = jnp.cos(_angles(jnp.full((1,), half, I32), t, n))
    bot = jnp.concatenate([nyq, bot[1:]], axis=0)
    return jnp.concatenate([top, bot], axis=0).astype(BF16)


def _group_dft_table(width):
    gd = FNET_GROUP_DIM
    idx = jnp.arange(gd, dtype=I32)
    ang = _angles(idx, idx, gd)
    s = 1.0 / math.sqrt(gd)
    eye = jnp.eye(width // gd, dtype=F32)
    return jnp.concatenate([jnp.kron(eye, jnp.cos(ang) * s), jnp.kron(eye, jnp.sin(ang) * s)], axis=1)


def _even_mixer(x, g1, shift, scale, w_in_f, conv_w, conv_b, hy_bias, filt, fw, hw):
    b, l, d = x.shape
    n = 2 * l
    long_seq = l >= LONG_SEQ
    seq_dtype = F32 if long_seq else BF16
    u, y_cos, y_sin = _norm_mod_matmul(x, g1, shift, scale, w_in_f, [3 * hw, fw, fw], [BF16, seq_dtype, seq_dtype])
    uu, x0 = _hyena_pre(u, conv_w, conv_b, hw, seq_dtype)
    k_raw, k_ss = filt(l)
    if long_seq:
        a = _fnet_long(y_cos, y_sin)
        hy = _hyena_long(uu, x0, hy_bias, k_raw, k_ss)
    else:
        y_both = jnp.concatenate([y_cos, y_sin], axis=2)
        a = _left_dft(_fnet_table(l), y_both, lambda bi, k, tk: (bi, k % (l // tk), k // (l // tk)), n, fw, BF16,
                      tk=min(l, 1024))
        wf = _rfft_table(n)
        wts = jnp.concatenate([jnp.ones((1,), F32), jnp.full((l - 1,), 2.0, F32)]) / n
        row_scale = jnp.concatenate([wts, wts]).reshape(n, 1)
        kspec = _left_dft(wf, k_raw[None], lambda bi, k, tk: (0, k, 0), n, hw, F32,
                          row_scale=row_scale, col_sumsq=k_ss)
        uhat = _left_dft(wf, uu, lambda bi, k, tk: (bi, k, 0), l, hw, F32)
        hy = _hyena_inverse(wf[:, :l].T, uhat, kspec, uu, x0, hy_bias)
    return a, hy


def kernel(x, c, ctx, c_ctx, ada_w, ada_b, norm1_g, norm2_g, ev_w_in, ev_w_out, hy_conv_w, hy_conv_b, hy_bias,
           hf_w0, hf_b0, hf_w1, hf_b1, hf_w2, hf_b2, hf_w3, hf_freq, ffn_w_gate, ffn_w_up, ffn_w_down, od_w_qkv,
           od_w_out, q_norm_g, k_norm_g, attn_sink, moe_router, moe_w_gate, moe_w_up, moe_w_down):
    b, l, d = x.shape
    lc = ctx.shape[1]
    assert ada_w.shape[0] == 2, "this implementation covers the two-layer (even, odd) stack"
    hw = hy_bias.shape[1]
    fw = ev_w_in.shape[2] - 3 * hw
    qw = od_w_out.shape[1]
    kw = (od_w_qkv.shape[2] - qw) // 2

    rows = (b + 1 + 7) // 8 * 8
    cc = jnp.concatenate([c, c_ctx[None, :], jnp.zeros((rows - b - 1, d), F32)], axis=0)
    mod = _ada_vectors(cc, ada_w, ada_b)
    ml = [[mod[i, :b, None, m * d:(m + 1) * d] for m in range(N_MOD)] for i in range(2)]
    mc = [[mod[i, b:b + 1, None, m * d:(m + 1) * d] for m in range(N_MOD)] for i in range(2)]

    w_in = ev_w_in[0]
    w_fnet = _matmul3(w_in[:, :fw], _group_dft_table(fw))
    w_in_f = jnp.concatenate([w_in[:, fw:], w_fnet], axis=1).astype(BF16)
    w_out0 = ev_w_out[0].astype(BF16)
    filt = lambda seq: _hyena_filter(seq, hw, hf_w0[0], hf_b0[0], hf_w1[0], hf_b1[0], hf_w2[0], hf_b2[0],
                                     hf_w3[0], hf_freq[0])
    wg, wu, wd = ffn_w_gate[0].astype(BF16), ffn_w_up[0].astype(BF16), ffn_w_down[0].astype(BF16)

    mix = _even_mixer(x, norm1_g[0], ml[0][0], ml[0][1], w_in_f, hy_conv_w[0], hy_conv_b[0], hy_bias[0], filt, fw,
                      hw)
    x = _mixer_out_ffn(x, mix, w_out0, ml[0][2], norm2_g[0], ml[0][3], ml[0][4], ml[0][5], wg, wu, wd)

    bc = lambda v: jnp.broadcast_to(v, (b, 1, d))
    mix = _even_mixer(ctx, norm1_g[0], bc(mc[0][0]), bc(mc[0][1]), w_in_f, hy_conv_w[0], hy_conv_b[0], hy_bias[0],
                      filt, fw, hw)
    flat = lambda t: t.reshape(1, b * lc, t.shape[2])
    ctx = _mixer_out_ffn(flat(ctx), [flat(t) for t in mix], w_out0, mc[0][2], norm2_g[0], mc[0][3], mc[0][4],
                         mc[0][5], wg, wu, wd)

    w_qkv = od_w_qkv[0].astype(BF16)
    q, k, v = _qkv_project(x, norm1_g[1], ml[1][0], ml[1][1], w_qkv, q_norm_g[0], k_norm_g[0], qw, kw, True)
    kx, vx = _qkv_project(ctx, norm1_g[1], mc[1][0], mc[1][1], w_qkv[:, qw:], q_norm_g[0], k_norm_g[0], 0, kw,
                          False)
    o = _window_attention(q, k, v, kx.reshape(b, lc, kw), vx.reshape(b, lc, kw), attn_sink[0])
    x = _proj_residual(x, o, od_w_out[0].astype(BF16), ml[1][2])

    h2, gates, rank, rankt, cnt = _route(x, norm2_g[1], ml[1][3], ml[1][4], moe_router[0])
    n_slots, off, cnt, zero_blk, block_e, used = _slot_layout(cnt[:, 0, :N_EXPERTS], MOE_ROWS)
    xs = _dispatch(h2, rankt, n_slots, off, cnt, zero_blk, MOE_ROWS)
    ys = _expert_ffn(xs, block_e, used, moe_w_gate[0].astype(BF16), moe_w_up[0].astype(BF16),
                     moe_w_down[0].astype(BF16), MOE_ROWS)
    return _moe_combine(x, ml[1][5], gates, rank, ys, off, cnt)
```
